```python
import math
import jax, jax.numpy as jnp
from jax import lax
import numpy as np

D_MODEL = 1024
BATCH = 32
SEQ = 256
DEPTH = 1
DEC_BATCH = 2
DEC_SEQ = 4096
PAST_LEN = 256

GRID_W = 64
N_HEADS = 8
N_KV_HEADS = 2
HEAD_DIM = 128
GROUP = N_HEADS // N_KV_HEADS
ATTN_WIDTH = N_HEADS * HEAD_DIM
KV_WIDTH = N_KV_HEADS * HEAD_DIM
POOL_WINDOWS = (2, 4, 8, 16)
N_POOL_GROUPS = 4
POOL_GROUP_DIM = 128
POOL_WIDTH = N_POOL_GROUPS * POOL_GROUP_DIM
IN_WIDTH = ATTN_WIDTH + 2 * KV_WIDTH + POOL_WIDTH + 2 * D_MODEL
SPLITS = (ATTN_WIDTH, ATTN_WIDTH + KV_WIDTH, ATTN_WIDTH + 2 * KV_WIDTH,
          ATTN_WIDTH + 2 * KV_WIDTH + POOL_WIDTH, ATTN_WIDTH + 2 * KV_WIDTH + POOL_WIDTH + D_MODEL)
D_FF = 2816
N_MOD = 9
ROPE_THETA = 10000.0
Q_BLOCK = 128
LN_EPS = 1e-6
RMS_EPS = 1e-6
ALPHA = (2.0 * DEPTH) ** 0.25
BETA = (8.0 * DEPTH) ** -0.25

kernel_name = "hybrid_flow_prefix_gqa_pool_macaron"


def _layer_norm(x, g, b):
    xf = x.astype(jnp.float32)
    mu = jnp.mean(xf, axis=-1, keepdims=True)
    var = jnp.mean(jnp.square(xf - mu), axis=-1, keepdims=True)
    return ((xf - mu) * lax.rsqrt(var + LN_EPS)).astype(x.dtype) * g + b


def _rms_norm(x, g):
    xf = x.astype(jnp.float32)
    ms = jnp.mean(jnp.square(xf), axis=-1, keepdims=True)
    return (xf * lax.rsqrt(ms + RMS_EPS)).astype(x.dtype) * g


def _modulation(cond, w_mod, b_mod):
    m = jax.nn.silu(cond) @ w_mod + b_mod
    return m.reshape(cond.shape[0], N_MOD, D_MODEL)


def _modulate(x, shift, scale):
    return x * (1.0 + scale) + shift


def _swiglu(h, w1, w2):
    a, b = jnp.split(h @ w1, 2, axis=-1)
    return (jax.nn.silu(a) * b) @ w2


def _rope_tables(n_tokens):
    t = jnp.arange(n_tokens, dtype=jnp.int32)
    row = (t // GRID_W).astype(jnp.float32)
    col = (t % GRID_W).astype(jnp.float32)
    n_freq = HEAD_DIM // 4
    inv_freq = ROPE_THETA ** (-jnp.arange(n_freq, dtype=jnp.float32) / n_freq)
    ang = jnp.concatenate([row[:, None] * inv_freq, col[:, None] * inv_freq], axis=-1)
    return jnp.cos(ang), jnp.sin(ang)


def _apply_rope(x, cos, sin):
    half = HEAD_DIM // 2
    x1, x2 = x[..., :half], x[..., half:]
    c = cos[None, :, None, :].astype(x.dtype)
    s = sin[None, :, None, :].astype(x.dtype)
    return jnp.concatenate([x1 * c - x2 * s, x1 * s + x2 * c], axis=-1)


def _ctx_attention(q, k, v):
    B, S = q.shape[0], q.shape[1]
    qg = q.reshape(B, S, N_KV_HEADS, GROUP, HEAD_DIM)
    s = jnp.einsum('bskgd,btkd->bkgst', qg, k).astype(jnp.float32) * (HEAD_DIM ** -0.5)
    p = jax.nn.softmax(s, axis=-1).astype(v.dtype)
    o = jnp.einsum('bkgst,btkd->bskgd', p, v)
    return o.reshape(B, S, ATTN_WIDTH)


def _latent_attention(q, k_lat, v_lat, k_ctx, v_ctx):
    B, N = q.shape[0], q.shape[1]
    k_all = jnp.concatenate([k_ctx.astype(k_lat.dtype), k_lat], axis=1)
    v_all = jnp.concatenate([v_ctx.astype(v_lat.dtype), v_lat], axis=1)
    n_blk = N // Q_BLOCK
    qb = q.reshape(B, n_blk, Q_BLOCK, N_KV_HEADS, GROUP, HEAD_DIM).transpose(1, 0, 2, 3, 4, 5)

    def block(qblk):
        s = jnp.einsum('bqkgd,btkd->bkgqt', qblk, k_all).astype(jnp.float32) * (HEAD_DIM ** -0.5)
        p = jax.nn.softmax(s, axis=-1).astype(v_all.dtype)
        return jnp.einsum('bkgqt,btkd->bqkgd', p, v_all)

    o = lax.map(block, qb)
    return o.transpose(1, 0, 2, 3, 4, 5).reshape(B, N, ATTN_WIDTH)


def _multiscale_pool(p):
    B, N, _ = p.shape
    cs = jnp.cumsum(p.astype(jnp.float32), axis=1)
    cs = jnp.concatenate([jnp.zeros((B, 1, POOL_WIDTH), jnp.float32), cs], axis=1)
    t = jnp.arange(N, dtype=jnp.int32)
    outs = []
    for g, w in enumerate(POOL_WINDOWS):
        lo = jnp.clip(t - w // 2, 0, N)
        hi = jnp.clip(t - w // 2 + w, 0, N)
        seg = cs[:, :, g * POOL_GROUP_DIM:(g + 1) * POOL_GROUP_DIM]
        cnt = (hi - lo).astype(jnp.float32)[None, :, None]
        outs.append((seg[:, hi] - seg[:, lo]) / cnt)
    pooled = jnp.concatenate(outs, axis=-1).astype(p.dtype)
    return pooled - p


def _mixer(h, lw, rope, kv_ctx):
    B, N, _ = h.shape
    proj = h @ lw['w_in']
    q, k, v, pin, ga, gp = jnp.split(proj, SPLITS, axis=-1)
    q = _rms_norm(q.reshape(B, N, N_HEADS, HEAD_DIM), lw['q_norm_g'])
    k = _rms_norm(k.reshape(B, N, N_KV_HEADS, HEAD_DIM), lw['k_norm_g'])
    v = v.reshape(B, N, N_KV_HEADS, HEAD_DIM)
    if rope is None:
        attn = _ctx_attention(q, k, v)
    else:
        cos, sin = rope
        attn = _latent_attention(_apply_rope(q, cos, sin), _apply_rope(k, cos, sin), v, kv_ctx[0], kv_ctx[1])
    pooled = _multiscale_pool(pin).reshape(B, N, N_POOL_GROUPS, POOL_GROUP_DIM)
    pooled = jnp.einsum('bngc,gcd->bngd', pooled, lw['pool_w']).reshape(B, N, POOL_WIDTH) * lw['pool_scale']
    merged = jax.nn.sigmoid(ga) * (attn @ lw['w_up_attn']) + jax.nn.sigmoid(gp) * (pooled @ lw['w_up_pool'])
    return merged @ lw['w_out'], k, v


def _layer(x, mod, lw, rope, kv_ctx):
    sh1, sc1, g1, sh2, sc2, g2, sh3, sc3, g3 = [mod[:, i][:, None, :] for i in range(N_MOD)]
    f1 = _swiglu(_modulate(x, sh1, sc1), lw['ffn1_w1'], lw['ffn1_w2'])
    x = _layer_norm(ALPHA * x + 0.5 * g1 * f1, lw['ln_g'][0], lw['ln_b'][0])
    mix, k, v = _mixer(_modulate(x, sh2, sc2), lw, rope, kv_ctx)
    x = _layer_norm(ALPHA * x + g2 * mix, lw['ln_g'][1], lw['ln_b'][1])
    f2 = _swiglu(_modulate(x, sh3, sc3), lw['ffn2_w1'], lw['ffn2_w2'])
    x = _layer_norm(ALPHA * x + 0.5 * g3 * f2, lw['ln_g'][2], lw['ln_b'][2])
    return x, k, v


def setup_inputs(seed: int = 0) -> dict:
    key = jax.random.key(seed)
    ks = jax.random.split(key, 24)
    f32 = jnp.float32

    def nrm(k, shape, scale):
        return jax.random.normal(k, shape, f32) * scale

    return {
        'x_prompt': nrm(ks[0], (BATCH, SEQ, D_MODEL), 1.0),
        'x_sample': nrm(ks[1], (DEC_BATCH, DEC_SEQ, D_MODEL), 1.0),
        'cache_k': nrm(ks[2], (DEC_BATCH, DEPTH, PAST_LEN, N_KV_HEADS, HEAD_DIM), 1.0),
        'cache_v': nrm(ks[3], (DEC_BATCH, DEPTH, PAST_LEN, N_KV_HEADS, HEAD_DIM), 1.0),
        'c': nrm(ks[4], (DEC_BATCH, D_MODEL), 1.0),
        'c_ctx': nrm(ks[5], (D_MODEL,), 1.0),
        'w_mod': nrm(ks[6], (DEPTH, D_MODEL, N_MOD * D_MODEL), 0.1 * D_MODEL ** -0.5),
        'b_mod': nrm(ks[7], (DEPTH, N_MOD * D_MODEL), 0.01),
        'ln_g': 1.0 + nrm(ks[8], (DEPTH, 3, D_MODEL), 0.02),
        'ln_b': nrm(ks[9], (DEPTH, 3, D_MODEL), 0.02),
        'ffn1_w1': nrm(ks[10], (DEPTH, D_MODEL, 2 * D_FF), D_MODEL ** -0.5),
        'ffn1_w2': nrm(ks[11], (DEPTH, D_FF, D_MODEL), BETA * D_FF ** -0.5),
        'w_in': nrm(ks[12], (DEPTH, D_MODEL, IN_WIDTH), D_MODEL ** -0.5),
        'q_norm_g': 1.0 + nrm(ks[13], (DEPTH, HEAD_DIM), 0.02),
        'k_norm_g': 1.0 + nrm(ks[14], (DEPTH, HEAD_DIM), 0.02),
        'pool_w': nrm(ks[15], (DEPTH, N_POOL_GROUPS, POOL_GROUP_DIM, POOL_GROUP_DIM), POOL_GROUP_DIM ** -0.5),
        'pool_scale': 1.0 + nrm(ks[16], (DEPTH, POOL_WIDTH), 0.1),
        'w_up_attn': nrm(ks[17], (DEPTH, ATTN_WIDTH, D_MODEL), ATTN_WIDTH ** -0.5),
        'w_up_pool': nrm(ks[18], (DEPTH, POOL_WIDTH, D_MODEL), POOL_WIDTH ** -0.5),
        'w_out': nrm(ks[19], (DEPTH, D_MODEL, D_MODEL), BETA * D_MODEL ** -0.5),
        'ffn2_w1': nrm(ks[20], (DEPTH, D_MODEL, 2 * D_FF), D_MODEL ** -0.5),
        'ffn2_w2': nrm(ks[21], (DEPTH, D_FF, D_MODEL), BETA * D_FF ** -0.5),
    }


def reference(x_prompt, x_sample, cache_k, cache_v, c, c_ctx, w_mod, b_mod, ln_g, ln_b,
              ffn1_w1, ffn1_w2, w_in, q_norm_g, k_norm_g, pool_w, pool_scale,
              w_up_attn, w_up_pool, w_out, ffn2_w1, ffn2_w2):
    n_lat = x_sample.shape[1]
    rows = n_lat // GRID_W
    rope = _rope_tables(rows * GRID_W)
    yp, ys = x_prompt, x_sample
    new_k, new_v = [], []
    for l in range(DEPTH):
        lw = {'ln_g': ln_g[l], 'ln_b': ln_b[l], 'ffn1_w1': ffn1_w1[l], 'ffn1_w2': ffn1_w2[l],
              'w_in': w_in[l], 'q_norm_g': q_norm_g[l], 'k_norm_g': k_norm_g[l],
              'pool_w': pool_w[l], 'pool_scale': pool_scale[l], 'w_up_attn': w_up_attn[l],
              'w_up_pool': w_up_pool[l], 'w_out': w_out[l], 'ffn2_w1': ffn2_w1[l], 'ffn2_w2': ffn2_w2[l]}
        mod_ctx = _modulation(c_ctx[None, :], w_mod[l], b_mod[l])
        mod_lat = _modulation(c, w_mod[l], b_mod[l])
        yp, k_l, v_l = _layer(yp, mod_ctx, lw, None, None)
        ys, _, _ = _layer(ys, mod_lat, lw, rope, (cache_k[:, l], cache_v[:, l]))
        new_k.append(k_l)
        new_v.append(v_l)
    new_cache_k = jnp.stack(new_k, axis=1)
    new_cache_v = jnp.stack(new_v, axis=1)
    return (yp, ys, new_cache_k, new_cache_v)
```

```python
import functools
import math

import jax
import jax.numpy as jnp
from jax import lax
from jax.experimental import pallas as pl
from jax.experimental.pallas import tpu as pltpu

D_MODEL = 1024
GRID_W = 64
N_HEADS = 8
N_KV_HEADS = 2
HEAD_DIM = 128
GROUP = N_HEADS // N_KV_HEADS
ATTN_WIDTH = N_HEADS * HEAD_DIM
KV_WIDTH = N_KV_HEADS * HEAD_DIM
POOL_WINDOWS = (2, 4, 8, 16)
N_POOL_GROUPS = 4
POOL_GROUP_DIM = 128
POOL_WIDTH = N_POOL_GROUPS * POOL_GROUP_DIM
QKV_WIDTH = ATTN_WIDTH + 2 * KV_WIDTH
D_FF = 2816
N_MOD = 9
ROPE_THETA = 10000.0
LN_EPS = 1e-6
RMS_EPS = 1e-6

SUBLANES = 8
LANES = 128
MXU_DIM = 256
VMEM_LIMIT_BYTES = 56 * 1024 * 1024

HALO = SUBLANES
MOD_ROWS = SUBLANES
TOKEN_TILE = 256
Q_TILE = 128
FF_CHUNKS = ((0, 1024), (1024, 1024), (2048, 768))

BF16 = jnp.bfloat16
F32 = jnp.float32


def _dot(a, b):
    return jnp.dot(a, b, preferred_element_type=F32)


def _dot_nt(a, b):
    return lax.dot_general(a, b, (((1,), (1,)), ((), ())), preferred_element_type=F32)


def _layer_norm(x, g, b):
    mu = jnp.mean(x, axis=-1, keepdims=True)
    xc = x - mu
    var = jnp.mean(xc * xc, axis=-1, keepdims=True)
    return xc * lax.rsqrt(var + LN_EPS) * g + b


def _modulate_bf16(x, shift, scale):
    return (x * (1.0 + scale) + shift).astype(BF16)


def _resident(shape):
    zeros = (0,) * len(shape)
    return pl.BlockSpec(shape, lambda *_: zeros, pipeline_mode=pl.Buffered(1))


def _compiler_params(n_axes):
    return pltpu.CompilerParams(
        dimension_semantics=("arbitrary",) * n_axes,
        vmem_limit_bytes=VMEM_LIMIT_BYTES,
    )


def _mod_kernel(cond_ref, w_ref, b_ref, o_ref):
    c = cond_ref[...]
    s = (c * jax.nn.sigmoid(c)).astype(BF16)
    o_ref[...] = _dot(s, w_ref[...].astype(BF16)) + b_ref[...]


def _modulation_table(cond, w_mod, b_mod):
    n_out = w_mod.shape[1]
    bn = 1152
    return pl.pallas_call(
        _mod_kernel,
        grid=(n_out // bn,),
        in_specs=[
            pl.BlockSpec((MOD_ROWS, D_MODEL), lambda j: (0, 0)),
            pl.BlockSpec((D_MODEL, bn), lambda j: (0, j)),
            pl.BlockSpec((1, bn), lambda j: (0, j)),
        ],
        out_specs=pl.BlockSpec((MOD_ROWS, bn), lambda j: (0, j)),
        out_shape=jax.ShapeDtypeStruct((MOD_ROWS, n_out), F32),
        compiler_params=_compiler_params(1),
        name="modulation",
    )(cond, w_mod, b_mod.reshape(1, n_out))


def _ffn_body(x, shift, scale, gate, w1_ref, w2_ref, ln_g, ln_b, alpha):
    h = _modulate_bf16(x, shift, scale)
    f = None
    for start, width in FF_CHUNKS:
        a = _dot(h, w1_ref[:, start:start + width])
        b = _dot(h, w1_ref[:, D_FF + start:D_FF + start + width])
        u = (a * jax.nn.sigmoid(a) * b).astype(BF16)
        part = _dot(u, w2_ref[start:start + width, :])
        f = part if f is None else f + part
    return _layer_norm(alpha * x + (0.5 * gate) * f, ln_g, ln_b)


def _ffn_kernel(x_ref, mod_ref, w1_ref, w2_ref, lng_ref, lnb_ref, o_ref, *, mod_base, alpha):
    o_ref[...] = _ffn_body(
        x_ref[...],
        mod_ref[mod_base:mod_base + 1, :],
        mod_ref[mod_base + 1:mod_base + 2, :],
        mod_ref[mod_base + 2:mod_base + 3, :],
        w1_ref, w2_ref, lng_ref[...], lnb_ref[...], alpha)


def _ffn(x2d, mod, mod_row_of_tile, w1, w2, ln_g, ln_b, *, mod_base, alpha):
    n_tok = x2d.shape[0]
    tm = TOKEN_TILE
    return pl.pallas_call(
        functools.partial(_ffn_kernel, mod_base=mod_base, alpha=alpha),
        grid=(n_tok // tm,),
        in_specs=[
            pl.BlockSpec((tm, D_MODEL), lambda i: (i, 0)),
            pl.BlockSpec((None, N_MOD, D_MODEL), lambda i: (mod_row_of_tile(i), 0, 0)),
            _resident(w1.shape),
            _resident(w2.shape),
            _resident((1, D_MODEL)),
            _resident((1, D_MODEL)),
        ],
        out_specs=pl.BlockSpec((tm, D_MODEL), lambda i: (i, 0)),
        out_shape=jax.ShapeDtypeStruct((n_tok, D_MODEL), F32),
        compiler_params=_compiler_params(1),
        name="ffn_half_step",
    )(x2d, mod, w1, w2, ln_g.reshape(1, D_MODEL), ln_b.reshape(1, D_MODEL))


def _rms_head(xh, g):
    ms = jnp.mean(xh * xh, axis=-1, keepdims=True)
    return xh * lax.rsqrt(ms + RMS_EPS) * g


def _rope(xh, cos2, sin2):
    return xh * cos2 + pltpu.roll(xh, HEAD_DIM // 2, axis=1) * sin2


def _qkv_ctx_kernel(x_ref, mod_ref, w_ref, qg_ref, kg_ref,
                    q_ref, k_ref, v_ref, kb_ref, vb_ref):
    h = _modulate_bf16(x_ref[...], mod_ref[3:4, :], mod_ref[4:5, :])
    qkv = _dot(h, w_ref[...])
    qg = qg_ref[...] * (HEAD_DIM ** -0.5)
    kg = kg_ref[...]
    for hh in range(N_HEADS):
        sl = slice(hh * HEAD_DIM, (hh + 1) * HEAD_DIM)
        q_ref[:, sl] = _rms_head(qkv[:, sl], qg).astype(BF16)
    for hh in range(N_KV_HEADS):
        sl = slice(hh * HEAD_DIM, (hh + 1) * HEAD_DIM)
        kn = _rms_head(qkv[:, ATTN_WIDTH + hh * HEAD_DIM:ATTN_WIDTH + (hh + 1) * HEAD_DIM], kg)
        k_ref[:, sl] = kn
        kb_ref[:, sl] = kn.astype(BF16)
    v = qkv[:, ATTN_WIDTH + KV_WIDTH:]
    v_ref[...] = v
    vb_ref[...] = v.astype(BF16)


def _qkv_lat_kernel(x_ref, mod_ref, w_ref, qg_ref, kg_ref, cos_ref, sin_ref,
                    q_ref, kb_ref, vb_ref):
    h = _modulate_bf16(x_ref[...], mod_ref[3:4, :], mod_ref[4:5, :])
    qkv = _dot(h, w_ref[...])
    qg = qg_ref[...] * (HEAD_DIM ** -0.5)
    kg = kg_ref[...]
    cos2 = cos_ref[...]
    sin2 = sin_ref[...]
    for hh in range(N_HEADS):
        sl = slice(hh * HEAD_DIM, (hh + 1) * HEAD_DIM)
        q_ref[:, sl] = _rope(_rms_head(qkv[:, sl], qg), cos2, sin2).astype(BF16)
    for hh in range(N_KV_HEADS):
        sl = slice(hh * HEAD_DIM, (hh + 1) * HEAD_DIM)
        kn = _rms_head(qkv[:, ATTN_WIDTH + hh * HEAD_DIM:ATTN_WIDTH + (hh + 1) * HEAD_DIM], kg)
        kb_ref[:, sl] = _rope(kn, cos2, sin2).astype(BF16)
    vb_ref[...] = qkv[:, ATTN_WIDTH + KV_WIDTH:].astype(BF16)


def _qkv_ctx(x2d, mod, w_qkv, q_g, k_g):
    n_tok = x2d.shape[0]
    tm = TOKEN_TILE
    tok = lambda width: pl.BlockSpec((tm, width), lambda i: (i, 0))
    return pl.pallas_call(
        _qkv_ctx_kernel,
        grid=(n_tok // tm,),
        in_specs=[
            tok(D_MODEL),
            pl.BlockSpec((None, N_MOD, D_MODEL), lambda i: (0, 0, 0)),
            _resident(w_qkv.shape),
            _resident((1, HEAD_DIM)),
            _resident((1, HEAD_DIM)),
        ],
        out_specs=[tok(ATTN_WIDTH), tok(KV_WIDTH), tok(KV_WIDTH), tok(KV_WIDTH), tok(KV_WIDTH)],
        out_shape=[
            jax.ShapeDtypeStruct((n_tok, ATTN_WIDTH), BF16),
            jax.ShapeDtypeStruct((n_tok, KV_WIDTH), F32),
            jax.ShapeDtypeStruct((n_tok, KV_WIDTH), F32),
            jax.ShapeDtypeStruct((n_tok, KV_WIDTH), BF16),
            jax.ShapeDtypeStruct((n_tok, KV_WIDTH), BF16),
        ],
        compiler_params=_compiler_params(1),
        name="qkv_ctx",
    )(x2d, mod, w_qkv, q_g.reshape(1, HEAD_DIM), k_g.reshape(1, HEAD_DIM))


def _qkv_lat(x2d, mod, w_qkv, q_g, k_g, cos2, sin2, n_lat):
    n_tok = x2d.shape[0]
    tm = TOKEN_TILE
    tiles_per_seq = n_lat // tm
    tok = lambda width: pl.BlockSpec((tm, width), lambda i: (i, 0))
    rope_spec = pl.BlockSpec((tm, HEAD_DIM), lambda i: (i % tiles_per_seq, 0))
    return pl.pallas_call(
        _qkv_lat_kernel,
        grid=(n_tok // tm,),
        in_specs=[
            tok(D_MODEL),
            pl.BlockSpec((None, N_MOD, D_MODEL), lambda i: (1 + i // tiles_per_seq, 0, 0)),
            _resident(w_qkv.shape),
            _resident((1, HEAD_DIM)),
            _resident((1, HEAD_DIM)),
            rope_spec,
            rope_spec,
        ],
        out_specs=[tok(ATTN_WIDTH), tok(KV_WIDTH), tok(KV_WIDTH)],
        out_shape=[
            jax.ShapeDtypeStruct((n_tok, ATTN_WIDTH), BF16),
            jax.ShapeDtypeStruct((n_tok, KV_WIDTH), BF16),
            jax.ShapeDtypeStruct((n_tok, KV_WIDTH), BF16),
        ],
        compiler_params=_compiler_params(1),
        name="qkv_lat",
    )(x2d, mod, w_qkv, q_g.reshape(1, HEAD_DIM), k_g.reshape(1, HEAD_DIM), cos2, sin2)


def _stack_group_heads(q_blk):
    return jnp.concatenate(
        [q_blk[:, g * HEAD_DIM:(g + 1) * HEAD_DIM] for g in range(GROUP)], axis=0)


def _attn_ctx_kernel(q_ref, k_ref, v_ref, o_ref):
    t = q_ref.shape[0]
    for kv in range(N_KV_HEADS):
        q4 = _stack_group_heads(q_ref[:, kv * GROUP * HEAD_DIM:(kv + 1) * GROUP * HEAD_DIM])
        ksl = slice(kv * HEAD_DIM, (kv + 1) * HEAD_DIM)
        s = _dot_nt(q4, k_ref[:, ksl])
        m = jnp.max(s, axis=-1, keepdims=True)
        p = jnp.exp(s - m)
        l = jnp.sum(p, axis=-1, keepdims=True)
        o = _dot(p.astype(BF16), v_ref[:, ksl]) / l
        for g in range(GROUP):
            col = (kv * GROUP + g) * HEAD_DIM
            o_ref[:, col:col + HEAD_DIM] = o[g * t:(g + 1) * t].astype(BF16)


def _attn_ctx(q, kb, vb, seq):
    n_tok = q.shape[0]
    tok = lambda width: pl.BlockSpec((seq, width), lambda b: (b, 0))
    return pl.pallas_call(
        _attn_ctx_kernel,
        grid=(n_tok // seq,),
        in_specs=[tok(ATTN_WIDTH), tok(KV_WIDTH), tok(KV_WIDTH)],
        out_specs=tok(ATTN_WIDTH),
        out_shape=jax.ShapeDtypeStruct((n_tok, ATTN_WIDTH), BF16),
        compiler_params=_compiler_params(1),
        name="attn_ctx",
    )(q, kb, vb)


def _attn_lat_kernel(q_ref, kc_ref, vc_ref, kl_ref, vl_ref, o_ref):
    t = q_ref.shape[0]
    q4 = _stack_group_heads(q_ref[...])
    kc = kc_ref[...].astype(BF16)
    vc = vc_ref[...].astype(BF16)
    s_c = _dot_nt(q4, kc)
    s_l = _dot_nt(q4, kl_ref[...])
    m = jnp.maximum(jnp.max(s_c, axis=-1, keepdims=True), jnp.max(s_l, axis=-1, keepdims=True))
    p_c = jnp.exp(s_c - m)
    p_l = jnp.exp(s_l - m)
    l = jnp.sum(p_c, axis=-1, keepdims=True) + jnp.sum(p_l, axis=-1, keepdims=True)
    o = (_dot(p_c.astype(BF16), vc) + _dot(p_l.astype(BF16), vl_ref[...])) / l
    for g in range(GROUP):
        o_ref[:, g * HEAD_DIM:(g + 1) * HEAD_DIM] = o[g * t:(g + 1) * t].astype(BF16)


def _attn_lat(q, kb, vb, cache_k, cache_v, n_lat):
    n_tok = q.shape[0]
    n_batch = n_tok // n_lat
    past = cache_k.shape[1]
    tq = Q_TILE
    q_tiles = n_lat // tq
    gw = GROUP * HEAD_DIM
    return pl.pallas_call(
        _attn_lat_kernel,
        grid=(n_batch, N_KV_HEADS, q_tiles),
        in_specs=[
            pl.BlockSpec((tq, gw), lambda b, kv, i: (b * q_tiles + i, kv)),
            pl.BlockSpec((None, past, HEAD_DIM), lambda b, kv, i: (b, 0, kv)),
            pl.BlockSpec((None, past, HEAD_DIM), lambda b, kv, i: (b, 0, kv)),
            pl.BlockSpec((n_lat, HEAD_DIM), lambda b, kv, i: (b, kv)),
            pl.BlockSpec((n_lat, HEAD_DIM), lambda b, kv, i: (b, kv)),
        ],
        out_specs=pl.BlockSpec((tq, gw), lambda b, kv, i: (b * q_tiles + i, kv)),
        out_shape=jax.ShapeDtypeStruct((n_tok, ATTN_WIDTH), BF16),
        compiler_params=_compiler_params(3),
        name="attn_lat",
    )(q, cache_k, cache_v, kb, vb)


def _window_sums(e):
    n = e.shape[0]
    back = lambda a, k: pltpu.roll(a, k, axis=0)
    fwd = lambda a, k: pltpu.roll(a, n - k, axis=0)
    c2 = e + back(e, 1)
    c4 = back(c2, 1) + fwd(c2, 1)
    c8 = back(c4, 2) + fwd(c4, 2)
    c16 = back(c8, 4) + fwd(c8, 4)
    return c2, c4, c8, c16


def _mix_kernel(x_ref, xp_ref, xn_ref, a_ref, mod_ref, wpg_ref, poolw_ref, pscale_ref,
                wua_ref, wup_ref, wo_ref, lng_ref, lnb_ref, o_ref, *, tiles_per_seq, alpha):
    tm = x_ref.shape[0]
    i = pl.program_id(0)
    pos = i % tiles_per_seq
    seq_len = tiles_per_seq * tm
    shift = mod_ref[3:4, :]
    scale = mod_ref[4:5, :]
    gate = mod_ref[5:6, :]
    x = x_ref[...]

    x_ext = jnp.concatenate([xp_ref[...], x, xn_ref[...]], axis=0)
    p_ext = _dot(_modulate_bf16(x_ext, shift, scale), wpg_ref[:, :POOL_WIDTH])
    row = lax.broadcasted_iota(jnp.int32, (tm + 2 * HALO, 1), 0)
    first_row = jnp.where(pos == 0, HALO, 0)
    end_row = jnp.where(pos == tiles_per_seq - 1, HALO + tm, tm + 2 * HALO)
    outside = jnp.logical_or(row < first_row, row >= end_row)
    p_ext = jnp.where(outside, 0.0, p_ext)

    t = pos * tm + lax.broadcasted_iota(jnp.int32, (tm, 1), 0)
    pooled_up = None
    for g, w in enumerate(POOL_WINDOWS):
        sl = slice(g * POOL_GROUP_DIM, (g + 1) * POOL_GROUP_DIM)
        e = p_ext[:, sl]
        win = _window_sums(e)[g][HALO:HALO + tm]
        lo = jnp.clip(t - w // 2, 0, seq_len)
        hi = jnp.clip(t - w // 2 + w, 0, seq_len)
        cnt = (hi - lo).astype(F32)
        pooled = win / cnt - e[HALO:HALO + tm]
        pg = _dot(pooled.astype(BF16), poolw_ref[g]) * pscale_ref[:, sl]
        part = _dot(pg.astype(BF16), wup_ref[sl, :])
        pooled_up = part if pooled_up is None else pooled_up + part

    h = _modulate_bf16(x, shift, scale)
    ga = _dot(h, wpg_ref[:, POOL_WIDTH:POOL_WIDTH + D_MODEL])
    gp = _dot(h, wpg_ref[:, POOL_WIDTH + D_MODEL:])
    attn_up = _dot(a_ref[...], wua_ref[...])
    merged = jax.nn.sigmoid(ga) * attn_up + jax.nn.sigmoid(gp) * pooled_up
    mix = _dot(merged.astype(BF16), wo_ref[...])
    o_ref[...] = _layer_norm(alpha * x + gate * mix, lng_ref[...], lnb_ref[...])


def _mix(x2d, attn, mod, mod_row_of_tile, w_pg, pool_w, pool_scale, w_up_attn, w_up_pool, w_out,
         ln_g, ln_b, *, seq_len, alpha):
    n_tok = x2d.shape[0]
    tm = TOKEN_TILE
    tiles_per_seq = seq_len // tm
    halo_per_tile = tm // HALO
    n_halo_blocks = n_tok // HALO
    return pl.pallas_call(
        functools.partial(_mix_kernel, tiles_per_seq=tiles_per_seq, alpha=alpha),
        grid=(n_tok // tm,),
        in_specs=[
            pl.BlockSpec((tm, D_MODEL), lambda i: (i, 0)),
            pl.BlockSpec((HALO, D_MODEL), lambda i: (jnp.maximum(i * halo_per_tile - 1, 0), 0)),
            pl.BlockSpec((HALO, D_MODEL),
                         lambda i: (jnp.minimum((i + 1) * halo_per_tile, n_halo_blocks - 1), 0)),
            pl.BlockSpec((tm, ATTN_WIDTH), lambda i: (i, 0)),
            pl.BlockSpec((None, N_MOD, D_MODEL), lambda i: (mod_row_of_tile(i), 0, 0)),
            _resident(w_pg.shape),
            _resident(pool_w.shape),
            _resident((1, POOL_WIDTH)),
            _resident(w_up_attn.shape),
            _resident(w_up_pool.shape),
            _resident(w_out.shape),
            _resident((1, D_MODEL)),
            _resident((1, D_MODEL)),
        ],
        out_specs=pl.BlockSpec((tm, D_MODEL), lambda i: (i, 0)),
        out_shape=jax.ShapeDtypeStruct((n_tok, D_MODEL), F32),
        compiler_params=_compiler_params(1),
        name="mixer_out",
    )(x2d, x2d, x2d, attn, mod, w_pg, pool_w, pool_scale.reshape(1, POOL_WIDTH),
      w_up_attn, w_up_pool, w_out, ln_g.reshape(1, D_MODEL), ln_b.reshape(1, D_MODEL))


def _rope_tables(n_tokens):
    t = jnp.arange(n_tokens, dtype=jnp.int32)
    row = (t // GRID_W).astype(F32)
    col = (t % GRID_W).astype(F32)
    n_freq = HEAD_DIM // 4
    inv_freq = ROPE_THETA ** (-jnp.arange(n_freq, dtype=F32) / n_freq)
    ang = jnp.concatenate([row[:, None] * inv_freq, col[:, None] * inv_freq], axis=-1)
    cos, sin = jnp.cos(ang), jnp.sin(ang)
    return jnp.concatenate([cos, cos], axis=-1), jnp.concatenate([-sin, sin], axis=-1)


def kernel(x_prompt, x_sample, cache_k, cache_v, c, c_ctx, w_mod, b_mod, ln_g, ln_b, ffn1_w1, ffn1_w2, w_in, q_norm_g, k_norm_g, pool_w, pool_scale, w_up_attn, w_up_pool, w_out, ffn2_w1, ffn2_w2):
    batch, seq, _ = x_prompt.shape
    dec_batch, n_lat, _ = x_sample.shape
    depth = w_mod.shape[0]
    past = cache_k.shape[2]
    alpha = (2.0 * depth) ** 0.25
    assert seq == TOKEN_TILE and n_lat % TOKEN_TILE == 0 and n_lat % Q_TILE == 0
    assert 1 + dec_batch <= MOD_ROWS

    cos2, sin2 = _rope_tables((n_lat // GRID_W) * GRID_W)
    cond = jnp.zeros((MOD_ROWS, D_MODEL), F32).at[0].set(c_ctx).at[1:1 + dec_batch].set(c)

    lat_tiles_per_seq = n_lat // TOKEN_TILE
    ctx_row = lambda i: 0
    lat_row = lambda i: 1 + i // lat_tiles_per_seq

    yp = x_prompt.reshape(batch * seq, D_MODEL)
    ys = x_sample.reshape(dec_batch * n_lat, D_MODEL)
    new_k, new_v = [], []
    for l in range(depth):
        w1a = ffn1_w1[l].astype(BF16)
        w2a = ffn1_w2[l].astype(BF16)
        w1b = ffn2_w1[l].astype(BF16)
        w2b = ffn2_w2[l].astype(BF16)
        w_in_b = w_in[l].astype(BF16)
        w_qkv = w_in_b[:, :QKV_WIDTH]
        w_pg = w_in_b[:, QKV_WIDTH:]
        pool_w_b = pool_w[l].astype(BF16)
        w_ua = w_up_attn[l].astype(BF16)
        w_upl = w_up_pool[l].astype(BF16)
        w_o = w_out[l].astype(BF16)

        mod = _modulation_table(cond, w_mod[l], b_mod[l]).reshape(MOD_ROWS, N_MOD, D_MODEL)

        yp = _ffn(yp, mod, ctx_row, w1a, w2a, ln_g[l, 0], ln_b[l, 0], mod_base=0, alpha=alpha)
        q, k_l, v_l, kb, vb = _qkv_ctx(yp, mod, w_qkv, q_norm_g[l], k_norm_g[l])
        attn = _attn_ctx(q, kb, vb, seq)
        yp = _mix(yp, attn, mod, ctx_row, w_pg, pool_w_b, pool_scale[l], w_ua, w_upl, w_o,
                  ln_g[l, 1], ln_b[l, 1], seq_len=seq, alpha=alpha)
        yp = _ffn(yp, mod, ctx_row, w1b, w2b, ln_g[l, 2], ln_b[l, 2], mod_base=6, alpha=alpha)
        new_k.append(k_l.reshape(batch, seq, N_KV_HEADS, HEAD_DIM))
        new_v.append(v_l.reshape(batch, seq, N_KV_HEADS, HEAD_DIM))

        ys = _ffn(ys, mod, lat_row, w1a, w2a, ln_g[l, 0], ln_b[l, 0], mod_base=0, alpha=alpha)
        q, kb, vb = _qkv_lat(ys, mod, w_qkv, q_norm_g[l], k_norm_g[l], cos2, sin2, n_lat)
        attn = _attn_lat(q, kb, vb,
                         cache_k[:, l].reshape(dec_batch, past, KV_WIDTH),
                         cache_v[:, l].reshape(dec_batch, past, KV_WIDTH), n_lat)
        ys = _mix(ys, attn, mod, lat_row, w_pg, pool_w_b, pool_scale[l], w_ua, w_upl, w_o,
                  ln_g[l, 1], ln_b[l, 1], seq_len=n_lat, alpha=alpha)
        ys = _ffn(ys, mod, lat_row, w1b, w2b, ln_g[l, 2], ln_b[l, 2], mod_base=6, alpha=alpha)

    return (yp.reshape(batch, seq, D_MODEL),
            ys.reshape(dec_batch, n_lat, D_MODEL),
            jnp.stack(new_k, axis=1),
            jnp.stack(new_v, axis=1))
```

```python
import functools
import math

import jax
import jax.numpy as jnp
from jax import lax
from jax.experimental import pallas as pl
from jax.experimental.pallas import tpu as pltpu

D_MODEL = 1024
GRID_W = 64
N_HEADS = 8
N_KV_HEADS = 2
HEAD_DIM = 128
GROUP = N_HEADS // N_KV_HEADS
ATTN_WIDTH = N_HEADS * HEAD_DIM
KV_WIDTH = N_KV_HEADS * HEAD_DIM
POOL_WINDOWS = (2, 4, 8, 16)
N_POOL_GROUPS = 4
POOL_GROUP_DIM = 128
POOL_WIDTH = N_POOL_GROUPS * POOL_GROUP_DIM
QKV_WIDTH = ATTN_WIDTH + 2 * KV_WIDTH
D_FF = 2816
N_MOD = 9
ROPE_THETA = 10000.0
LN_EPS = 1e-6
RMS_EPS = 1e-6

SUBLANES = 8
LANES = 128
MXU_DIM = 256
VMEM_LIMIT_BYTES = 56 * 1024 * 1024

HALO = SUBLANES
MOD_ROWS = SUBLANES
TOKEN_TILE = 256
Q_TILE = 256
KV_CHUNK = 256
Q_SCALE = HEAD_DIM ** -0.5 * math.log2(math.e)
V_AUG = MXU_DIM
FF_CHUNKS = ((0, 1024), (1024, 1024), (2048, 768))

BF16 = jnp.bfloat16
F32 = jnp.float32


def _dot(a, b):
    return jnp.dot(a, b, preferred_element_type=F32)


def _dot_nt(a, b):
    return lax.dot_general(a, b, (((1,), (1,)), ((), ())), preferred_element_type=F32)


def _layer_norm(x, g, b):
    mu = jnp.mean(x, axis=-1, keepdims=True)
    xc = x - mu
    var = jnp.mean(xc * xc, axis=-1, keepdims=True)
    return xc * lax.rsqrt(var + LN_EPS) * g + b


def _modulate_bf16(x, shift, scale):
    return (x * (1.0 + scale) + shift).astype(BF16)


def _resident(shape):
    zeros = (0,) * len(shape)
    return pl.BlockSpec(shape, lambda *_: zeros, pipeline_mode=pl.Buffered(1))


def _compiler_params(n_axes):
    return pltpu.CompilerParams(
        dimension_semantics=("arbitrary",) * n_axes,
        vmem_limit_bytes=VMEM_LIMIT_BYTES,
    )


def _mod_kernel(cond_ref, w_ref, b_ref, o_ref):
    c = cond_ref[...]
    s = (c * jax.nn.sigmoid(c)).astype(BF16)
    o_ref[...] = _dot(s, w_ref[...].astype(BF16)) + b_ref[...]


def _modulation_table(cond, w_mod, b_mod):
    n_out = w_mod.shape[1]
    bn = 1152
    return pl.pallas_call(
        _mod_kernel,
        grid=(n_out // bn,),
        in_specs=[
            pl.BlockSpec((MOD_ROWS, D_MODEL), lambda j: (0, 0)),
            pl.BlockSpec((D_MODEL, bn), lambda j: (0, j)),
            pl.BlockSpec((1, bn), lambda j: (0, j)),
        ],
        out_specs=pl.BlockSpec((MOD_ROWS, bn), lambda j: (0, j)),
        out_shape=jax.ShapeDtypeStruct((MOD_ROWS, n_out), F32),
        compiler_params=_compiler_params(1),
        name="modulation",
    )(cond, w_mod, b_mod.reshape(1, n_out))


def _ffn_body(x, shift, scale, gate, w1_ref, w2_ref, ln_g, ln_b, alpha):
    h = _modulate_bf16(x, shift, scale)
    f = None
    for start, width in FF_CHUNKS:
        a = _dot(h, w1_ref[:, start:start + width])
        b = _dot(h, w1_ref[:, D_FF + start:D_FF + start + width])
        u = (a * jax.nn.sigmoid(a) * b).astype(BF16)
        part = _dot(u, w2_ref[start:start + width, :])
        f = part if f is None else f + part
    return _layer_norm(alpha * x + (0.5 * gate) * f, ln_g, ln_b)


def _ffn_kernel(x_ref, mod_ref, w1_ref, w2_ref, lng_ref, lnb_ref, o_ref, *, mod_base, alpha):
    o_ref[...] = _ffn_body(
        x_ref[...],
        mod_ref[mod_base:mod_base + 1, :],
        mod_ref[mod_base + 1:mod_base + 2, :],
        mod_ref[mod_base + 2:mod_base + 3, :],
        w1_ref, w2_ref, lng_ref[...], lnb_ref[...], alpha)


def _ffn(x2d, mod, mod_row_of_tile, w1, w2, ln_g, ln_b, *, mod_base, alpha):
    n_tok = x2d.shape[0]
    tm = TOKEN_TILE
    return pl.pallas_call(
        functools.partial(_ffn_kernel, mod_base=mod_base, alpha=alpha),
        grid=(n_tok // tm,),
        in_specs=[
            pl.BlockSpec((tm, D_MODEL), lambda i: (i, 0)),
            pl.BlockSpec((None, N_MOD, D_MODEL), lambda i: (mod_row_of_tile(i), 0, 0)),
            _resident(w1.shape),
            _resident(w2.shape),
            _resident((1, D_MODEL)),
            _resident((1, D_MODEL)),
        ],
        out_specs=pl.BlockSpec((tm, D_MODEL), lambda i: (i, 0)),
        out_shape=jax.ShapeDtypeStruct((n_tok, D_MODEL), F32),
        compiler_params=_compiler_params(1),
        name="ffn_half_step",
    )(x2d, mod, w1, w2, ln_g.reshape(1, D_MODEL), ln_b.reshape(1, D_MODEL))


def _rms_head(xh, g):
    ms = jnp.mean(xh * xh, axis=-1, keepdims=True)
    return xh * lax.rsqrt(ms + RMS_EPS) * g


def _rope(xh, cos2, sin2):
    return xh * cos2 + pltpu.roll(xh, HEAD_DIM // 2, axis=1) * sin2


def _ones_column_block(rows):
    col = lax.broadcasted_iota(jnp.int32, (rows, V_AUG - HEAD_DIM), 1)
    return jnp.where(col == 0, 1.0, 0.0).astype(BF16)


def _qkv_ctx_kernel(x_ref, mod_ref, w_ref, qg_ref, kg_ref,
                    q_ref, k_ref, v_ref, kb_ref, vb_ref):
    h = _modulate_bf16(x_ref[...], mod_ref[3:4, :], mod_ref[4:5, :])
    qkv = _dot(h, w_ref[...])
    qg = qg_ref[...] * Q_SCALE
    kg = kg_ref[...]
    for hh in range(N_HEADS):
        sl = slice(hh * HEAD_DIM, (hh + 1) * HEAD_DIM)
        q_ref[:, sl] = _rms_head(qkv[:, sl], qg).astype(BF16)
    for hh in range(N_KV_HEADS):
        sl = slice(hh * HEAD_DIM, (hh + 1) * HEAD_DIM)
        kn = _rms_head(qkv[:, ATTN_WIDTH + hh * HEAD_DIM:ATTN_WIDTH + (hh + 1) * HEAD_DIM], kg)
        k_ref[:, sl] = kn
        kb_ref[:, sl] = kn.astype(BF16)
    v = qkv[:, ATTN_WIDTH + KV_WIDTH:]
    v_ref[...] = v
    vb_ref[...] = v.astype(BF16)


def _qkv_lat_kernel(x_ref, mod_ref, w_ref, qg_ref, kg_ref, cos_ref, sin_ref,
                    q_ref, kb_ref, vb_ref):
    h = _modulate_bf16(x_ref[...], mod_ref[3:4, :], mod_ref[4:5, :])
    qkv = _dot(h, w_ref[...])
    qg = qg_ref[...] * Q_SCALE
    kg = kg_ref[...]
    cos2 = cos_ref[...]
    sin2 = sin_ref[...]
    for hh in range(N_HEADS):
        sl = slice(hh * HEAD_DIM, (hh + 1) * HEAD_DIM)
        q_ref[:, sl] = _rope(_rms_head(qkv[:, sl], qg), cos2, sin2).astype(BF16)
    for hh in range(N_KV_HEADS):
        sl = slice(hh * HEAD_DIM, (hh + 1) * HEAD_DIM)
        kn = _rms_head(qkv[:, ATTN_WIDTH + hh * HEAD_DIM:ATTN_WIDTH + (hh + 1) * HEAD_DIM], kg)
        kb_ref[:, sl] = _rope(kn, cos2, sin2).astype(BF16)
    ones_col = _ones_column_block(x_ref.shape[0])
    for hh in range(N_KV_HEADS):
        v0 = ATTN_WIDTH + KV_WIDTH + hh * HEAD_DIM
        vb_ref[:, hh * V_AUG:hh * V_AUG + HEAD_DIM] = qkv[:, v0:v0 + HEAD_DIM].astype(BF16)
        vb_ref[:, hh * V_AUG + HEAD_DIM:(hh + 1) * V_AUG] = ones_col


def _qkv_ctx(x2d, mod, w_qkv, q_g, k_g):
    n_tok = x2d.shape[0]
    tm = TOKEN_TILE
    tok = lambda width: pl.BlockSpec((tm, width), lambda i: (i, 0))
    return pl.pallas_call(
        _qkv_ctx_kernel,
        grid=(n_tok // tm,),
        in_specs=[
            tok(D_MODEL),
            pl.BlockSpec((None, N_MOD, D_MODEL), lambda i: (0, 0, 0)),
            _resident(w_qkv.shape),
            _resident((1, HEAD_DIM)),
            _resident((1, HEAD_DIM)),
        ],
        out_specs=[tok(ATTN_WIDTH), tok(KV_WIDTH), tok(KV_WIDTH), tok(KV_WIDTH), tok(KV_WIDTH)],
        out_shape=[
            jax.ShapeDtypeStruct((n_tok, ATTN_WIDTH), BF16),
            jax.ShapeDtypeStruct((n_tok, KV_WIDTH), F32),
            jax.ShapeDtypeStruct((n_tok, KV_WIDTH), F32),
            jax.ShapeDtypeStruct((n_tok, KV_WIDTH), BF16),
            jax.ShapeDtypeStruct((n_tok, KV_WIDTH), BF16),
        ],
        compiler_params=_compiler_params(1),
        name="qkv_ctx",
    )(x2d, mod, w_qkv, q_g.reshape(1, HEAD_DIM), k_g.reshape(1, HEAD_DIM))


def _qkv_lat(x2d, mod, w_qkv, q_g, k_g, cos2, sin2, n_lat):
    n_tok = x2d.shape[0]
    tm = TOKEN_TILE
    tiles_per_seq = n_lat // tm
    tok = lambda width: pl.BlockSpec((tm, width), lambda i: (i, 0))
    rope_spec = pl.BlockSpec((tm, HEAD_DIM), lambda i: (i % tiles_per_seq, 0))
    return pl.pallas_call(
        _qkv_lat_kernel,
        grid=(n_tok // tm,),
        in_specs=[
            tok(D_MODEL),
            pl.BlockSpec((None, N_MOD, D_MODEL), lambda i: (1 + i // tiles_per_seq, 0, 0)),
            _resident(w_qkv.shape),
            _resident((1, HEAD_DIM)),
            _resident((1, HEAD_DIM)),
            rope_spec,
            rope_spec,
        ],
        out_specs=[tok(ATTN_WIDTH), tok(KV_WIDTH), tok(N_KV_HEADS * V_AUG)],
        out_shape=[
            jax.ShapeDtypeStruct((n_tok, ATTN_WIDTH), BF16),
            jax.ShapeDtypeStruct((n_tok, KV_WIDTH), BF16),
            jax.ShapeDtypeStruct((n_tok, N_KV_HEADS * V_AUG), BF16),
        ],
        compiler_params=_compiler_params(1),
        name="qkv_lat",
    )(x2d, mod, w_qkv, q_g.reshape(1, HEAD_DIM), k_g.reshape(1, HEAD_DIM), cos2, sin2)


def _stack_group_heads(q_blk):
    return jnp.concatenate(
        [q_blk[:, g * HEAD_DIM:(g + 1) * HEAD_DIM] for g in range(GROUP)], axis=0)


def _attn_ctx_kernel(q_ref, k_ref, v_ref, o_ref):
    t = q_ref.shape[0]
    for kv in range(N_KV_HEADS):
        q4 = _stack_group_heads(q_ref[:, kv * GROUP * HEAD_DIM:(kv + 1) * GROUP * HEAD_DIM])
        ksl = slice(kv * HEAD_DIM, (kv + 1) * HEAD_DIM)
        s = _dot_nt(q4, k_ref[:, ksl])
        m = jnp.max(s, axis=-1, keepdims=True)
        p = jnp.exp2(s - m)
        l = jnp.sum(p, axis=-1, keepdims=True)
        o = _dot(p.astype(BF16), v_ref[:, ksl]) / l
        for g in range(GROUP):
            col = (kv * GROUP + g) * HEAD_DIM
            o_ref[:, col:col + HEAD_DIM] = o[g * t:(g + 1) * t].astype(BF16)


def _attn_ctx(q, kb, vb, seq):
    n_tok = q.shape[0]
    tok = lambda width: pl.BlockSpec((seq, width), lambda b: (b, 0))
    return pl.pallas_call(
        _attn_ctx_kernel,
        grid=(n_tok // seq,),
        in_specs=[tok(ATTN_WIDTH), tok(KV_WIDTH), tok(KV_WIDTH)],
        out_specs=tok(ATTN_WIDTH),
        out_shape=jax.ShapeDtypeStruct((n_tok, ATTN_WIDTH), BF16),
        compiler_params=_compiler_params(1),
        name="attn_ctx",
    )(q, kb, vb)


def _attn_lat_kernel(q_ref, kc_ref, vc_ref, kl_ref, vl_ref, o_ref):
    t = q_ref.shape[0]
    n_lat = kl_ref.shape[0]
    past = kc_ref.shape[0]
    q4 = _stack_group_heads(q_ref[...])
    vc = jnp.concatenate([vc_ref[...].astype(BF16), _ones_column_block(past)], axis=1)
    chunks = [(kc_ref[...].astype(BF16), vc)]
    for c0 in range(0, n_lat, KV_CHUNK):
        chunks.append((kl_ref[c0:c0 + KV_CHUNK, :], vl_ref[c0:c0 + KV_CHUNK, :]))
    m = None
    acc = None
    for k_c, v_c in chunks:
        s = _dot_nt(q4, k_c)
        m_c = jnp.max(s, axis=-1, keepdims=True)
        if m is None:
            m = m_c
            acc = _dot(jnp.exp2(s - m).astype(BF16), v_c)
        else:
            m_new = jnp.maximum(m, m_c)
            acc = jnp.exp2(m - m_new) * acc + _dot(jnp.exp2(s - m_new).astype(BF16), v_c)
            m = m_new
    o = acc[:, :HEAD_DIM] / acc[:, HEAD_DIM:HEAD_DIM + 1]
    for g in range(GROUP):
        o_ref[:, g * HEAD_DIM:(g + 1) * HEAD_DIM] = o[g * t:(g + 1) * t].astype(BF16)


def _attn_lat(q, kb, vb, cache_k, cache_v, n_lat):
    n_tok = q.shape[0]
    n_batch = n_tok // n_lat
    past = cache_k.shape[1]
    tq = Q_TILE
    q_tiles = n_lat // tq
    gw = GROUP * HEAD_DIM
    return pl.pallas_call(
        _attn_lat_kernel,
        grid=(n_batch, N_KV_HEADS, q_tiles),
        in_specs=[
            pl.BlockSpec((tq, gw), lambda b, kv, i: (b * q_tiles + i, kv)),
            pl.BlockSpec((None, past, HEAD_DIM), lambda b, kv, i: (b, 0, kv)),
            pl.BlockSpec((None, past, HEAD_DIM), lambda b, kv, i: (b, 0, kv)),
            pl.BlockSpec((n_lat, HEAD_DIM), lambda b, kv, i: (b, kv)),
            pl.BlockSpec((n_lat, V_AUG), lambda b, kv, i: (b, kv)),
        ],
        out_specs=pl.BlockSpec((tq, gw), lambda b, kv, i: (b * q_tiles + i, kv)),
        out_shape=jax.ShapeDtypeStruct((n_tok, ATTN_WIDTH), BF16),
        compiler_params=_compiler_params(3),
        name="attn_lat",
    )(q, cache_k, cache_v, kb, vb)


def _window_sums(e):
    n = e.shape[0]
    back = lambda a, k: pltpu.roll(a, k, axis=0)
    fwd = lambda a, k: pltpu.roll(a, n - k, axis=0)
    c2 = e + back(e, 1)
    c4 = back(c2, 1) + fwd(c2, 1)
    c8 = back(c4, 2) + fwd(c4, 2)
    c16 = back(c8, 4) + fwd(c8, 4)
    return c2, c4, c8, c16


def _mix_kernel(x_ref, xp_ref, xn_ref, a_ref, mod_ref, wpg_ref, poolw_ref, pscale_ref,
                wua_ref, wup_ref, wo_ref, lng_ref, lnb_ref, o_ref, *, tiles_per_seq, alpha):
    tm = x_ref.shape[0]
    i = pl.program_id(0)
    pos = i % tiles_per_seq
    seq_len = tiles_per_seq * tm
    shift = mod_ref[3:4, :]
    scale = mod_ref[4:5, :]
    gate = mod_ref[5:6, :]
    x = x_ref[...]

    x_ext = jnp.concatenate([xp_ref[...], x, xn_ref[...]], axis=0)
    p_ext = _dot(_modulate_bf16(x_ext, shift, scale), wpg_ref[:, :POOL_WIDTH])
    row = lax.broadcasted_iota(jnp.int32, (tm + 2 * HALO, 1), 0)
    first_row = jnp.where(pos == 0, HALO, 0)
    end_row = jnp.where(pos == tiles_per_seq - 1, HALO + tm, tm + 2 * HALO)
    outside = jnp.logical_or(row < first_row, row >= end_row)
    p_ext = jnp.where(outside, 0.0, p_ext)

    t = pos * tm + lax.broadcasted_iota(jnp.int32, (tm, 1), 0)
    pooled_up = None
    for g, w in enumerate(POOL_WINDOWS):
        sl = slice(g * POOL_GROUP_DIM, (g + 1) * POOL_GROUP_DIM)
        e = p_ext[:, sl]
        win = _window_sums(e)[g][HALO:HALO + tm]
        lo = jnp.clip(t - w // 2, 0, seq_len)
        hi = jnp.clip(t - w // 2 + w, 0, seq_len)
        cnt = (hi - lo).astype(F32)
        pooled = win / cnt - e[HALO:HALO + tm]
        pg = _dot(pooled.astype(BF16), poolw_ref[g]) * pscale_ref[:, sl]
        part = _dot(pg.astype(BF16), wup_ref[sl, :])
        pooled_up = part if pooled_up is None else pooled_up + part

    h = _modulate_bf16(x, shift, scale)
    ga = _dot(h, wpg_ref[:, POOL_WIDTH:POOL_WIDTH + D_MODEL])
    gp = _dot(h, wpg_ref[:, POOL_WIDTH + D_MODEL:])
    attn_up = _dot(a_ref[...], wua_ref[...])
    merged = jax.nn.sigmoid(ga) * attn_up + jax.nn.sigmoid(gp) * pooled_up
    mix = _dot(merged.astype(BF16), wo_ref[...])
    o_ref[...] = _layer_norm(alpha * x + gate * mix, lng_ref[...], lnb_ref[...])


def _mix(x2d, attn, mod, mod_row_of_tile, w_pg, pool_w, pool_scale, w_up_attn, w_up_pool, w_out,
         ln_g, ln_b, *, seq_len, alpha):
    n_tok = x2d.shape[0]
    tm = TOKEN_TILE
    tiles_per_seq = seq_len // tm
    halo_per_tile = tm // HALO
    n_halo_blocks = n_tok // HALO
    return pl.pallas_call(
        functools.partial(_mix_kernel, tiles_per_seq=tiles_per_seq, alpha=alpha),
        grid=(n_tok // tm,),
        in_specs=[
            pl.BlockSpec((tm, D_MODEL), lambda i: (i, 0)),
            pl.BlockSpec((HALO, D_MODEL), lambda i: (jnp.maximum(i * halo_per_tile - 1, 0), 0)),
            pl.BlockSpec((HALO, D_MODEL),
                         lambda i: (jnp.minimum((i + 1) * halo_per_tile, n_halo_blocks - 1), 0)),
            pl.BlockSpec((tm, ATTN_WIDTH), lambda i: (i, 0)),
            pl.BlockSpec((None, N_MOD, D_MODEL), lambda i: (mod_row_of_tile(i), 0, 0)),
            _resident(w_pg.shape),
            _resident(pool_w.shape),
            _resident((1, POOL_WIDTH)),
            _resident(w_up_attn.shape),
            _resident(w_up_pool.shape),
            _resident(w_out.shape),
            _resident((1, D_MODEL)),
            _resident((1, D_MODEL)),
        ],
        out_specs=pl.BlockSpec((tm, D_MODEL), lambda i: (i, 0)),
        out_shape=jax.ShapeDtypeStruct((n_tok, D_MODEL), F32),
        compiler_params=_compiler_params(1),
        name="mixer_out",
    )(x2d, x2d, x2d, attn, mod, w_pg, pool_w, pool_scale.reshape(1, POOL_WIDTH),
      w_up_attn, w_up_pool, w_out, ln_g.reshape(1, D_MODEL), ln_b.reshape(1, D_MODEL))


def _rope_tables(n_tokens):
    t = jnp.arange(n_tokens, dtype=jnp.int32)
    row = (t // GRID_W).astype(F32)
    col = (t % GRID_W).astype(F32)
    n_freq = HEAD_DIM // 4
    inv_freq = ROPE_THETA ** (-jnp.arange(n_freq, dtype=F32) / n_freq)
    ang = jnp.concatenate([row[:, None] * inv_freq, col[:, None] * inv_freq], axis=-1)
    cos, sin = jnp.cos(ang), jnp.sin(ang)
    return jnp.concatenate([cos, cos], axis=-1), jnp.concatenate([-sin, sin], axis=-1)


def kernel(x_prompt, x_sample, cache_k, cache_v, c, c_ctx, w_mod, b_mod, ln_g, ln_b, ffn1_w1, ffn1_w2, w_in, q_norm_g, k_norm_g, pool_w, pool_scale, w_up_attn, w_up_pool, w_out, ffn2_w1, ffn2_w2):
    batch, seq, _ = x_prompt.shape
    dec_batch, n_lat, _ = x_sample.shape
    depth = w_mod.shape[0]
    past = cache_k.shape[2]
    alpha = (2.0 * depth) ** 0.25
    assert seq == TOKEN_TILE and n_lat % TOKEN_TILE == 0 and n_lat % Q_TILE == 0
    assert 1 + dec_batch <= MOD_ROWS

    cos2, sin2 = _rope_tables((n_lat // GRID_W) * GRID_W)
    cond = jnp.zeros((MOD_ROWS, D_MODEL), F32).at[0].set(c_ctx).at[1:1 + dec_batch].set(c)

    lat_tiles_per_seq = n_lat // TOKEN_TILE
    ctx_row = lambda i: 0
    lat_row = lambda i: 1 + i // lat_tiles_per_seq

    yp = x_prompt.reshape(batch * seq, D_MODEL)
    ys = x_sample.reshape(dec_batch * n_lat, D_MODEL)
    new_k, new_v = [], []
    for l in range(depth):
        w1a = ffn1_w1[l].astype(BF16)
        w2a = ffn1_w2[l].astype(BF16)
        w1b = ffn2_w1[l].astype(BF16)
        w2b = ffn2_w2[l].astype(BF16)
        w_in_b = w_in[l].astype(BF16)
        w_qkv = w_in_b[:, :QKV_WIDTH]
        w_pg = w_in_b[:, QKV_WIDTH:]
        pool_w_b = pool_w[l].astype(BF16)
        w_ua = w_up_attn[l].astype(BF16)
        w_upl = w_up_pool[l].astype(BF16)
        w_o = w_out[l].astype(BF16)

        mod = _modulation_table(cond, w_mod[l], b_mod[l]).reshape(MOD_ROWS, N_MOD, D_MODEL)

        yp = _ffn(yp, mod, ctx_row, w1a, w2a, ln_g[l, 0], ln_b[l, 0], mod_base=0, alpha=alpha)
        q, k_l, v_l, kb, vb = _qkv_ctx(yp, mod, w_qkv, q_norm_g[l], k_norm_g[l])
        attn = _attn_ctx(q, kb, vb, seq)
        yp = _mix(yp, attn, mod, ctx_row, w_pg, pool_w_b, pool_scale[l], w_ua, w_upl, w_o,
                  ln_g[l, 1], ln_b[l, 1], seq_len=seq, alpha=alpha)
        yp = _ffn(yp, mod, ctx_row, w1b, w2b, ln_g[l, 2], ln_b[l, 2], mod_base=6, alpha=alpha)
        new_k.append(k_l.reshape(batch, seq, N_KV_HEADS, HEAD_DIM))
        new_v.append(v_l.reshape(batch, seq, N_KV_HEADS, HEAD_DIM))

        ys = _ffn(ys, mod, lat_row, w1a, w2a, ln_g[l, 0], ln_b[l, 0], mod_base=0, alpha=alpha)
        q, kb, vb = _qkv_lat(ys, mod, w_qkv, q_norm_g[l], k_norm_g[l], cos2, sin2, n_lat)
        attn = _attn_lat(q, kb, vb,
                         cache_k[:, l].reshape(dec_batch, past, KV_WIDTH),
                         cache_v[:, l].reshape(dec_batch, past, KV_WIDTH), n_lat)
        ys = _mix(ys, attn, mod, lat_row, w_pg, pool_w_b, pool_scale[l], w_ua, w_upl, w_o,
                  ln_g[l, 1], ln_b[l, 1], seq_len=n_lat, alpha=alpha)
        ys = _ffn(ys, mod, lat_row, w1b, w2b, ln_g[l, 2], ln_b[l, 2], mod_base=6, alpha=alpha)

    return (yp.reshape(batch, seq, D_MODEL),
            ys.reshape(dec_batch, n_lat, D_MODEL),
            jnp.stack(new_k, axis=1),
            jnp.stack(new_v, axis=1))
```

```python
import functools
import math

import jax
import jax.numpy as jnp
from jax import lax
from jax.experimental import pallas as pl
from jax.experimental.pallas import tpu as pltpu

D_MODEL = 1024
GRID_W = 64
N_HEADS = 8
N_KV_HEADS = 2
HEAD_DIM = 128
GROUP = N_HEADS // N_KV_HEADS
ATTN_WIDTH = N_HEADS * HEAD_DIM
KV_WIDTH = N_KV_HEADS * HEAD_DIM
POOL_WINDOWS = (2, 4, 8, 16)
N_POOL_GROUPS = 4
POOL_GROUP_DIM = 128
POOL_WIDTH = N_POOL_GROUPS * POOL_GROUP_DIM
QKV_WIDTH = ATTN_WIDTH + 2 * KV_WIDTH
D_FF = 2816
N_MOD = 9
ROPE_THETA = 10000.0
LN_EPS = 1e-6
RMS_EPS = 1e-6

SUBLANES = 8
LANES = 128
MXU_DIM = 256
VMEM_LIMIT_BYTES = 56 * 1024 * 1024

HALO = SUBLANES
MOD_ROWS = SUBLANES
TOKEN_TILE = 512
SUB_TILE = 256
QKV_TILE = 256
Q_TILE = 256
KV_CHUNK = 256
Q_SCALE = HEAD_DIM ** -0.5 * math.log2(math.e)
V_AUG = MXU_DIM
FF_CHUNKS = ((0, 1024), (1024, 1024), (2048, 768))

BF16 = jnp.bfloat16
F32 = jnp.float32


def _dot(a, b):
    return jnp.dot(a, b, preferred_element_type=F32)


def _dot_nt(a, b):
    return lax.dot_general(a, b, (((1,), (1,)), ((), ())), preferred_element_type=F32)


def _layer_norm(x, g, b):
    mu = jnp.mean(x, axis=-1, keepdims=True)
    xc = x - mu
    var = jnp.mean(xc * xc, axis=-1, keepdims=True)
    return xc * lax.rsqrt(var + LN_EPS) * g + b


def _modulate_bf16(x, shift, scale):
    return (x * (1.0 + scale) + shift).astype(BF16)


def _resident(shape):
    zeros = (0,) * len(shape)
    return pl.BlockSpec(shape, lambda *_: zeros, pipeline_mode=pl.Buffered(1))


def _compiler_params(n_axes):
    return pltpu.CompilerParams(
        dimension_semantics=("arbitrary",) * n_axes,
        vmem_limit_bytes=VMEM_LIMIT_BYTES,
    )


def _mod_kernel(cond_ref, w_ref, b_ref, o_ref):
    c = cond_ref[...]
    s = (c * jax.nn.sigmoid(c)).astype(BF16)
    o_ref[...] = _dot(s, w_ref[...].astype(BF16)) + b_ref[...]


def _modulation_table(cond, w_mod, b_mod):
    n_out = w_mod.shape[1]
    bn = 1152
    return pl.pallas_call(
        _mod_kernel,
        grid=(n_out // bn,),
        in_specs=[
            pl.BlockSpec((MOD_ROWS, D_MODEL), lambda j: (0, 0)),
            pl.BlockSpec((D_MODEL, bn), lambda j: (0, j)),
            pl.BlockSpec((1, bn), lambda j: (0, j)),
        ],
        out_specs=pl.BlockSpec((MOD_ROWS, bn), lambda j: (0, j)),
        out_shape=jax.ShapeDtypeStruct((MOD_ROWS, n_out), F32),
        compiler_params=_compiler_params(1),
        name="modulation",
    )(cond, w_mod, b_mod.reshape(1, n_out))


def _ffn_body(x, shift, scale, gate, w1_ref, w2_ref, ln_g, ln_b, alpha):
    h = _modulate_bf16(x, shift, scale)
    f = None
    for start, width in FF_CHUNKS:
        a = _dot(h, w1_ref[:, start:start + width])
        b = _dot(h, w1_ref[:, D_FF + start:D_FF + start + width])
        u = (a * jax.nn.sigmoid(a) * b).astype(BF16)
        part = _dot(u, w2_ref[start:start + width, :])
        f = part if f is None else f + part
    return _layer_norm(alpha * x + (0.5 * gate) * f, ln_g, ln_b)


def _ffn_kernel(x_ref, mod_ref, w1_ref, w2_ref, lng_ref, lnb_ref, o_ref, *, mod_base, alpha):
    for r0 in range(0, x_ref.shape[0], SUB_TILE):
        rows = slice(r0, r0 + SUB_TILE)
        o_ref[rows, :] = _ffn_body(
            x_ref[rows, :],
            mod_ref[mod_base:mod_base + 1, :],
            mod_ref[mod_base + 1:mod_base + 2, :],
            mod_ref[mod_base + 2:mod_base + 3, :],
            w1_ref, w2_ref, lng_ref[...], lnb_ref[...], alpha)


def _ffn(x2d, mod, mod_row_of_token,w1, w2, ln_g, ln_b, *, mod_base, alpha):
    n_tok = x2d.shape[0]
    tm = TOKEN_TILE
    return pl.pallas_call(
        functools.partial(_ffn_kernel, mod_base=mod_base, alpha=alpha),
        grid=(n_tok // tm,),
        in_specs=[
            pl.BlockSpec((tm, D_MODEL), lambda i: (i, 0)),
            pl.BlockSpec((None, N_MOD, D_MODEL), lambda i: (mod_row_of_token(i * tm), 0, 0)),
            _resident(w1.shape),
            _resident(w2.shape),
            _resident((1, D_MODEL)),
            _resident((1, D_MODEL)),
        ],
        out_specs=pl.BlockSpec((tm, D_MODEL), lambda i: (i, 0)),
        out_shape=jax.ShapeDtypeStruct((n_tok, D_MODEL), F32),
        compiler_params=_compiler_params(1),
        name="ffn_half_step",
    )(x2d, mod, w1, w2, ln_g.reshape(1, D_MODEL), ln_b.reshape(1, D_MODEL))


def _rms_head(xh, g):
    ms = jnp.mean(xh * xh, axis=-1, keepdims=True)
    return xh * lax.rsqrt(ms + RMS_EPS) * g


def _rope(xh, cos2, sin2):
    return xh * cos2 + pltpu.roll(xh, HEAD_DIM // 2, axis=1) * sin2


def _ones_column_block(rows):
    col = lax.broadcasted_iota(jnp.int32, (rows, V_AUG - HEAD_DIM), 1)
    return jnp.where(col == 0, 1.0, 0.0).astype(BF16)


def _qkv_ctx_kernel(x_ref, mod_ref, w_ref, qg_ref, kg_ref,
                    q_ref, k_ref, v_ref, kb_ref, vb_ref):
    h = _modulate_bf16(x_ref[...], mod_ref[3:4, :], mod_ref[4:5, :])
    qkv = _dot(h, w_ref[...])
    qg = qg_ref[...] * Q_SCALE
    kg = kg_ref[...]
    for hh in range(N_HEADS):
        sl = slice(hh * HEAD_DIM, (hh + 1) * HEAD_DIM)
        q_ref[:, sl] = _rms_head(qkv[:, sl], qg).astype(BF16)
    for hh in range(N_KV_HEADS):
        sl = slice(hh * HEAD_DIM, (hh + 1) * HEAD_DIM)
        kn = _rms_head(qkv[:, ATTN_WIDTH + hh * HEAD_DIM:ATTN_WIDTH + (hh + 1) * HEAD_DIM], kg)
        k_ref[:, sl] = kn
        kb_ref[:, sl] = kn.astype(BF16)
    v = qkv[:, ATTN_WIDTH + KV_WIDTH:]
    v_ref[...] = v
    vb_ref[...] = v.astype(BF16)


def _qkv_lat_kernel(x_ref, mod_ref, w_ref, qg_ref, kg_ref, cos_ref, sin_ref,
                    q_ref, kb_ref, vb_ref):
    h = _modulate_bf16(x_ref[...], mod_ref[3:4, :], mod_ref[4:5, :])
    qkv = _dot(h, w_ref[...])
    qg = qg_ref[...] * Q_SCALE
    kg = kg_ref[...]
    cos2 = cos_ref[...]
    sin2 = sin_ref[...]
    for hh in range(N_HEADS):
        sl = slice(hh * HEAD_DIM, (hh + 1) * HEAD_DIM)
        q_ref[:, sl] = _rope(_rms_head(qkv[:, sl], qg), cos2, sin2).astype(BF16)
    for hh in range(N_KV_HEADS):
        sl = slice(hh * HEAD_DIM, (hh + 1) * HEAD_DIM)
        kn = _rms_head(qkv[:, ATTN_WIDTH + hh * HEAD_DIM:ATTN_WIDTH + (hh + 1) * HEAD_DIM], kg)
        kb_ref[:, sl] = _rope(kn, cos2, sin2).astype(BF16)
    ones_col = _ones_column_block(x_ref.shape[0])
    for hh in range(N_KV_HEADS):
        v0 = ATTN_WIDTH + KV_WIDTH + hh * HEAD_DIM
        vb_ref[:, hh * V_AUG:hh * V_AUG + HEAD_DIM] = qkv[:, v0:v0 + HEAD_DIM].astype(BF16)
        vb_ref[:, hh * V_AUG + HEAD_DIM:(hh + 1) * V_AUG] = ones_col


def _qkv_ctx(x2d, mod, w_qkv, q_g, k_g):
    n_tok = x2d.shape[0]
    tm = QKV_TILE
    tok = lambda width: pl.BlockSpec((tm, width), lambda i: (i, 0))
    return pl.pallas_call(
        _qkv_ctx_kernel,
        grid=(n_tok // tm,),
        in_specs=[
            tok(D_MODEL),
            pl.BlockSpec((None, N_MOD, D_MODEL), lambda i: (0, 0, 0)),
            _resident(w_qkv.shape),
            _resident((1, HEAD_DIM)),
            _resident((1, HEAD_DIM)),
        ],
        out_specs=[tok(ATTN_WIDTH), tok(KV_WIDTH), tok(KV_WIDTH), tok(KV_WIDTH), tok(KV_WIDTH)],
        out_shape=[
            jax.ShapeDtypeStruct((n_tok, ATTN_WIDTH), BF16),
            jax.ShapeDtypeStruct((n_tok, KV_WIDTH), F32),
            jax.ShapeDtypeStruct((n_tok, KV_WIDTH), F32),
            jax.ShapeDtypeStruct((n_tok, KV_WIDTH), BF16),
            jax.ShapeDtypeStruct((n_tok, KV_WIDTH), BF16),
        ],
        compiler_params=_compiler_params(1),
        name="qkv_ctx",
    )(x2d, mod, w_qkv, q_g.reshape(1, HEAD_DIM), k_g.reshape(1, HEAD_DIM))


def _qkv_lat(x2d, mod, w_qkv, q_g, k_g, cos2, sin2, n_lat):
    n_tok = x2d.shape[0]
    tm = QKV_TILE
    tiles_per_seq = n_lat // tm
    tok = lambda width: pl.BlockSpec((tm, width), lambda i: (i, 0))
    rope_spec = pl.BlockSpec((tm, HEAD_DIM), lambda i: (i % tiles_per_seq, 0))
    return pl.pallas_call(
        _qkv_lat_kernel,
        grid=(n_tok // tm,),
        in_specs=[
            tok(D_MODEL),
            pl.BlockSpec((None, N_MOD, D_MODEL), lambda i: (1 + i // tiles_per_seq, 0, 0)),
            _resident(w_qkv.shape),
            _resident((1, HEAD_DIM)),
            _resident((1, HEAD_DIM)),
            rope_spec,
            rope_spec,
        ],
        out_specs=[tok(ATTN_WIDTH), tok(KV_WIDTH), tok(N_KV_HEADS * V_AUG)],
        out_shape=[
            jax.ShapeDtypeStruct((n_tok, ATTN_WIDTH), BF16),
            jax.ShapeDtypeStruct((n_tok, KV_WIDTH), BF16),
            jax.ShapeDtypeStruct((n_tok, N_KV_HEADS * V_AUG), BF16),
        ],
        compiler_params=_compiler_params(1),
        name="qkv_lat",
    )(x2d, mod, w_qkv, q_g.reshape(1, HEAD_DIM), k_g.reshape(1, HEAD_DIM), cos2, sin2)


def _stack_group_heads(q_blk):
    return jnp.concatenate(
        [q_blk[:, g * HEAD_DIM:(g + 1) * HEAD_DIM] for g in range(GROUP)], axis=0)


def _attn_ctx_kernel(q_ref, k_ref, v_ref, o_ref):
    t = q_ref.shape[0]
    for kv in range(N_KV_HEADS):
        q4 = _stack_group_heads(q_ref[:, kv * GROUP * HEAD_DIM:(kv + 1) * GROUP * HEAD_DIM])
        ksl = slice(kv * HEAD_DIM, (kv + 1) * HEAD_DIM)
        s = _dot_nt(q4, k_ref[:, ksl])
        m = jnp.max(s, axis=-1, keepdims=True)
        p = jnp.exp2(s - m)
        l = jnp.sum(p, axis=-1, keepdims=True)
        o = _dot(p.astype(BF16), v_ref[:, ksl]) / l
        for g in range(GROUP):
            col = (kv * GROUP + g) * HEAD_DIM
            o_ref[:, col:col + HEAD_DIM] = o[g * t:(g + 1) * t].astype(BF16)


def _attn_ctx(q, kb, vb, seq):
    n_tok = q.shape[0]
    tok = lambda width: pl.BlockSpec((seq, width), lambda b: (b, 0))
    return pl.pallas_call(
        _attn_ctx_kernel,
        grid=(n_tok // seq,),
        in_specs=[tok(ATTN_WIDTH), tok(KV_WIDTH), tok(KV_WIDTH)],
        out_specs=tok(ATTN_WIDTH),
        out_shape=jax.ShapeDtypeStruct((n_tok, ATTN_WIDTH), BF16),
        compiler_params=_compiler_params(1),
        name="attn_ctx",
    )(q, kb, vb)


def _attn_lat_kernel(q_ref, kc_ref, vc_ref, kl_ref, vl_ref, o_ref):
    t = q_ref.shape[0]
    n_lat = kl_ref.shape[0]
    past = kc_ref.shape[0]
    q4 = _stack_group_heads(q_ref[...])
    vc = jnp.concatenate([vc_ref[...].astype(BF16), _ones_column_block(past)], axis=1)
    chunks = [(kc_ref[...].astype(BF16), vc)]
    for c0 in range(0, n_lat, KV_CHUNK):
        chunks.append((kl_ref[c0:c0 + KV_CHUNK, :], vl_ref[c0:c0 + KV_CHUNK, :]))
    m = None
    acc = None
    for k_c, v_c in chunks:
        s = _dot_nt(q4, k_c)
        m_c = jnp.max(s, axis=-1, keepdims=True)
        if m is None:
            m = m_c
            acc = _dot(jnp.exp2(s - m).astype(BF16), v_c)
        else:
            m_new = jnp.maximum(m, m_c)
            acc = jnp.exp2(m - m_new) * acc + _dot(jnp.exp2(s - m_new).astype(BF16), v_c)
            m = m_new
    o = acc[:, :HEAD_DIM] / acc[:, HEAD_DIM:HEAD_DIM + 1]
    for g in range(GROUP):
        o_ref[:, g * HEAD_DIM:(g + 1) * HEAD_DIM] = o[g * t:(g + 1) * t].astype(BF16)


def _attn_lat(q, kb, vb, cache_k, cache_v, n_lat):
    n_tok = q.shape[0]
    n_batch = n_tok // n_lat
    past = cache_k.shape[1]
    tq = Q_TILE
    q_tiles = n_lat // tq
    gw = GROUP * HEAD_DIM
    return pl.pallas_call(
        _attn_lat_kernel,
        grid=(n_batch, N_KV_HEADS, q_tiles),
        in_specs=[
            pl.BlockSpec((tq, gw), lambda b, kv, i: (b * q_tiles + i, kv)),
            pl.BlockSpec((None, past, HEAD_DIM), lambda b, kv, i: (b, 0, kv)),
            pl.BlockSpec((None, past, HEAD_DIM), lambda b, kv, i: (b, 0, kv)),
            pl.BlockSpec((n_lat, HEAD_DIM), lambda b, kv, i: (b, kv)),
            pl.BlockSpec((n_lat, V_AUG), lambda b, kv, i: (b, kv)),
        ],
        out_specs=pl.BlockSpec((tq, gw), lambda b, kv, i: (b * q_tiles + i, kv)),
        out_shape=jax.ShapeDtypeStruct((n_tok, ATTN_WIDTH), BF16),
        compiler_params=_compiler_params(3),
        name="attn_lat",
    )(q, cache_k, cache_v, kb, vb)


def _window_sums(e):
    n = e.shape[0]
    back = lambda a, k: pltpu.roll(a, k, axis=0)
    fwd = lambda a, k: pltpu.roll(a, n - k, axis=0)
    c2 = e + back(e, 1)
    c4 = back(c2, 1) + fwd(c2, 1)
    c8 = back(c4, 2) + fwd(c4, 2)
    c16 = back(c8, 4) + fwd(c8, 4)
    return c2, c4, c8, c16


def _mix_kernel(x_ref, xp_ref, xn_ref, a_ref, mod_ref, wpg_ref, poolw_ref, pscale_ref,
                wua_ref, wup_ref, wo_ref, lng_ref, lnb_ref, o_ref, *, seq_len, alpha):
    tm = x_ref.shape[0]
    tile_start = pl.program_id(0) * tm
    shift = mod_ref[3:4, :]
    scale = mod_ref[4:5, :]
    gate = mod_ref[5:6, :]

    h_all = _modulate_bf16(jnp.concatenate([x_ref[...], xp_ref[...], xn_ref[...]], axis=0),
                           shift, scale)
    p_all = _dot(h_all, wpg_ref[:, :POOL_WIDTH])
    p_ext = jnp.concatenate([p_all[tm:tm + HALO], p_all[:tm], p_all[tm + HALO:]], axis=0)

    for r0 in range(0, tm, SUB_TILE):
        rows = slice(r0, r0 + SUB_TILE)
        x = x_ref[rows, :]
        pos = (tile_start + r0) % seq_len

        e_all = p_ext[r0:r0 + SUB_TILE + 2 * HALO, :]
        row = lax.broadcasted_iota(jnp.int32, (SUB_TILE + 2 * HALO, 1), 0)
        first_row = jnp.where(pos == 0, HALO, 0)
        end_row = jnp.where(pos + SUB_TILE == seq_len, HALO + SUB_TILE, SUB_TILE + 2 * HALO)
        e_all = jnp.where(jnp.logical_or(row < first_row, row >= end_row), 0.0, e_all)

        t = pos + lax.broadcasted_iota(jnp.int32, (SUB_TILE, 1), 0)
        pooled_up = None
        for g, w in enumerate(POOL_WINDOWS):
            sl = slice(g * POOL_GROUP_DIM, (g + 1) * POOL_GROUP_DIM)
            e = e_all[:, sl]
            win = _window_sums(e)[g][HALO:HALO + SUB_TILE]
            lo = jnp.clip(t - w // 2, 0, seq_len)
            hi = jnp.clip(t - w // 2 + w, 0, seq_len)
            inv_cnt = 1.0 / (hi - lo).astype(F32)
            pooled = win * inv_cnt - e[HALO:HALO + SUB_TILE]
            pg = _dot(pooled.astype(BF16), poolw_ref[g]) * pscale_ref[:, sl]
            part = _dot(pg.astype(BF16), wup_ref[sl, :])
            pooled_up = part if pooled_up is None else pooled_up + part

        h = h_all[rows, :]
        ga = _dot(h, wpg_ref[:, POOL_WIDTH:POOL_WIDTH + D_MODEL])
        gp = _dot(h, wpg_ref[:, POOL_WIDTH + D_MODEL:])
        attn_up = _dot(a_ref[rows, :], wua_ref[...])
        merged = jax.nn.sigmoid(ga) * attn_up + jax.nn.sigmoid(gp) * pooled_up
        mix = _dot(merged.astype(BF16), wo_ref[...])
        o_ref[rows, :] = _layer_norm(alpha * x + gate * mix, lng_ref[...], lnb_ref[...])


def _mix(x2d, attn, mod, mod_row_of_token,w_pg, pool_w, pool_scale, w_up_attn, w_up_pool, w_out,
         ln_g, ln_b, *, seq_len, alpha):
    n_tok = x2d.shape[0]
    tm = TOKEN_TILE
    halo_per_tile = tm // HALO
    n_halo_blocks = n_tok // HALO
    return pl.pallas_call(
        functools.partial(_mix_kernel, seq_len=seq_len, alpha=alpha),
        grid=(n_tok // tm,),
        in_specs=[
            pl.BlockSpec((tm, D_MODEL), lambda i: (i, 0)),
            pl.BlockSpec((HALO, D_MODEL), lambda i: (jnp.maximum(i * halo_per_tile - 1, 0), 0)),
            pl.BlockSpec((HALO, D_MODEL),
                         lambda i: (jnp.minimum((i + 1) * halo_per_tile, n_halo_blocks - 1), 0)),
            pl.BlockSpec((tm, ATTN_WIDTH), lambda i: (i, 0)),
            pl.BlockSpec((None, N_MOD, D_MODEL), lambda i: (mod_row_of_token(i * tm), 0, 0)),
            _resident(w_pg.shape),
            _resident(pool_w.shape),
            _resident((1, POOL_WIDTH)),
            _resident(w_up_attn.shape),
            _resident(w_up_pool.shape),
            _resident(w_out.shape),
            _resident((1, D_MODEL)),
            _resident((1, D_MODEL)),
        ],
        out_specs=pl.BlockSpec((tm, D_MODEL), lambda i: (i, 0)),
        out_shape=jax.ShapeDtypeStruct((n_tok, D_MODEL), F32),
        compiler_params=_compiler_params(1),
        name="mixer_out",
    )(x2d, x2d, x2d, attn, mod, w_pg, pool_w, pool_scale.reshape(1, POOL_WIDTH),
      w_up_attn, w_up_pool, w_out, ln_g.reshape(1, D_MODEL), ln_b.reshape(1, D_MODEL))


def _rope_tables(n_tokens):
    t = jnp.arange(n_tokens, dtype=jnp.int32)
    row = (t // GRID_W).astype(F32)
    col = (t % GRID_W).astype(F32)
    n_freq = HEAD_DIM // 4
    inv_freq = ROPE_THETA ** (-jnp.arange(n_freq, dtype=F32) / n_freq)
    ang = jnp.concatenate([row[:, None] * inv_freq, col[:, None] * inv_freq], axis=-1)
    cos, sin = jnp.cos(ang), jnp.sin(ang)
    return jnp.concatenate([cos, cos], axis=-1), jnp.concatenate([-sin, sin], axis=-1)


def kernel(x_prompt, x_sample, cache_k, cache_v, c, c_ctx, w_mod, b_mod, ln_g, ln_b, ffn1_w1, ffn1_w2, w_in, q_norm_g, k_norm_g, pool_w, pool_scale, w_up_attn, w_up_pool, w_out, ffn2_w1, ffn2_w2):
    batch, seq, _ = x_prompt.shape
    dec_batch, n_lat, _ = x_sample.shape
    depth = w_mod.shape[0]
    past = cache_k.shape[2]
    alpha = (2.0 * depth) ** 0.25
    assert seq % SUB_TILE == 0 and n_lat % TOKEN_TILE == 0 and n_lat % Q_TILE == 0
    assert (batch * seq) % TOKEN_TILE == 0 and seq % QKV_TILE == 0
    assert 1 + dec_batch <= MOD_ROWS

    cos2, sin2 = _rope_tables((n_lat // GRID_W) * GRID_W)
    cond = jnp.zeros((MOD_ROWS, D_MODEL), F32).at[0].set(c_ctx).at[1:1 + dec_batch].set(c)

    ctx_row = lambda t: 0
    lat_row = lambda t: 1 + t // n_lat

    yp = x_prompt.reshape(batch * seq, D_MODEL)
    ys = x_sample.reshape(dec_batch * n_lat, D_MODEL)
    new_k, new_v = [], []
    for l in range(depth):
        w1a = ffn1_w1[l].astype(BF16)
        w2a = ffn1_w2[l].astype(BF16)
        w1b = ffn2_w1[l].astype(BF16)
        w2b = ffn2_w2[l].astype(BF16)
        w_in_b = w_in[l].astype(BF16)
        w_qkv = w_in_b[:, :QKV_WIDTH]
        w_pg = w_in_b[:, QKV_WIDTH:]
        pool_w_b = pool_w[l].astype(BF16)
        w_ua = w_up_attn[l].astype(BF16)
        w_upl = w_up_pool[l].astype(BF16)
        w_o = w_out[l].astype(BF16)

        mod = _modulation_table(cond, w_mod[l], b_mod[l]).reshape(MOD_ROWS, N_MOD, D_MODEL)

        yp = _ffn(yp, mod, ctx_row, w1a, w2a, ln_g[l, 0], ln_b[l, 0], mod_base=0, alpha=alpha)
        q, k_l, v_l, kb, vb = _qkv_ctx(yp, mod, w_qkv, q_norm_g[l], k_norm_g[l])
        attn = _attn_ctx(q, kb, vb, seq)
        yp = _mix(yp, attn, mod, ctx_row, w_pg, pool_w_b, pool_scale[l], w_ua, w_upl, w_o,
                  ln_g[l, 1], ln_b[l, 1], seq_len=seq, alpha=alpha)
        yp = _ffn(yp, mod, ctx_row, w1b, w2b, ln_g[l, 2], ln_b[l, 2], mod_base=6, alpha=alpha)
        new_k.append(k_l.reshape(batch, seq, N_KV_HEADS, HEAD_DIM))
        new_v.append(v_l.reshape(batch, seq, N_KV_HEADS, HEAD_DIM))

        ys = _ffn(ys, mod, lat_row, w1a, w2a, ln_g[l, 0], ln_b[l, 0], mod_base=0, alpha=alpha)
        q, kb, vb = _qkv_lat(ys, mod, w_qkv, q_norm_g[l], k_norm_g[l], cos2, sin2, n_lat)
        attn = _attn_lat(q, kb, vb,
                         cache_k[:, l].reshape(dec_batch, past, KV_WIDTH),
                         cache_v[:, l].reshape(dec_batch, past, KV_WIDTH), n_lat)
        ys = _mix(ys, attn, mod, lat_row, w_pg, pool_w_b, pool_scale[l], w_ua, w_upl, w_o,
                  ln_g[l, 1], ln_b[l, 1], seq_len=n_lat, alpha=alpha)
        ys = _ffn(ys, mod, lat_row, w1b, w2b, ln_g[l, 2], ln_b[l, 2], mod_base=6, alpha=alpha)

    return (yp.reshape(batch, seq, D_MODEL),
            ys.reshape(dec_batch, n_lat, D_MODEL),
            jnp.stack(new_k, axis=1),
            jnp.stack(new_v, axis=1))
```

```python
import functools
import math

import numpy as np

import jax
import jax.numpy as jnp
from jax import lax
from jax.experimental import pallas as pl
from jax.experimental.pallas import tpu as pltpu

D_MODEL = 1024
GRID_W = 64
N_HEADS = 8
N_KV_HEADS = 2
HEAD_DIM = 128
GROUP = N_HEADS // N_KV_HEADS
ATTN_WIDTH = N_HEADS * HEAD_DIM
KV_WIDTH = N_KV_HEADS * HEAD_DIM
POOL_WINDOWS = (2, 4, 8, 16)
N_POOL_GROUPS = 4
POOL_GROUP_DIM = 128
POOL_WIDTH = N_POOL_GROUPS * POOL_GROUP_DIM
QKV_WIDTH = ATTN_WIDTH + 2 * KV_WIDTH
FRONT_WIDTH = QKV_WIDTH + POOL_WIDTH
D_FF = 2816
N_MOD = 9
ROPE_THETA = 10000.0
LN_EPS = 1e-6
RMS_EPS = 1e-6

SUBLANES = 8
LANES = 128
MXU_DIM = 256
VMEM_LIMIT_BYTES = 56 * 1024 * 1024

HALO = SUBLANES
MOD_ROWS = SUBLANES
TOKEN_TILE = 512
SUB_TILE = 256
Q_TILE = 256
KV_CHUNK = 256
FF_CHUNKS = ((0, 1024), (1024, 1024), (2048, 768))
Q_SCALE = HEAD_DIM ** -0.5 * math.log2(math.e)
V_AUG = MXU_DIM

BF16 = jnp.bfloat16
F32 = jnp.float32


def _dot(a, b):
    return jnp.dot(a, b, preferred_element_type=F32)


def _dot_nt(a, b):
    return lax.dot_general(a, b, (((1,), (1,)), ((), ())), preferred_element_type=F32)


def _layer_norm(x, g, b):
    mu = jnp.mean(x, axis=-1, keepdims=True)
    xc = x - mu
    var = jnp.mean(xc * xc, axis=-1, keepdims=True)
    return xc * lax.rsqrt(var + LN_EPS) * g + b


def _modulate_bf16(x, shift, scale):
    return (x * (1.0 + scale) + shift).astype(BF16)


def _resident(shape):
    zeros = (0,) * len(shape)
    return pl.BlockSpec(shape, lambda *_: zeros, pipeline_mode=pl.Buffered(1))


def _compiler_params(n_axes):
    return pltpu.CompilerParams(
        dimension_semantics=("arbitrary",) * n_axes,
        vmem_limit_bytes=VMEM_LIMIT_BYTES,
    )


def _mod_kernel(cond_ref, w_ref, b_ref, o_ref):
    c = cond_ref[...]
    s = (c * jax.nn.sigmoid(c)).astype(BF16)
    o_ref[...] = _dot(s, w_ref[...].astype(BF16)) + b_ref[...]


def _modulation_table(cond, w_mod, b_mod):
    n_out = w_mod.shape[1]
    bn = 1152
    return pl.pallas_call(
        _mod_kernel,
        grid=(n_out // bn,),
        in_specs=[
            pl.BlockSpec((MOD_ROWS, D_MODEL), lambda j: (0, 0)),
            pl.BlockSpec((D_MODEL, bn), lambda j: (0, j)),
            pl.BlockSpec((1, bn), lambda j: (0, j)),
        ],
        out_specs=pl.BlockSpec((MOD_ROWS, bn), lambda j: (0, j)),
        out_shape=jax.ShapeDtypeStruct((MOD_ROWS, n_out), F32),
        compiler_params=_compiler_params(1),
        name="modulation",
    )(cond, w_mod, b_mod.reshape(1, n_out))


def _ffn_body(x, mod_ref, mod_base, w1_ref, w2_ref, ln_g, ln_b, alpha):
    shift = mod_ref[mod_base:mod_base + 1, :]
    scale = mod_ref[mod_base + 1:mod_base + 2, :]
    gate = mod_ref[mod_base + 2:mod_base + 3, :]
    h = _modulate_bf16(x, shift, scale)
    f = None
    for start, width in FF_CHUNKS:
        a = _dot(h, w1_ref[:, start:start + width])
        b = _dot(h, w1_ref[:, D_FF + start:D_FF + start + width])
        u = (a * jax.nn.sigmoid(a) * b).astype(BF16)
        part = _dot(u, w2_ref[start:start + width, :])
        f = part if f is None else f + part
    return _layer_norm(alpha * x + (0.5 * gate) * f, ln_g, ln_b)


def _rms_head(xh, g):
    ms = jnp.mean(xh * xh, axis=-1, keepdims=True)
    return xh * lax.rsqrt(ms + RMS_EPS) * g


def _rope(xh, cos2, sin2):
    return xh * cos2 + pltpu.roll(xh, HEAD_DIM // 2, axis=1) * sin2


def _ones_column_block(rows):
    col = lax.broadcasted_iota(jnp.int32, (rows, V_AUG - HEAD_DIM), 1)
    return jnp.where(col == 0, 1.0, 0.0).astype(BF16)


def _stack_heads(heads):
    return jnp.concatenate(heads, axis=0)


def _head_cols(hh):
    return slice(hh * HEAD_DIM, (hh + 1) * HEAD_DIM)


def _front_ctx_kernel(x_ref, mod_ref, w1_ref, w2_ref, lng_ref, lnb_ref, win_ref, qg_ref, kg_ref,
                      x1_ref, a_ref, k_ref, v_ref, pin_ref, *, alpha):
    qg = qg_ref[...] * Q_SCALE
    kg = kg_ref[...]
    for r0 in range(0, x_ref.shape[0], SUB_TILE):
        rows = slice(r0, r0 + SUB_TILE)
        x1 = _ffn_body(x_ref[rows, :], mod_ref, 0, w1_ref, w2_ref, lng_ref[...], lnb_ref[...], alpha)
        x1_ref[rows, :] = x1
        proj = _dot(_modulate_bf16(x1, mod_ref[3:4, :], mod_ref[4:5, :]), win_ref[...])
        pin_ref[rows, :] = proj[:, QKV_WIDTH:]
        v = proj[:, ATTN_WIDTH + KV_WIDTH:QKV_WIDTH]
        v_ref[rows, :] = v
        vb = v.astype(BF16)
        for kv in range(N_KV_HEADS):
            kn = _rms_head(proj[:, ATTN_WIDTH + kv * HEAD_DIM:ATTN_WIDTH + (kv + 1) * HEAD_DIM], kg)
            k_ref[rows, _head_cols(kv)] = kn
            q4 = _stack_heads([
                _rms_head(proj[:, _head_cols(kv * GROUP + g)], qg).astype(BF16)
                for g in range(GROUP)])
            s = _dot_nt(q4, kn.astype(BF16))
            p = jnp.exp2(s - jnp.max(s, axis=-1, keepdims=True))
            l = jnp.sum(p, axis=-1, keepdims=True)
            o = _dot(p.astype(BF16), vb[:, _head_cols(kv)]) / l
            for g in range(GROUP):
                a_ref[rows, _head_cols(kv * GROUP + g)] = (
                    o[g * SUB_TILE:(g + 1) * SUB_TILE].astype(BF16))


def _front_lat_kernel(x_ref, mod_ref, w1_ref, w2_ref, lng_ref, lnb_ref, win_ref, qg_ref, kg_ref,
                      cos_ref, sin_ref, x1_ref, q_ref, kb_ref, vb_ref, pin_ref, *, alpha):
    qg = qg_ref[...] * Q_SCALE
    kg = kg_ref[...]
    ones_col = _ones_column_block(SUB_TILE)
    for r0 in range(0, x_ref.shape[0], SUB_TILE):
        rows = slice(r0, r0 + SUB_TILE)
        x1 = _ffn_body(x_ref[rows, :], mod_ref, 0, w1_ref, w2_ref, lng_ref[...], lnb_ref[...], alpha)
        x1_ref[rows, :] = x1
        proj = _dot(_modulate_bf16(x1, mod_ref[3:4, :], mod_ref[4:5, :]), win_ref[...])
        pin_ref[rows, :] = proj[:, QKV_WIDTH:]
        cos2 = cos_ref[rows, :]
        sin2 = sin_ref[rows, :]
        for hh in range(N_HEADS):
            q_ref[rows, _head_cols(hh)] = _rope(
                _rms_head(proj[:, _head_cols(hh)], qg), cos2, sin2).astype(BF16)
        for kv in range(N_KV_HEADS):
            kn = _rms_head(proj[:, ATTN_WIDTH + kv * HEAD_DIM:ATTN_WIDTH + (kv + 1) * HEAD_DIM], kg)
            kb_ref[rows, _head_cols(kv)] = _rope(kn, cos2, sin2).astype(BF16)
            v0 = ATTN_WIDTH + KV_WIDTH + kv * HEAD_DIM
            vb_ref[rows, kv * V_AUG:kv * V_AUG + HEAD_DIM] = proj[:, v0:v0 + HEAD_DIM].astype(BF16)
            vb_ref[rows, kv * V_AUG + HEAD_DIM:(kv + 1) * V_AUG] = ones_col


def _front_in_specs(tm, mod_row_of_token, w1, w2, w_front):
    return [
        pl.BlockSpec((tm, D_MODEL), lambda i: (i, 0)),
        pl.BlockSpec((None, N_MOD, D_MODEL), lambda i: (mod_row_of_token(i * tm), 0, 0)),
        _resident(w1.shape),
        _resident(w2.shape),
        _resident((1, D_MODEL)),
        _resident((1, D_MODEL)),
        _resident(w_front.shape),
        _resident((1, HEAD_DIM)),
        _resident((1, HEAD_DIM)),
    ]


def _front_ctx(x2d, mod, mod_row_of_token, w1, w2, ln_g, ln_b, w_front, q_g, k_g, *, alpha):
    n_tok = x2d.shape[0]
    tm = TOKEN_TILE
    tok = lambda width: pl.BlockSpec((tm, width), lambda i: (i, 0))
    return pl.pallas_call(
        functools.partial(_front_ctx_kernel, alpha=alpha),
        grid=(n_tok // tm,),
        in_specs=_front_in_specs(tm, mod_row_of_token, w1, w2, w_front),
        out_specs=[tok(D_MODEL), tok(ATTN_WIDTH), tok(KV_WIDTH), tok(KV_WIDTH), tok(POOL_WIDTH)],
        out_shape=[
            jax.ShapeDtypeStruct((n_tok, D_MODEL), F32),
            jax.ShapeDtypeStruct((n_tok, ATTN_WIDTH), BF16),
            jax.ShapeDtypeStruct((n_tok, KV_WIDTH), F32),
            jax.ShapeDtypeStruct((n_tok, KV_WIDTH), F32),
            jax.ShapeDtypeStruct((n_tok, POOL_WIDTH), F32),
        ],
        compiler_params=_compiler_params(1),
        name="front_ctx",
    )(x2d, mod, w1, w2, ln_g.reshape(1, D_MODEL), ln_b.reshape(1, D_MODEL), w_front,
      q_g.reshape(1, HEAD_DIM), k_g.reshape(1, HEAD_DIM))


def _front_lat(x2d, mod, mod_row_of_token, w1, w2, ln_g, ln_b, w_front, q_g, k_g, cos2, sin2,
               *, n_lat, alpha):
    n_tok = x2d.shape[0]
    tm = TOKEN_TILE
    tiles_per_seq = n_lat // tm
    tok = lambda width: pl.BlockSpec((tm, width), lambda i: (i, 0))
    rope_spec = pl.BlockSpec((tm, HEAD_DIM), lambda i: (i % tiles_per_seq, 0))
    return pl.pallas_call(
        functools.partial(_front_lat_kernel, alpha=alpha),
        grid=(n_tok // tm,),
        in_specs=_front_in_specs(tm, mod_row_of_token, w1, w2, w_front) + [rope_spec, rope_spec],
        out_specs=[tok(D_MODEL), tok(ATTN_WIDTH), tok(KV_WIDTH), tok(N_KV_HEADS * V_AUG),
                   tok(POOL_WIDTH)],
        out_shape=[
            jax.ShapeDtypeStruct((n_tok, D_MODEL), F32),
            jax.ShapeDtypeStruct((n_tok, ATTN_WIDTH), BF16),
            jax.ShapeDtypeStruct((n_tok, KV_WIDTH), BF16),
            jax.ShapeDtypeStruct((n_tok, N_KV_HEADS * V_AUG), BF16),
            jax.ShapeDtypeStruct((n_tok, POOL_WIDTH), F32),
        ],
        compiler_params=_compiler_params(1),
        name="front_lat",
    )(x2d, mod, w1, w2, ln_g.reshape(1, D_MODEL), ln_b.reshape(1, D_MODEL), w_front,
      q_g.reshape(1, HEAD_DIM), k_g.reshape(1, HEAD_DIM), cos2, sin2)


def _attn_lat_kernel(q_ref, kc_ref, vc_ref, kl_ref, vl_ref, o_ref):
    t = q_ref.shape[0]
    n_lat = kl_ref.shape[0]
    past = kc_ref.shape[0]
    q4 = _stack_heads([q_ref[:, _head_cols(g)] for g in range(GROUP)])
    vc = jnp.concatenate([vc_ref[...].astype(BF16), _ones_column_block(past)], axis=1)
    chunks = [(kc_ref[...].astype(BF16), vc)]
    for c0 in range(0, n_lat, KV_CHUNK):
        chunks.append((kl_ref[c0:c0 + KV_CHUNK, :], vl_ref[c0:c0 + KV_CHUNK, :]))
    m = None
    acc = None
    for k_c, v_c in chunks:
        s = _dot_nt(q4, k_c)
        m_c = jnp.max(s, axis=-1, keepdims=True)
        if m is None:
            m = m_c
            acc = _dot(jnp.exp2(s - m).astype(BF16), v_c)
        else:
            m_new = jnp.maximum(m, m_c)
            acc = jnp.exp2(m - m_new) * acc + _dot(jnp.exp2(s - m_new).astype(BF16), v_c)
            m = m_new
    o = acc[:, :HEAD_DIM] / acc[:, HEAD_DIM:HEAD_DIM + 1]
    for g in range(GROUP):
        o_ref[:, _head_cols(g)] = o[g * t:(g + 1) * t].astype(BF16)


def _attn_lat(q, kb, vb, cache_k, cache_v, n_lat):
    n_tok = q.shape[0]
    n_batch = n_tok // n_lat
    past = cache_k.shape[1]
    tq = Q_TILE
    q_tiles = n_lat // tq
    gw = GROUP * HEAD_DIM
    return pl.pallas_call(
        _attn_lat_kernel,
        grid=(n_batch, N_KV_HEADS, q_tiles),
        in_specs=[
            pl.BlockSpec((tq, gw), lambda b, kv, i: (b * q_tiles + i, kv)),
            pl.BlockSpec((None, past, HEAD_DIM), lambda b, kv, i: (b, 0, kv)),
            pl.BlockSpec((None, past, HEAD_DIM), lambda b, kv, i: (b, 0, kv)),
            pl.BlockSpec((n_lat, HEAD_DIM), lambda b, kv, i: (b, kv)),
            pl.BlockSpec((n_lat, V_AUG), lambda b, kv, i: (b, kv)),
        ],
        out_specs=pl.BlockSpec((tq, gw), lambda b, kv, i: (b * q_tiles + i, kv)),
        out_shape=jax.ShapeDtypeStruct((n_tok, ATTN_WIDTH), BF16),
        compiler_params=_compiler_params(3),
        name="attn_lat",
    )(q, cache_k, cache_v, kb, vb)


def _window_sums(e):
    n = e.shape[0]
    back = lambda a, k: pltpu.roll(a, k, axis=0)
    fwd = lambda a, k: pltpu.roll(a, n - k, axis=0)
    c2 = e + back(e, 1)
    c4 = back(c2, 1) + fwd(c2, 1)
    c8 = back(c4, 2) + fwd(c4, 2)
    c16 = back(c8, 4) + fwd(c8, 4)
    return c2, c4, c8, c16


def _back_kernel(x_ref, a_ref, pin_ref, pp_ref, pn_ref, mod_ref, wg_ref, poolw_ref, pscale_ref,
                 wua_ref, wup_ref, wo_ref, lng1_ref, lnb1_ref, w1_ref, w2_ref, lng2_ref, lnb2_ref,
                 o_ref, *, seq_len, alpha):
    tm = x_ref.shape[0]
    tile_start = pl.program_id(0) * tm
    shift = mod_ref[3:4, :]
    scale = mod_ref[4:5, :]
    gate = mod_ref[5:6, :]
    p_ext = jnp.concatenate([pp_ref[...], pin_ref[...], pn_ref[...]], axis=0)

    for r0 in range(0, tm, SUB_TILE):
        rows = slice(r0, r0 + SUB_TILE)
        x = x_ref[rows, :]
        pos = (tile_start + r0) % seq_len

        e_all = p_ext[r0:r0 + SUB_TILE + 2 * HALO, :]
        row = lax.broadcasted_iota(jnp.int32, (SUB_TILE + 2 * HALO, 1), 0)
        first_row = jnp.where(pos == 0, HALO, 0)
        end_row = jnp.where(pos + SUB_TILE == seq_len, HALO + SUB_TILE, SUB_TILE + 2 * HALO)
        e_all = jnp.where(jnp.logical_or(row < first_row, row >= end_row), 0.0, e_all)

        t = pos + lax.broadcasted_iota(jnp.int32, (SUB_TILE, 1), 0)
        pooled_up = None
        for g, w in enumerate(POOL_WINDOWS):
            sl = slice(g * POOL_GROUP_DIM, (g + 1) * POOL_GROUP_DIM)
            e = e_all[:, sl]
            win = _window_sums(e)[g][HALO:HALO + SUB_TILE]
            lo = jnp.maximum(t - w // 2, 0)
            hi = jnp.minimum(t + w // 2, seq_len)
            inv_cnt = 1.0 / (hi - lo).astype(F32)
            pooled = win * inv_cnt - e[HALO:HALO + SUB_TILE]
            pg = _dot(pooled.astype(BF16), poolw_ref[g]) * pscale_ref[:, sl]
            part = _dot(pg.astype(BF16), wup_ref[sl, :])
            pooled_up = part if pooled_up is None else pooled_up + part

        h = _modulate_bf16(x, shift, scale)
        ga = _dot(h, wg_ref[:, :D_MODEL])
        gp = _dot(h, wg_ref[:, D_MODEL:])
        attn_up = _dot(a_ref[rows, :], wua_ref[...])
        merged = jax.nn.sigmoid(ga) * attn_up + jax.nn.sigmoid(gp) * pooled_up
        mix = _dot(merged.astype(BF16), wo_ref[...])
        x2 = _layer_norm(alpha * x + gate * mix, lng1_ref[...], lnb1_ref[...])
        o_ref[rows, :] = _ffn_body(x2, mod_ref, 6, w1_ref, w2_ref, lng2_ref[...], lnb2_ref[...],
                                   alpha)


def _back(x2d, attn, pin, mod, mod_row_of_token, w_g, pool_w, pool_scale, w_up_attn, w_up_pool,
          w_out, ln_g1, ln_b1, w1, w2, ln_g2, ln_b2, *, seq_len, alpha):
    n_tok = x2d.shape[0]
    tm = TOKEN_TILE
    halo_per_tile = tm // HALO
    n_halo_blocks = n_tok // HALO
    row_vec = lambda v: v.reshape(1, D_MODEL)
    return pl.pallas_call(
        functools.partial(_back_kernel, seq_len=seq_len, alpha=alpha),
        grid=(n_tok // tm,),
        in_specs=[
            pl.BlockSpec((tm, D_MODEL), lambda i: (i, 0)),
            pl.BlockSpec((tm, ATTN_WIDTH), lambda i: (i, 0)),
            pl.BlockSpec((tm, POOL_WIDTH), lambda i: (i, 0)),
            pl.BlockSpec((HALO, POOL_WIDTH), lambda i: (jnp.maximum(i * halo_per_tile - 1, 0), 0)),
            pl.BlockSpec((HALO, POOL_WIDTH),
                         lambda i: (jnp.minimum((i + 1) * halo_per_tile, n_halo_blocks - 1), 0)),
            pl.BlockSpec((None, N_MOD, D_MODEL), lambda i: (mod_row_of_token(i * tm), 0, 0)),
            _resident(w_g.shape),
            _resident(pool_w.shape),
            _resident((1, POOL_WIDTH)),
            _resident(w_up_attn.shape),
            _resident(w_up_pool.shape),
            _resident(w_out.shape),
            _resident((1, D_MODEL)),
            _resident((1, D_MODEL)),
            _resident(w1.shape),
            _resident(w2.shape),
            _resident((1, D_MODEL)),
            _resident((1, D_MODEL)),
        ],
        out_specs=pl.BlockSpec((tm, D_MODEL), lambda i: (i, 0)),
        out_shape=jax.ShapeDtypeStruct((n_tok, D_MODEL), F32),
        compiler_params=_compiler_params(1),
        name="back",
    )(x2d, attn, pin, pin, pin, mod, w_g, pool_w, pool_scale.reshape(1, POOL_WIDTH),
      w_up_attn, w_up_pool, w_out, row_vec(ln_g1), row_vec(ln_b1), w1, w2,
      row_vec(ln_g2), row_vec(ln_b2))


def _rope_tables(n_tokens):
    t = np.arange(n_tokens)
    n_freq = HEAD_DIM // 4
    inv_freq = ROPE_THETA ** (-np.arange(n_freq, dtype=np.float64) / n_freq)
    ang = np.concatenate([(t // GRID_W)[:, None] * inv_freq, (t % GRID_W)[:, None] * inv_freq],
                         axis=-1)
    cos, sin = np.cos(ang), np.sin(ang)
    return (jnp.asarray(np.concatenate([cos, cos], axis=-1), F32),
            jnp.asarray(np.concatenate([-sin, sin], axis=-1), F32))


def kernel(x_prompt, x_sample, cache_k, cache_v, c, c_ctx, w_mod, b_mod, ln_g, ln_b, ffn1_w1, ffn1_w2, w_in, q_norm_g, k_norm_g, pool_w, pool_scale, w_up_attn, w_up_pool, w_out, ffn2_w1, ffn2_w2):
    batch, seq, _ = x_prompt.shape
    dec_batch, n_lat, _ = x_sample.shape
    depth = w_mod.shape[0]
    past = cache_k.shape[2]
    alpha = (2.0 * depth) ** 0.25
    assert seq == SUB_TILE and (batch * seq) % TOKEN_TILE == 0
    assert n_lat % TOKEN_TILE == 0 and n_lat % Q_TILE == 0 and n_lat % KV_CHUNK == 0
    assert 1 + dec_batch <= MOD_ROWS

    cos2, sin2 = _rope_tables((n_lat // GRID_W) * GRID_W)
    cond = jnp.zeros((MOD_ROWS, D_MODEL), F32).at[0].set(c_ctx).at[1:1 + dec_batch].set(c)

    ctx_row = lambda t: 0
    lat_row = lambda t: 1 + t // n_lat

    yp = x_prompt.reshape(batch * seq, D_MODEL)
    ys = x_sample.reshape(dec_batch * n_lat, D_MODEL)
    new_k, new_v = [], []
    for l in range(depth):
        w1a = ffn1_w1[l].astype(BF16)
        w2a = ffn1_w2[l].astype(BF16)
        w1b = ffn2_w1[l].astype(BF16)
        w2b = ffn2_w2[l].astype(BF16)
        w_front = w_in[l, :, :FRONT_WIDTH].astype(BF16)
        w_g = w_in[l, :, FRONT_WIDTH:].astype(BF16)
        pool_w_b = pool_w[l].astype(BF16)
        w_ua = w_up_attn[l].astype(BF16)
        w_upl = w_up_pool[l].astype(BF16)
        w_o = w_out[l].astype(BF16)

        mod = _modulation_table(cond, w_mod[l], b_mod[l]).reshape(MOD_ROWS, N_MOD, D_MODEL)
        back = functools.partial(
            _back, w_g=w_g, pool_w=pool_w_b, pool_scale=pool_scale[l], w_up_attn=w_ua,
            w_up_pool=w_upl, w_out=w_o, ln_g1=ln_g[l, 1], ln_b1=ln_b[l, 1], w1=w1b, w2=w2b,
            ln_g2=ln_g[l, 2], ln_b2=ln_b[l, 2], alpha=alpha)

        x1, attn, k_l, v_l, pin = _front_ctx(
            yp, mod, ctx_row, w1a, w2a, ln_g[l, 0], ln_b[l, 0], w_front,
            q_norm_g[l], k_norm_g[l], alpha=alpha)
        yp = back(x1, attn, pin, mod, ctx_row, seq_len=seq)
        new_k.append(k_l.reshape(batch, seq, N_KV_HEADS, HEAD_DIM))
        new_v.append(v_l.reshape(batch, seq, N_KV_HEADS, HEAD_DIM))

        x1, q, kb, vb, pin = _front_lat(
            ys, mod, lat_row, w1a, w2a, ln_g[l, 0], ln_b[l, 0], w_front,
            q_norm_g[l], k_norm_g[l], cos2, sin2, n_lat=n_lat, alpha=alpha)
        attn = _attn_lat(q, kb, vb,
                         cache_k[:, l].reshape(dec_batch, past, KV_WIDTH),
                         cache_v[:, l].reshape(dec_batch, past, KV_WIDTH), n_lat)
        ys = back(x1, attn, pin, mod, lat_row, seq_len=n_lat)

    return (yp.reshape(batch, seq, D_MODEL),
            ys.reshape(dec_batch, n_lat, D_MODEL),
            jnp.stack(new_k, axis=1),
            jnp.stack(new_v, axis=1))
```

```python
import functools
import math

import numpy as np

import jax
import jax.numpy as jnp
from jax import lax
from jax.experimental import pallas as pl
from jax.experimental.pallas import tpu as pltpu

D_MODEL = 1024
GRID_W = 64
N_HEADS = 8
N_KV_HEADS = 2
HEAD_DIM = 128
GROUP = N_HEADS // N_KV_HEADS
ATTN_WIDTH = N_HEADS * HEAD_DIM
KV_WIDTH = N_KV_HEADS * HEAD_DIM
POOL_WINDOWS = (2, 4, 8, 16)
N_POOL_GROUPS = 4
POOL_GROUP_DIM = 128
POOL_WIDTH = N_POOL_GROUPS * POOL_GROUP_DIM
QKV_WIDTH = ATTN_WIDTH + 2 * KV_WIDTH
FRONT_WIDTH = QKV_WIDTH + POOL_WIDTH
D_FF = 2816
N_MOD = 9
ROPE_THETA = 10000.0
LN_EPS = 1e-6
RMS_EPS = 1e-6

SUBLANES = 8
LANES = 128
MXU_DIM = 256
VMEM_LIMIT_BYTES = 56 * 1024 * 1024

HALO = SUBLANES
MOD_ROWS = SUBLANES
TOKEN_TILE = 512
SUB_TILE = 256
Q_TILE = 256
KV_CHUNK = 256
FF_CHUNKS = ((0, 768), (768, 1024), (1792, 1024))
Q_SCALE = HEAD_DIM ** -0.5 * math.log2(math.e)
V_AUG = MXU_DIM

BF16 = jnp.bfloat16
F32 = jnp.float32


def _dot(a, b):
    return jnp.dot(a, b, preferred_element_type=F32)


def _dot_nt(a, b):
    return lax.dot_general(a, b, (((1,), (1,)), ((), ())), preferred_element_type=F32)


def _layer_norm(x, g, b):
    mu = jnp.mean(x, axis=-1, keepdims=True)
    xc = x - mu
    var = jnp.mean(xc * xc, axis=-1, keepdims=True)
    return xc * lax.rsqrt(var + LN_EPS) * g + b


def _modulate_bf16(x, shift, scale):
    return (x * (1.0 + scale) + shift).astype(BF16)


def _resident(shape):
    zeros = (0,) * len(shape)
    return pl.BlockSpec(shape, lambda *_: zeros, pipeline_mode=pl.Buffered(1))


def _compiler_params(n_axes):
    return pltpu.CompilerParams(
        dimension_semantics=("arbitrary",) * n_axes,
        vmem_limit_bytes=VMEM_LIMIT_BYTES,
    )


def _mod_kernel(cond_ref, w_ref, b_ref, o_ref):
    c = cond_ref[...]
    s = (c * jax.nn.sigmoid(c)).astype(BF16)
    o_ref[...] = _dot(s, w_ref[...].astype(BF16)) + b_ref[...]


def _modulation_table(cond, w_mod, b_mod):
    n_out = w_mod.shape[1]
    bn = 1152
    return pl.pallas_call(
        _mod_kernel,
        grid=(n_out // bn,),
        in_specs=[
            pl.BlockSpec((MOD_ROWS, D_MODEL), lambda j: (0, 0)),
            pl.BlockSpec((D_MODEL, bn), lambda j: (0, j)),
            pl.BlockSpec((1, bn), lambda j: (0, j)),
        ],
        out_specs=pl.BlockSpec((MOD_ROWS, bn), lambda j: (0, j)),
        out_shape=jax.ShapeDtypeStruct((MOD_ROWS, n_out), F32),
        compiler_params=_compiler_params(1),
        name="modulation",
    )(cond, w_mod, b_mod.reshape(1, n_out))


def _ffn_rows(xs, mod_ref, mod_base, w1_ref, w2_ref, ln_g, ln_b, alpha):
    shift = mod_ref[mod_base:mod_base + 1, :]
    scale = mod_ref[mod_base + 1:mod_base + 2, :]
    gate = mod_ref[mod_base + 2:mod_base + 3, :]
    hs = [_modulate_bf16(x, shift, scale) for x in xs]
    fs = [None] * len(xs)
    for start, width in FF_CHUNKS:
        us = []
        for h in hs:
            a = _dot(h, w1_ref[:, start:start + width])
            b = _dot(h, w1_ref[:, D_FF + start:D_FF + start + width])
            us.append((a * jax.nn.sigmoid(a) * b).astype(BF16))
        for i, u in enumerate(us):
            part = _dot(u, w2_ref[start:start + width, :])
            fs[i] = part if fs[i] is None else fs[i] + part
    return [_layer_norm(alpha * x + (0.5 * gate) * f, ln_g, ln_b) for x, f in zip(xs, fs)]


def _sub_tiles(n_rows):
    return [slice(r0, r0 + SUB_TILE) for r0 in range(0, n_rows, SUB_TILE)]


def _rms_head(xh, g):
    ms = jnp.mean(xh * xh, axis=-1, keepdims=True)
    return xh * lax.rsqrt(ms + RMS_EPS) * g


def _rope(xh, cos2, sin2):
    return xh * cos2 + pltpu.roll(xh, HEAD_DIM // 2, axis=1) * sin2


def _ones_column_block(rows):
    col = lax.broadcasted_iota(jnp.int32, (rows, V_AUG - HEAD_DIM), 1)
    return jnp.where(col == 0, 1.0, 0.0).astype(BF16)


def _stack_heads(heads):
    return jnp.concatenate(heads, axis=0)


def _head_cols(hh):
    return slice(hh * HEAD_DIM, (hh + 1) * HEAD_DIM)


def _front_ctx_kernel(x_ref, mod_ref, w1_ref, w2_ref, lng_ref, lnb_ref, win_ref, qg_ref, kg_ref,
                      x1_ref, a_ref, k_ref, v_ref, pin_ref, *, alpha):
    qg = qg_ref[...] * Q_SCALE
    kg = kg_ref[...]
    tiles = _sub_tiles(x_ref.shape[0])
    x1s = _ffn_rows([x_ref[rows, :] for rows in tiles], mod_ref, 0, w1_ref, w2_ref,
                    lng_ref[...], lnb_ref[...], alpha)
    gw = GROUP * HEAD_DIM
    scores = []
    for rows, x1 in zip(tiles, x1s):
        x1_ref[rows, :] = x1
        h = _modulate_bf16(x1, mod_ref[3:4, :], mod_ref[4:5, :])
        q_cols = [_dot(h, win_ref[:, :gw]), _dot(h, win_ref[:, gw:ATTN_WIDTH])]
        q4s = [_stack_heads([_rms_head(q_cols[0][:, _head_cols(g)], qg).astype(BF16)
                             for g in range(GROUP)])]
        kv_cols = _dot(h, win_ref[:, ATTN_WIDTH:QKV_WIDTH])
        q4s.append(_stack_heads([_rms_head(q_cols[1][:, _head_cols(g)], qg).astype(BF16)
                                 for g in range(GROUP)]))
        pin_ref[rows, :] = _dot(h, win_ref[:, QKV_WIDTH:])
        v = kv_cols[:, KV_WIDTH:]
        v_ref[rows, :] = v
        vb = v.astype(BF16)
        for kv in range(N_KV_HEADS):
            kn = _rms_head(kv_cols[:, _head_cols(kv)], kg)
            k_ref[rows, _head_cols(kv)] = kn
            scores.append((rows, kv, _dot_nt(q4s[kv], kn.astype(BF16)), vb[:, _head_cols(kv)]))
    probs = []
    for rows, kv, s, vb_kv in scores:
        p = jnp.exp2(s - jnp.max(s, axis=-1, keepdims=True))
        probs.append((rows, kv, p.astype(BF16), jnp.sum(p, axis=-1, keepdims=True), vb_kv))
    for rows, kv, p, l, vb_kv in probs:
        o = _dot(p, vb_kv) / l
        for g in range(GROUP):
            a_ref[rows, _head_cols(kv * GROUP + g)] = (
                o[g * SUB_TILE:(g + 1) * SUB_TILE].astype(BF16))


def _front_lat_kernel(x_ref, mod_ref, w1_ref, w2_ref, lng_ref, lnb_ref, win_ref, qg_ref, kg_ref,
                      cos_ref, sin_ref, x1_ref, q_ref, kb_ref, vb_ref, pin_ref, *, alpha):
    qg = qg_ref[...] * Q_SCALE
    kg = kg_ref[...]
    ones_col = _ones_column_block(SUB_TILE)
    tiles = _sub_tiles(x_ref.shape[0])
    x1s = _ffn_rows([x_ref[rows, :] for rows in tiles], mod_ref, 0, w1_ref, w2_ref,
                    lng_ref[...], lnb_ref[...], alpha)
    gw = GROUP * HEAD_DIM
    for rows, x1 in zip(tiles, x1s):
        x1_ref[rows, :] = x1
        h = _modulate_bf16(x1, mod_ref[3:4, :], mod_ref[4:5, :])
        cos2 = cos_ref[rows, :]
        sin2 = sin_ref[rows, :]

        def q_epilogue(kv, q_cols):
            for g in range(GROUP):
                q_ref[rows, _head_cols(kv * GROUP + g)] = _rope(
                    _rms_head(q_cols[:, _head_cols(g)], qg), cos2, sin2).astype(BF16)

        q_lo = _dot(h, win_ref[:, :gw])
        q_hi = _dot(h, win_ref[:, gw:ATTN_WIDTH])
        q_epilogue(0, q_lo)
        kv_cols = _dot(h, win_ref[:, ATTN_WIDTH:QKV_WIDTH])
        q_epilogue(1, q_hi)
        pin_ref[rows, :] = _dot(h, win_ref[:, QKV_WIDTH:])
        for kv in range(N_KV_HEADS):
            kn = _rms_head(kv_cols[:, _head_cols(kv)], kg)
            kb_ref[rows, _head_cols(kv)] = _rope(kn, cos2, sin2).astype(BF16)
            vb_ref[rows, kv * V_AUG:kv * V_AUG + HEAD_DIM] = (
                kv_cols[:, KV_WIDTH + kv * HEAD_DIM:KV_WIDTH + (kv + 1) * HEAD_DIM].astype(BF16))
            vb_ref[rows, kv * V_AUG + HEAD_DIM:(kv + 1) * V_AUG] = ones_col


def _front_in_specs(tm, mod_row_of_token, w1, w2, w_front):
    return [
        pl.BlockSpec((tm, D_MODEL), lambda i: (i, 0)),
        pl.BlockSpec((None, N_MOD, D_MODEL), lambda i: (mod_row_of_token(i * tm), 0, 0)),
        _resident(w1.shape),
        _resident(w2.shape),
        _resident((1, D_MODEL)),
        _resident((1, D_MODEL)),
        _resident(w_front.shape),
        _resident((1, HEAD_DIM)),
        _resident((1, HEAD_DIM)),
    ]


def _front_ctx(x2d, mod, mod_row_of_token, w1, w2, ln_g, ln_b, w_front, q_g, k_g, *, alpha):
    n_tok = x2d.shape[0]
    tm = TOKEN_TILE
    tok = lambda width: pl.BlockSpec((tm, width), lambda i: (i, 0))
    return pl.pallas_call(
        functools.partial(_front_ctx_kernel, alpha=alpha),
        grid=(n_tok // tm,),
        in_specs=_front_in_specs(tm, mod_row_of_token, w1, w2, w_front),
        out_specs=[tok(D_MODEL), tok(ATTN_WIDTH), tok(KV_WIDTH), tok(KV_WIDTH), tok(POOL_WIDTH)],
        out_shape=[
            jax.ShapeDtypeStruct((n_tok, D_MODEL), F32),
            jax.ShapeDtypeStruct((n_tok, ATTN_WIDTH), BF16),
            jax.ShapeDtypeStruct((n_tok, KV_WIDTH), F32),
            jax.ShapeDtypeStruct((n_tok, KV_WIDTH), F32),
            jax.ShapeDtypeStruct((n_tok, POOL_WIDTH), F32),
        ],
        compiler_params=_compiler_params(1),
        name="front_ctx",
    )(x2d, mod, w1, w2, ln_g.reshape(1, D_MODEL), ln_b.reshape(1, D_MODEL), w_front,
      q_g.reshape(1, HEAD_DIM), k_g.reshape(1, HEAD_DIM))


def _front_lat(x2d, mod, mod_row_of_token, w1, w2, ln_g, ln_b, w_front, q_g, k_g, cos2, sin2,
               *, n_lat, alpha):
    n_tok = x2d.shape[0]
    tm = TOKEN_TILE
    tiles_per_seq = n_lat // tm
    tok = lambda width: pl.BlockSpec((tm, width), lambda i: (i, 0))
    rope_spec = pl.BlockSpec((tm, HEAD_DIM), lambda i: (i % tiles_per_seq, 0))
    return pl.pallas_call(
        functools.partial(_front_lat_kernel, alpha=alpha),
        grid=(n_tok // tm,),
        in_specs=_front_in_specs(tm, mod_row_of_token, w1, w2, w_front) + [rope_spec, rope_spec],
        out_specs=[tok(D_MODEL), tok(ATTN_WIDTH), tok(KV_WIDTH), tok(N_KV_HEADS * V_AUG),
                   tok(POOL_WIDTH)],
        out_shape=[
            jax.ShapeDtypeStruct((n_tok, D_MODEL), F32),
            jax.ShapeDtypeStruct((n_tok, ATTN_WIDTH), BF16),
            jax.ShapeDtypeStruct((n_tok, KV_WIDTH), BF16),
            jax.ShapeDtypeStruct((n_tok, N_KV_HEADS * V_AUG), BF16),
            jax.ShapeDtypeStruct((n_tok, POOL_WIDTH), F32),
        ],
        compiler_params=_compiler_params(1),
        name="front_lat",
    )(x2d, mod, w1, w2, ln_g.reshape(1, D_MODEL), ln_b.reshape(1, D_MODEL), w_front,
      q_g.reshape(1, HEAD_DIM), k_g.reshape(1, HEAD_DIM), cos2, sin2)


def _attn_lat_kernel(q_ref, kc_ref, vc_ref, kl_ref, vl_ref, o_ref):
    t = q_ref.shape[0]
    n_lat = kl_ref.shape[0]
    past = kc_ref.shape[0]
    q4 = _stack_heads([q_ref[:, _head_cols(g)] for g in range(GROUP)])
    vc = jnp.concatenate([vc_ref[...].astype(BF16), _ones_column_block(past)], axis=1)
    chunks = [(kc_ref[...].astype(BF16), vc)]
    for c0 in range(0, n_lat, KV_CHUNK):
        chunks.append((kl_ref[c0:c0 + KV_CHUNK, :], vl_ref[c0:c0 + KV_CHUNK, :]))
    m = None
    acc = None
    for k_c, v_c in chunks:
        s = _dot_nt(q4, k_c)
        m_c = jnp.max(s, axis=-1, keepdims=True)
        if m is None:
            m = m_c
            acc = _dot(jnp.exp2(s - m).astype(BF16), v_c)
        else:
            m_new = jnp.maximum(m, m_c)
            acc = jnp.exp2(m - m_new) * acc + _dot(jnp.exp2(s - m_new).astype(BF16), v_c)
            m = m_new
    o = acc[:, :HEAD_DIM] / acc[:, HEAD_DIM:HEAD_DIM + 1]
    for g in range(GROUP):
        o_ref[:, _head_cols(g)] = o[g * t:(g + 1) * t].astype(BF16)


def _attn_lat(q, kb, vb, cache_k, cache_v, n_lat):
    n_tok = q.shape[0]
    n_batch = n_tok // n_lat
    past = cache_k.shape[1]
    tq = Q_TILE
    q_tiles = n_lat // tq
    gw = GROUP * HEAD_DIM
    return pl.pallas_call(
        _attn_lat_kernel,
        grid=(n_batch, N_KV_HEADS, q_tiles),
        in_specs=[
            pl.BlockSpec((tq, gw), lambda b, kv, i: (b * q_tiles + i, kv)),
            pl.BlockSpec((None, past, HEAD_DIM), lambda b, kv, i: (b, 0, kv)),
            pl.BlockSpec((None, past, HEAD_DIM), lambda b, kv, i: (b, 0, kv)),
            pl.BlockSpec((n_lat, HEAD_DIM), lambda b, kv, i: (b, kv)),
            pl.BlockSpec((n_lat, V_AUG), lambda b, kv, i: (b, kv)),
        ],
        out_specs=pl.BlockSpec((tq, gw), lambda b, kv, i: (b * q_tiles + i, kv)),
        out_shape=jax.ShapeDtypeStruct((n_tok, ATTN_WIDTH), BF16),
        compiler_params=_compiler_params(3),
        name="attn_lat",
    )(q, cache_k, cache_v, kb, vb)


def _window_sums(e):
    n = e.shape[0]
    back = lambda a, k: pltpu.roll(a, k, axis=0)
    fwd = lambda a, k: pltpu.roll(a, n - k, axis=0)
    c2 = e + back(e, 1)
    c4 = back(c2, 1) + fwd(c2, 1)
    c8 = back(c4, 2) + fwd(c4, 2)
    c16 = back(c8, 4) + fwd(c8, 4)
    return c2, c4, c8, c16


def _back_kernel(x_ref, a_ref, pin_ref, pp_ref, pn_ref, mod_ref, wg_ref, poolw_ref, pscale_ref,
                 wua_ref, wup_ref, wo_ref, lng1_ref, lnb1_ref, w1_ref, w2_ref, lng2_ref, lnb2_ref,
                 o_ref, *, seq_len, alpha):
    tm = x_ref.shape[0]
    tile_start = pl.program_id(0) * tm
    shift = mod_ref[3:4, :]
    scale = mod_ref[4:5, :]
    gate = mod_ref[5:6, :]
    p_ext = jnp.concatenate([pp_ref[...], pin_ref[...], pn_ref[...]], axis=0)

    tiles = _sub_tiles(tm)
    xs = [x_ref[rows, :] for rows in tiles]

    gated = []
    pooled_up = []
    half = D_MODEL // 2
    for rows, x in zip(tiles, xs):
        h = _modulate_bf16(x, shift, scale)
        a_blk = a_ref[rows, :]
        big_dots = [
            lambda: _dot(h, wg_ref[:, :D_MODEL]),
            lambda: _dot(h, wg_ref[:, D_MODEL:]),
            lambda: _dot(a_blk, wua_ref[:, :half]),
            lambda: _dot(a_blk, wua_ref[:, half:]),
        ]
        big = []

        pos = (tile_start + rows.start) % seq_len
        e_all = p_ext[rows.start:rows.start + SUB_TILE + 2 * HALO, :]
        row = lax.broadcasted_iota(jnp.int32, (SUB_TILE + 2 * HALO, 1), 0)
        first_row = jnp.where(pos == 0, HALO, 0)
        end_row = jnp.where(pos + SUB_TILE == seq_len, HALO + SUB_TILE, SUB_TILE + 2 * HALO)
        e_all = jnp.where(jnp.logical_or(row < first_row, row >= end_row), 0.0, e_all)
        t = pos + lax.broadcasted_iota(jnp.int32, (SUB_TILE, 1), 0)
        pgs = []
        for g, w in enumerate(POOL_WINDOWS):
            big.append(big_dots[g]())
            sl = slice(g * POOL_GROUP_DIM, (g + 1) * POOL_GROUP_DIM)
            e = e_all[:, sl]
            win = _window_sums(e)[g][HALO:HALO + SUB_TILE]
            lo = jnp.maximum(t - w // 2, 0)
            hi = jnp.minimum(t + w // 2, seq_len)
            inv_cnt = 1.0 / (hi - lo).astype(F32)
            pooled = (win * inv_cnt - e[HALO:HALO + SUB_TILE]).astype(BF16)
            pgs.append(_dot(pooled, poolw_ref[g]))

        ga, gp, au_lo, au_hi = big
        pg = jnp.concatenate(pgs, axis=1)
        gated.append((jax.nn.sigmoid(ga) * jnp.concatenate([au_lo, au_hi], axis=1), gp))
        pooled_up.append(_dot((pg * pscale_ref[...]).astype(BF16), wup_ref[...]))

    merged = [(ga_up + jax.nn.sigmoid(gp) * pu).astype(BF16)
              for (ga_up, gp), pu in zip(gated, pooled_up)]
    mixes = [_dot(m, wo_ref[...]) for m in merged]
    x2s = [_layer_norm(alpha * x + gate * mix, lng1_ref[...], lnb1_ref[...])
           for x, mix in zip(xs, mixes)]

    ys = _ffn_rows(x2s, mod_ref, 6, w1_ref, w2_ref, lng2_ref[...], lnb2_ref[...], alpha)
    for rows, y in zip(tiles, ys):
        o_ref[rows, :] = y


def _back(x2d, attn, pin, mod, mod_row_of_token, w_g, pool_w, pool_scale, w_up_attn, w_up_pool,
          w_out, ln_g1, ln_b1, w1, w2, ln_g2, ln_b2, *, seq_len, alpha):
    n_tok = x2d.shape[0]
    tm = TOKEN_TILE
    halo_per_tile = tm // HALO
    n_halo_blocks = n_tok // HALO
    row_vec = lambda v: v.reshape(1, D_MODEL)
    return pl.pallas_call(
        functools.partial(_back_kernel, seq_len=seq_len, alpha=alpha),
        grid=(n_tok // tm,),
        in_specs=[
            pl.BlockSpec((tm, D_MODEL), lambda i: (i, 0)),
            pl.BlockSpec((tm, ATTN_WIDTH), lambda i: (i, 0)),
            pl.BlockSpec((tm, POOL_WIDTH), lambda i: (i, 0)),
            pl.BlockSpec((HALO, POOL_WIDTH), lambda i: (jnp.maximum(i * halo_per_tile - 1, 0), 0)),
            pl.BlockSpec((HALO, POOL_WIDTH),
                         lambda i: (jnp.minimum((i + 1) * halo_per_tile, n_halo_blocks - 1), 0)),
            pl.BlockSpec((None, N_MOD, D_MODEL), lambda i: (mod_row_of_token(i * tm), 0, 0)),
            _resident(w_g.shape),
            _resident(pool_w.shape),
            _resident((1, POOL_WIDTH)),
            _resident(w_up_attn.shape),
            _resident(w_up_pool.shape),
            _resident(w_out.shape),
            _resident((1, D_MODEL)),
            _resident((1, D_MODEL)),
            _resident(w1.shape),
            _resident(w2.shape),
            _resident((1, D_MODEL)),
            _resident((1, D_MODEL)),
        ],
        out_specs=pl.BlockSpec((tm, D_MODEL), lambda i: (i, 0)),
        out_shape=jax.ShapeDtypeStruct((n_tok, D_MODEL), F32),
        compiler_params=_compiler_params(1),
        name="back",
    )(x2d, attn, pin, pin, pin, mod, w_g, pool_w, pool_scale.reshape(1, POOL_WIDTH),
      w_up_attn, w_up_pool, w_out, row_vec(ln_g1), row_vec(ln_b1), w1, w2,
      row_vec(ln_g2), row_vec(ln_b2))


def _rope_tables(n_tokens):
    t = np.arange(n_tokens)
    n_freq = HEAD_DIM // 4
    inv_freq = ROPE_THETA ** (-np.arange(n_freq, dtype=np.float64) / n_freq)
    ang = np.concatenate([(t // GRID_W)[:, None] * inv_freq, (t % GRID_W)[:, None] * inv_freq],
                         axis=-1)
    cos, sin = np.cos(ang), np.sin(ang)
    return (jnp.asarray(np.concatenate([cos, cos], axis=-1), F32),
            jnp.asarray(np.concatenate([-sin, sin], axis=-1), F32))


def kernel(x_prompt, x_sample, cache_k, cache_v, c, c_ctx, w_mod, b_mod, ln_g, ln_b, ffn1_w1, ffn1_w2, w_in, q_norm_g, k_norm_g, pool_w, pool_scale, w_up_attn, w_up_pool, w_out, ffn2_w1, ffn2_w2):
    batch, seq, _ = x_prompt.shape
    dec_batch, n_lat, _ = x_sample.shape
    depth = w_mod.shape[0]
    past = cache_k.shape[2]
    alpha = (2.0 * depth) ** 0.25
    assert seq == SUB_TILE and (batch * seq) % TOKEN_TILE == 0
    assert n_lat % TOKEN_TILE == 0 and n_lat % Q_TILE == 0 and n_lat % KV_CHUNK == 0
    assert 1 + dec_batch <= MOD_ROWS

    cos2, sin2 = _rope_tables((n_lat // GRID_W) * GRID_W)
    cond = jnp.zeros((MOD_ROWS, D_MODEL), F32).at[0].set(c_ctx).at[1:1 + dec_batch].set(c)

    ctx_row = lambda t: 0
    lat_row = lambda t: 1 + t // n_lat

    yp = x_prompt.reshape(batch * seq, D_MODEL)
    ys = x_sample.reshape(dec_batch * n_lat, D_MODEL)
    new_k, new_v = [], []
    for l in range(depth):
        w1a = ffn1_w1[l].astype(BF16)
        w2a = ffn1_w2[l].astype(BF16)
        w1b = ffn2_w1[l].astype(BF16)
        w2b = ffn2_w2[l].astype(BF16)
        w_front = w_in[l, :, :FRONT_WIDTH].astype(BF16)
        w_g = w_in[l, :, FRONT_WIDTH:].astype(BF16)
        pool_w_b = pool_w[l].astype(BF16)
        w_ua = w_up_attn[l].astype(BF16)
        w_upl = w_up_pool[l].astype(BF16)
        w_o = w_out[l].astype(BF16)

        mod = _modulation_table(cond, w_mod[l], b_mod[l]).reshape(MOD_ROWS, N_MOD, D_MODEL)
        back = functools.partial(
            _back, w_g=w_g, pool_w=pool_w_b, pool_scale=pool_scale[l], w_up_attn=w_ua,
            w_up_pool=w_upl, w_out=w_o, ln_g1=ln_g[l, 1], ln_b1=ln_b[l, 1], w1=w1b, w2=w2b,
            ln_g2=ln_g[l, 2], ln_b2=ln_b[l, 2], alpha=alpha)

        x1, attn, k_l, v_l, pin = _front_ctx(
            yp, mod, ctx_row, w1a, w2a, ln_g[l, 0], ln_b[l, 0], w_front,
            q_norm_g[l], k_norm_g[l], alpha=alpha)
        yp = back(x1, attn, pin, mod, ctx_row, seq_len=seq)
        new_k.append(k_l.reshape(batch, seq, N_KV_HEADS, HEAD_DIM))
        new_v.append(v_l.reshape(batch, seq, N_KV_HEADS, HEAD_DIM))

        x1, q, kb, vb, pin = _front_lat(
            ys, mod, lat_row, w1a, w2a, ln_g[l, 0], ln_b[l, 0], w_front,
            q_norm_g[l], k_norm_g[l], cos2, sin2, n_lat=n_lat, alpha=alpha)
        attn = _attn_lat(q, kb, vb,
                         cache_k[:, l].reshape(dec_batch, past, KV_WIDTH),
                         cache_v[:, l].reshape(dec_batch, past, KV_WIDTH), n_lat)
        ys = back(x1, attn, pin, mod, lat_row, seq_len=n_lat)

    return (yp.reshape(batch, seq, D_MODEL),
            ys.reshape(dec_batch, n_lat, D_MODEL),
            jnp.stack(new_k, axis=1),
            jnp.stack(new_v, axis=1))
```

```python
import functools
import math

import numpy as np

import jax
import jax.numpy as jnp
from jax import lax
from jax.experimental import pallas as pl
from jax.experimental.pallas import tpu as pltpu

D_MODEL = 1024
GRID_W = 64
N_HEADS = 8
N_KV_HEADS = 2
HEAD_DIM = 128
GROUP = N_HEADS // N_KV_HEADS
ATTN_WIDTH = N_HEADS * HEAD_DIM
KV_WIDTH = N_KV_HEADS * HEAD_DIM
POOL_WINDOWS = (2, 4, 8, 16)
N_POOL_GROUPS = 4
POOL_GROUP_DIM = 128
POOL_WIDTH = N_POOL_GROUPS * POOL_GROUP_DIM
QKV_WIDTH = ATTN_WIDTH + 2 * KV_WIDTH
FRONT_WIDTH = QKV_WIDTH + POOL_WIDTH
D_FF = 2816
N_MOD = 9
ROPE_THETA = 10000.0
LN_EPS = 1e-6
RMS_EPS = 1e-6

SUBLANES = 8
LANES = 128
MXU_DIM = 256
VMEM_LIMIT_BYTES = 56 * 1024 * 1024

HALO = SUBLANES
MOD_ROWS = SUBLANES
TOKEN_TILE = 512
SUB_TILE = 256
Q_TILE = 256
BF16_SUBLANES = 2 * SUBLANES
KV_CHUNK = 512
SCORE_LEAD = 1
FF_CHUNKS = ((0, 768), (768, 1024), (1792, 1024))
Q_SCALE = HEAD_DIM ** -0.5 * math.log2(math.e)
V_ROWS = HEAD_DIM + BF16_SUBLANES

BF16 = jnp.bfloat16
F32 = jnp.float32


def _dot(a, b):
    return jnp.dot(a, b, preferred_element_type=F32)


def _dot_nt(a, b):
    return lax.dot_general(a, b, (((1,), (1,)), ((), ())), preferred_element_type=F32)


def _layer_norm(x, g, b):
    mu = jnp.mean(x, axis=-1, keepdims=True)
    xc = x - mu
    var = jnp.mean(xc * xc, axis=-1, keepdims=True)
    return xc * lax.rsqrt(var + LN_EPS) * g + b


def _modulate_bf16(x, shift, scale):
    return (x * (1.0 + scale) + shift).astype(BF16)


def _resident(shape):
    zeros = (0,) * len(shape)
    return pl.BlockSpec(shape, lambda *_: zeros, pipeline_mode=pl.Buffered(1))


def _compiler_params(n_axes):
    return pltpu.CompilerParams(
        dimension_semantics=("arbitrary",) * n_axes,
        vmem_limit_bytes=VMEM_LIMIT_BYTES,
    )


def _mod_kernel(cond_ref, w_ref, b_ref, o_ref):
    c = cond_ref[...]
    s = (c * jax.nn.sigmoid(c)).astype(BF16)
    o_ref[...] = _dot(s, w_ref[...].astype(BF16)) + b_ref[...]


def _modulation_table(cond, w_mod, b_mod):
    n_out = w_mod.shape[1]
    bn = 1152
    return pl.pallas_call(
        _mod_kernel,
        grid=(n_out // bn,),
        in_specs=[
            pl.BlockSpec((MOD_ROWS, D_MODEL), lambda j: (0, 0)),
            pl.BlockSpec((D_MODEL, bn), lambda j: (0, j)),
            pl.BlockSpec((1, bn), lambda j: (0, j)),
        ],
        out_specs=pl.BlockSpec((MOD_ROWS, bn), lambda j: (0, j)),
        out_shape=jax.ShapeDtypeStruct((MOD_ROWS, n_out), F32),
        compiler_params=_compiler_params(1),
        name="modulation",
    )(cond, w_mod, b_mod.reshape(1, n_out))


def _ffn_rows(xs, mod_ref, mod_base, w1_ref, w2_ref, ln_g, ln_b, alpha):
    shift = mod_ref[mod_base:mod_base + 1, :]
    scale = mod_ref[mod_base + 1:mod_base + 2, :]
    gate = mod_ref[mod_base + 2:mod_base + 3, :]
    hs = [_modulate_bf16(x, shift, scale) for x in xs]
    fs = [None] * len(xs)
    for start, width in FF_CHUNKS:
        us = []
        for h in hs:
            a = _dot(h, w1_ref[:, start:start + width])
            b = _dot(h, w1_ref[:, D_FF + start:D_FF + start + width])
            us.append((a * jax.nn.sigmoid(a) * b).astype(BF16))
        for i, u in enumerate(us):
            part = _dot(u, w2_ref[start:start + width, :])
            fs[i] = part if fs[i] is None else fs[i] + part
    return [_layer_norm(alpha * x + (0.5 * gate) * f, ln_g, ln_b) for x, f in zip(xs, fs)]


def _sub_tiles(n_rows):
    return [slice(r0, r0 + SUB_TILE) for r0 in range(0, n_rows, SUB_TILE)]


def _rms_head(xh, g):
    ms = jnp.mean(xh * xh, axis=-1, keepdims=True)
    return xh * lax.rsqrt(ms + RMS_EPS) * g


def _rope(xh, cos2, sin2):
    return xh * cos2 + pltpu.roll(xh, HEAD_DIM // 2, axis=1) * sin2


def _stack_heads(heads):
    return jnp.concatenate(heads, axis=0)


def _head_cols(hh):
    return slice(hh * HEAD_DIM, (hh + 1) * HEAD_DIM)


def _front_ctx_kernel(x_ref, mod_ref, w1_ref, w2_ref, lng_ref, lnb_ref, win_ref, qg_ref, kg_ref,
                      x1_ref, a_ref, k_ref, v_ref, pin_ref, *, alpha):
    qg = qg_ref[...] * Q_SCALE
    kg = kg_ref[...]
    tiles = _sub_tiles(x_ref.shape[0])
    x1s = _ffn_rows([x_ref[rows, :] for rows in tiles], mod_ref, 0, w1_ref, w2_ref,
                    lng_ref[...], lnb_ref[...], alpha)
    gw = GROUP * HEAD_DIM
    scores = []
    for rows, x1 in zip(tiles, x1s):
        x1_ref[rows, :] = x1
        h = _modulate_bf16(x1, mod_ref[3:4, :], mod_ref[4:5, :])
        q_cols = [_dot(h, win_ref[:, :gw]), _dot(h, win_ref[:, gw:ATTN_WIDTH])]
        q4s = [_stack_heads([_rms_head(q_cols[0][:, _head_cols(g)], qg).astype(BF16)
                             for g in range(GROUP)])]
        kv_cols = _dot(h, win_ref[:, ATTN_WIDTH:QKV_WIDTH])
        q4s.append(_stack_heads([_rms_head(q_cols[1][:, _head_cols(g)], qg).astype(BF16)
                                 for g in range(GROUP)]))
        pin_ref[rows, :] = _dot(h, win_ref[:, QKV_WIDTH:])
        v = kv_cols[:, KV_WIDTH:]
        v_ref[rows, :] = v
        vb = v.astype(BF16)
        for kv in range(N_KV_HEADS):
            kn = _rms_head(kv_cols[:, _head_cols(kv)], kg)
            k_ref[rows, _head_cols(kv)] = kn
            scores.append((rows, kv, _dot_nt(q4s[kv], kn.astype(BF16)), vb[:, _head_cols(kv)]))
    probs = []
    for rows, kv, s, vb_kv in scores:
        p = jnp.exp2(s - jnp.max(s, axis=-1, keepdims=True))
        probs.append((rows, kv, p.astype(BF16), jnp.sum(p, axis=-1, keepdims=True), vb_kv))
    for rows, kv, p, l, vb_kv in probs:
        o = _dot(p, vb_kv) / l
        for g in range(GROUP):
            a_ref[rows, _head_cols(kv * GROUP + g)] = (
                o[g * SUB_TILE:(g + 1) * SUB_TILE].astype(BF16))


def _front_lat_kernel(x_ref, mod_ref, w1_ref, w2_ref, lng_ref, lnb_ref, win_ref, qg_ref, kg_ref,
                      cos_ref, sin_ref, x1_ref, qt_ref, kb_ref, vt_ref, pin_ref, *, alpha):
    qg = qg_ref[...] * Q_SCALE
    kg = kg_ref[...]
    pad_row = lax.broadcasted_iota(jnp.int32, (V_ROWS - HEAD_DIM, SUB_TILE), 0)
    ones_rows = jnp.where(pad_row == 0, 1.0, 0.0).astype(BF16)
    tiles = _sub_tiles(x_ref.shape[0])
    x1s = _ffn_rows([x_ref[rows, :] for rows in tiles], mod_ref, 0, w1_ref, w2_ref,
                    lng_ref[...], lnb_ref[...], alpha)
    gw = GROUP * HEAD_DIM
    for rows, x1 in zip(tiles, x1s):
        x1_ref[rows, :] = x1
        h = _modulate_bf16(x1, mod_ref[3:4, :], mod_ref[4:5, :])
        cos2 = cos_ref[rows, :]
        sin2 = sin_ref[rows, :]

        def q_epilogue(kv, q_cols):
            for g in range(GROUP):
                qh = _rope(_rms_head(q_cols[:, _head_cols(g)], qg), cos2, sin2)
                qt_ref[kv * GROUP + g, :, rows] = qh.T.astype(BF16)

        q_lo = _dot(h, win_ref[:, :gw])
        q_hi = _dot(h, win_ref[:, gw:ATTN_WIDTH])
        q_epilogue(0, q_lo)
        kv_cols = _dot(h, win_ref[:, ATTN_WIDTH:QKV_WIDTH])
        q_epilogue(1, q_hi)
        pin_ref[rows, :] = _dot(h, win_ref[:, QKV_WIDTH:])
        for kv in range(N_KV_HEADS):
            kn = _rms_head(kv_cols[:, _head_cols(kv)], kg)
            kb_ref[rows, _head_cols(kv)] = _rope(kn, cos2, sin2).astype(BF16)
            v_kv = kv_cols[:, KV_WIDTH + kv * HEAD_DIM:KV_WIDTH + (kv + 1) * HEAD_DIM]
            vt_ref[kv, :HEAD_DIM, rows] = v_kv.T.astype(BF16)
            vt_ref[kv, HEAD_DIM:, rows] = ones_rows


def _front_in_specs(tm, mod_row_of_token, w1, w2, w_front):
    return [
        pl.BlockSpec((tm, D_MODEL), lambda i: (i, 0)),
        pl.BlockSpec((None, N_MOD, D_MODEL), lambda i: (mod_row_of_token(i * tm), 0, 0)),
        _resident(w1.shape),
        _resident(w2.shape),
        _resident((1, D_MODEL)),
        _resident((1, D_MODEL)),
        _resident(w_front.shape),
        _resident((1, HEAD_DIM)),
        _resident((1, HEAD_DIM)),
    ]


def _front_ctx(x2d, mod, mod_row_of_token, w1, w2, ln_g, ln_b, w_front, q_g, k_g, *, alpha):
    n_tok = x2d.shape[0]
    tm = TOKEN_TILE
    tok = lambda width: pl.BlockSpec((tm, width), lambda i: (i, 0))
    return pl.pallas_call(
        functools.partial(_front_ctx_kernel, alpha=alpha),
        grid=(n_tok // tm,),
        in_specs=_front_in_specs(tm, mod_row_of_token, w1, w2, w_front),
        out_specs=[tok(D_MODEL), tok(ATTN_WIDTH), tok(KV_WIDTH), tok(KV_WIDTH), tok(POOL_WIDTH)],
        out_shape=[
            jax.ShapeDtypeStruct((n_tok, D_MODEL), F32),
            jax.ShapeDtypeStruct((n_tok, ATTN_WIDTH), BF16),
            jax.ShapeDtypeStruct((n_tok, KV_WIDTH), F32),
            jax.ShapeDtypeStruct((n_tok, KV_WIDTH), F32),
            jax.ShapeDtypeStruct((n_tok, POOL_WIDTH), F32),
        ],
        compiler_params=_compiler_params(1),
        name="front_ctx",
    )(x2d, mod, w1, w2, ln_g.reshape(1, D_MODEL), ln_b.reshape(1, D_MODEL), w_front,
      q_g.reshape(1, HEAD_DIM), k_g.reshape(1, HEAD_DIM))


def _front_lat(x2d, mod, mod_row_of_token, w1, w2, ln_g, ln_b, w_front, q_g, k_g, cos2, sin2,
               *, n_lat, alpha):
    n_tok = x2d.shape[0]
    tm = TOKEN_TILE
    tiles_per_seq = n_lat // tm
    n_batch = n_tok // n_lat
    tok = lambda width: pl.BlockSpec((tm, width), lambda i: (i, 0))
    rope_spec = pl.BlockSpec((tm, HEAD_DIM), lambda i: (i % tiles_per_seq, 0))
    feat_major = lambda heads, feats: pl.BlockSpec(
        (None, heads, feats, tm), lambda i: (i // tiles_per_seq, 0, 0, i % tiles_per_seq))
    return pl.pallas_call(
        functools.partial(_front_lat_kernel, alpha=alpha),
        grid=(n_tok // tm,),
        in_specs=_front_in_specs(tm, mod_row_of_token, w1, w2, w_front) + [rope_spec, rope_spec],
        out_specs=[tok(D_MODEL), feat_major(N_HEADS, HEAD_DIM), tok(KV_WIDTH),
                   feat_major(N_KV_HEADS, V_ROWS), tok(POOL_WIDTH)],
        out_shape=[
            jax.ShapeDtypeStruct((n_tok, D_MODEL), F32),
            jax.ShapeDtypeStruct((n_batch, N_HEADS, HEAD_DIM, n_lat), BF16),
            jax.ShapeDtypeStruct((n_tok, KV_WIDTH), BF16),
            jax.ShapeDtypeStruct((n_batch, N_KV_HEADS, V_ROWS, n_lat), BF16),
            jax.ShapeDtypeStruct((n_tok, POOL_WIDTH), F32),
        ],
        compiler_params=_compiler_params(1),
        name="front_lat",
    )(x2d, mod, w1, w2, ln_g.reshape(1, D_MODEL), ln_b.reshape(1, D_MODEL), w_front,
      q_g.reshape(1, HEAD_DIM), k_g.reshape(1, HEAD_DIM), cos2, sin2)


def _attn_lat_kernel(qt_ref, kc_ref, vct_ref, kl_ref, vlt_ref, o_ref):
    tq = qt_ref.shape[-1]
    n_lat = kl_ref.shape[0]
    qt = jnp.concatenate([qt_ref[g] for g in range(GROUP)], axis=1)
    chunks = [(kc_ref[...].astype(BF16), vct_ref[...])]
    for c0 in range(0, n_lat, KV_CHUNK):
        chunks.append((kl_ref[c0:c0 + KV_CHUNK, :], vlt_ref[:, c0:c0 + KV_CHUNK]))
    m = None
    acc = None
    scores = []
    for c in range(len(chunks) + SCORE_LEAD):
        if c < len(chunks):
            scores.append(_dot(chunks[c][0], qt))
        if c < SCORE_LEAD:
            continue
        st = scores[c - SCORE_LEAD]
        vt_c = chunks[c - SCORE_LEAD][1]
        m_c = jnp.max(st, axis=0, keepdims=True)
        if m is None:
            m = m_c
            acc = _dot(vt_c, jnp.exp2(st - m).astype(BF16))
        else:
            m_new = jnp.maximum(m, m_c)
            acc = jnp.exp2(m - m_new) * acc + _dot(vt_c, jnp.exp2(st - m_new).astype(BF16))
            m = m_new
    ot = acc[:HEAD_DIM, :] / acc[HEAD_DIM:HEAD_DIM + 1, :]
    for g in range(GROUP):
        o_ref[:, _head_cols(g)] = ot[:, g * tq:(g + 1) * tq].T.astype(BF16)


def _attn_lat(qt, kb, vt, cache_k, cache_vt):
    n_batch, _, _, n_lat = qt.shape
    past = cache_k.shape[1]
    tq = Q_TILE
    q_tiles = n_lat // tq
    gw = GROUP * HEAD_DIM
    return pl.pallas_call(
        _attn_lat_kernel,
        grid=(n_batch, N_KV_HEADS, q_tiles),
        in_specs=[
            pl.BlockSpec((None, GROUP, HEAD_DIM, tq), lambda b, kv, i: (b, kv, 0, i)),
            pl.BlockSpec((None, past, HEAD_DIM), lambda b, kv, i: (b, 0, kv)),
            pl.BlockSpec((None, None, V_ROWS, past), lambda b, kv, i: (b, kv, 0, 0)),
            pl.BlockSpec((n_lat, HEAD_DIM), lambda b, kv, i: (b, kv)),
            pl.BlockSpec((None, None, V_ROWS, n_lat), lambda b, kv, i: (b, kv, 0, 0)),
        ],
        out_specs=pl.BlockSpec((tq, gw), lambda b, kv, i: (b * q_tiles + i, kv)),
        out_shape=jax.ShapeDtypeStruct((n_batch * n_lat, ATTN_WIDTH), BF16),
        compiler_params=_compiler_params(3),
        name="attn_lat",
    )(qt, cache_k, cache_vt, kb, vt)


def _window_sums(e):
    n = e.shape[0]
    back = lambda a, k: pltpu.roll(a, k, axis=0)
    fwd = lambda a, k: pltpu.roll(a, n - k, axis=0)
    c2 = e + back(e, 1)
    c4 = back(c2, 1) + fwd(c2, 1)
    c8 = back(c4, 2) + fwd(c4, 2)
    c16 = back(c8, 4) + fwd(c8, 4)
    return c2, c4, c8, c16


def _back_kernel(x_ref, a_ref, pin_ref, pp_ref, pn_ref, mod_ref, wg_ref, poolw_ref, pscale_ref,
                 wua_ref, wup_ref, wo_ref, lng1_ref, lnb1_ref, w1_ref, w2_ref, lng2_ref, lnb2_ref,
                 o_ref, *, seq_len, alpha):
    tm = x_ref.shape[0]
    tile_start = pl.program_id(0) * tm
    shift = mod_ref[3:4, :]
    scale = mod_ref[4:5, :]
    gate = mod_ref[5:6, :]
    p_ext = jnp.concatenate([pp_ref[...], pin_ref[...], pn_ref[...]], axis=0)

    tiles = _sub_tiles(tm)
    xs = [x_ref[rows, :] for rows in tiles]

    gated = []
    pooled_up = []
    half = D_MODEL // 2
    for rows, x in zip(tiles, xs):
        h = _modulate_bf16(x, shift, scale)
        a_blk = a_ref[rows, :]
        big_dots = [
            lambda: _dot(h, wg_ref[:, :D_MODEL]),
            lambda: _dot(h, wg_ref[:, D_MODEL:]),
            lambda: _dot(a_blk, wua_ref[:, :half]),
            lambda: _dot(a_blk, wua_ref[:, half:]),
        ]
        big = []

        pos = (tile_start + rows.start) % seq_len
        e_all = p_ext[rows.start:rows.start + SUB_TILE + 2 * HALO, :]
        row = lax.broadcasted_iota(jnp.int32, (SUB_TILE + 2 * HALO, 1), 0)
        first_row = jnp.where(pos == 0, HALO, 0)
        end_row = jnp.where(pos + SUB_TILE == seq_len, HALO + SUB_TILE, SUB_TILE + 2 * HALO)
        e_all = jnp.where(jnp.logical_or(row < first_row, row >= end_row), 0.0, e_all)
        t = pos + lax.broadcasted_iota(jnp.int32, (SUB_TILE, 1), 0)
        pgs = []
        for g, w in enumerate(POOL_WINDOWS):
            big.append(big_dots[g]())
            sl = slice(g * POOL_GROUP_DIM, (g + 1) * POOL_GROUP_DIM)
            e = e_all[:, sl]
            win = _window_sums(e)[g][HALO:HALO + SUB_TILE]
            lo = jnp.maximum(t - w // 2, 0)
            hi = jnp.minimum(t + w // 2, seq_len)
            inv_cnt = 1.0 / (hi - lo).astype(F32)
            pooled = (win * inv_cnt - e[HALO:HALO + SUB_TILE]).astype(BF16)
            pgs.append(_dot(pooled, poolw_ref[g]))

        ga, gp, au_lo, au_hi = big
        pg = jnp.concatenate(pgs, axis=1)
        gated.append((jax.nn.sigmoid(ga) * jnp.concatenate([au_lo, au_hi], axis=1), gp))
        pooled_up.append(_dot((pg * pscale_ref[...]).astype(BF16), wup_ref[...]))

    merged = [(ga_up + jax.nn.sigmoid(gp) * pu).astype(BF16)
              for (ga_up, gp), pu in zip(gated, pooled_up)]
    mixes = [_dot(m, wo_ref[...]) for m in merged]
    x2s = [_layer_norm(alpha * x + gate * mix, lng1_ref[...], lnb1_ref[...])
           for x, mix in zip(xs, mixes)]

    ys = _ffn_rows(x2s, mod_ref, 6, w1_ref, w2_ref, lng2_ref[...], lnb2_ref[...], alpha)
    for rows, y in zip(tiles, ys):
        o_ref[rows, :] = y


def _back(x2d, attn, pin, mod, mod_row_of_token, w_g, pool_w, pool_scale, w_up_attn, w_up_pool,
          w_out, ln_g1, ln_b1, w1, w2, ln_g2, ln_b2, *, seq_len, alpha):
    n_tok = x2d.shape[0]
    tm = TOKEN_TILE
    halo_per_tile = tm // HALO
    n_halo_blocks = n_tok // HALO
    row_vec = lambda v: v.reshape(1, D_MODEL)
    return pl.pallas_call(
        functools.partial(_back_kernel, seq_len=seq_len, alpha=alpha),
        grid=(n_tok // tm,),
        in_specs=[
            pl.BlockSpec((tm, D_MODEL), lambda i: (i, 0)),
            pl.BlockSpec((tm, ATTN_WIDTH), lambda i: (i, 0)),
            pl.BlockSpec((tm, POOL_WIDTH), lambda i: (i, 0)),
            pl.BlockSpec((HALO, POOL_WIDTH), lambda i: (jnp.maximum(i * halo_per_tile - 1, 0), 0)),
            pl.BlockSpec((HALO, POOL_WIDTH),
                         lambda i: (jnp.minimum((i + 1) * halo_per_tile, n_halo_blocks - 1), 0)),
            pl.BlockSpec((None, N_MOD, D_MODEL), lambda i: (mod_row_of_token(i * tm), 0, 0)),
            _resident(w_g.shape),
            _resident(pool_w.shape),
            _resident((1, POOL_WIDTH)),
            _resident(w_up_attn.shape),
            _resident(w_up_pool.shape),
            _resident(w_out.shape),
            _resident((1, D_MODEL)),
            _resident((1, D_MODEL)),
            _resident(w1.shape),
            _resident(w2.shape),
            _resident((1, D_MODEL)),
            _resident((1, D_MODEL)),
        ],
        out_specs=pl.BlockSpec((tm, D_MODEL), lambda i: (i, 0)),
        out_shape=jax.ShapeDtypeStruct((n_tok, D_MODEL), F32),
        compiler_params=_compiler_params(1),
        name="back",
    )(x2d, attn, pin, pin, pin, mod, w_g, pool_w, pool_scale.reshape(1, POOL_WIDTH),
      w_up_attn, w_up_pool, w_out, row_vec(ln_g1), row_vec(ln_b1), w1, w2,
      row_vec(ln_g2), row_vec(ln_b2))


def _rope_tables(n_tokens):
    t = np.arange(n_tokens)
    n_freq = HEAD_DIM // 4
    inv_freq = ROPE_THETA ** (-np.arange(n_freq, dtype=np.float64) / n_freq)
    ang = np.concatenate([(t // GRID_W)[:, None] * inv_freq, (t % GRID_W)[:, None] * inv_freq],
                         axis=-1)
    cos, sin = np.cos(ang), np.sin(ang)
    return (jnp.asarray(np.concatenate([cos, cos], axis=-1), F32),
            jnp.asarray(np.concatenate([-sin, sin], axis=-1), F32))


def kernel(x_prompt, x_sample, cache_k, cache_v, c, c_ctx, w_mod, b_mod, ln_g, ln_b, ffn1_w1, ffn1_w2, w_in, q_norm_g, k_norm_g, pool_w, pool_scale, w_up_attn, w_up_pool, w_out, ffn2_w1, ffn2_w2):
    batch, seq, _ = x_prompt.shape
    dec_batch, n_lat, _ = x_sample.shape
    depth = w_mod.shape[0]
    past = cache_k.shape[2]
    alpha = (2.0 * depth) ** 0.25
    assert seq == SUB_TILE and (batch * seq) % TOKEN_TILE == 0
    assert n_lat % TOKEN_TILE == 0 and n_lat % Q_TILE == 0 and n_lat % KV_CHUNK == 0
    assert 1 + dec_batch <= MOD_ROWS

    cos2, sin2 = _rope_tables((n_lat // GRID_W) * GRID_W)
    cond = jnp.zeros((MOD_ROWS, D_MODEL), F32).at[0].set(c_ctx).at[1:1 + dec_batch].set(c)

    ctx_row = lambda t: 0
    lat_row = lambda t: 1 + t // n_lat

    yp = x_prompt.reshape(batch * seq, D_MODEL)
    ys = x_sample.reshape(dec_batch * n_lat, D_MODEL)
    new_k, new_v = [], []
    for l in range(depth):
        w1a = ffn1_w1[l].astype(BF16)
        w2a = ffn1_w2[l].astype(BF16)
        w1b = ffn2_w1[l].astype(BF16)
        w2b = ffn2_w2[l].astype(BF16)
        w_front = w_in[l, :, :FRONT_WIDTH].astype(BF16)
        w_g = w_in[l, :, FRONT_WIDTH:].astype(BF16)
        pool_w_b = pool_w[l].astype(BF16)
        w_ua = w_up_attn[l].astype(BF16)
        w_upl = w_up_pool[l].astype(BF16)
        w_o = w_out[l].astype(BF16)

        mod = _modulation_table(cond, w_mod[l], b_mod[l]).reshape(MOD_ROWS, N_MOD, D_MODEL)
        back = functools.partial(
            _back, w_g=w_g, pool_w=pool_w_b, pool_scale=pool_scale[l], w_up_attn=w_ua,
            w_up_pool=w_upl, w_out=w_o, ln_g1=ln_g[l, 1], ln_b1=ln_b[l, 1], w1=w1b, w2=w2b,
            ln_g2=ln_g[l, 2], ln_b2=ln_b[l, 2], alpha=alpha)

        x1, attn, k_l, v_l, pin = _front_ctx(
            yp, mod, ctx_row, w1a, w2a, ln_g[l, 0], ln_b[l, 0], w_front,
            q_norm_g[l], k_norm_g[l], alpha=alpha)
        yp = back(x1, attn, pin, mod, ctx_row, seq_len=seq)
        new_k.append(k_l.reshape(batch, seq, N_KV_HEADS, HEAD_DIM))
        new_v.append(v_l.reshape(batch, seq, N_KV_HEADS, HEAD_DIM))

        x1, qt, kb, vt, pin = _front_lat(
            ys, mod, lat_row, w1a, w2a, ln_g[l, 0], ln_b[l, 0], w_front,
            q_norm_g[l], k_norm_g[l], cos2, sin2, n_lat=n_lat, alpha=alpha)
        pad_rows = jnp.zeros((dec_batch, N_KV_HEADS, V_ROWS - HEAD_DIM, past), F32).at[:, :, 0].set(1.0)
        cache_vt = jnp.concatenate(
            [jnp.transpose(cache_v[:, l], (0, 2, 3, 1)), pad_rows], axis=2).astype(BF16)
        attn = _attn_lat(qt, kb, vt, cache_k[:, l].reshape(dec_batch, past, KV_WIDTH), cache_vt)
        ys = back(x1, attn, pin, mod, lat_row, seq_len=n_lat)

    return (yp.reshape(batch, seq, D_MODEL),
            ys.reshape(dec_batch, n_lat, D_MODEL),
            jnp.stack(new_k, axis=1),
            jnp.stack(new_v, axis=1))
```

```python
import functools
import math

import numpy as np

import jax
import jax.numpy as jnp
from jax import lax
from jax.experimental import pallas as pl
from jax.experimental.pallas import tpu as pltpu

D_MODEL = 1024
GRID_W = 64
N_HEADS = 8
N_KV_HEADS = 2
HEAD_DIM = 128
GROUP = N_HEADS // N_KV_HEADS
ATTN_WIDTH = N_HEADS * HEAD_DIM
KV_WIDTH = N_KV_HEADS * HEAD_DIM
POOL_WINDOWS = (2, 4, 8, 16)
N_POOL_GROUPS = 4
POOL_GROUP_DIM = 128
POOL_WIDTH = N_POOL_GROUPS * POOL_GROUP_DIM
QKV_WIDTH = ATTN_WIDTH + 2 * KV_WIDTH
FRONT_WIDTH = QKV_WIDTH + POOL_WIDTH
D_FF = 2816
N_MOD = 9
ROPE_THETA = 10000.0
LN_EPS = 1e-6
RMS_EPS = 1e-6

SUBLANES = 8
LANES = 128
MXU_DIM = 256
VMEM_LIMIT_BYTES = 56 * 1024 * 1024

HALO = SUBLANES
MOD_ROWS = SUBLANES
TOKEN_TILE = 512
SUB_TILE = 256
Q_TILE = 256
KV_CHUNK = 256
FF_CHUNKS = ((0, 768), (768, 1024), (1792, 1024))
Q_SCALE = HEAD_DIM ** -0.5 * math.log2(math.e)
V_AUG = MXU_DIM

BF16 = jnp.bfloat16
F32 = jnp.float32


def _dot(a, b):
    return jnp.dot(a, b, preferred_element_type=F32)


def _dot_nt(a, b):
    return lax.dot_general(a, b, (((1,), (1,)), ((), ())), preferred_element_type=F32)


def _layer_norm(x, g, b):
    mu = jnp.mean(x, axis=-1, keepdims=True)
    xc = x - mu
    var = jnp.mean(xc * xc, axis=-1, keepdims=True)
    return xc * lax.rsqrt(var + LN_EPS) * g + b


def _modulate_bf16(x, shift, scale):
    return (x * (1.0 + scale) + shift).astype(BF16)


def _resident(shape):
    zeros = (0,) * len(shape)
    return pl.BlockSpec(shape, lambda *_: zeros, pipeline_mode=pl.Buffered(1))


def _compiler_params(n_axes):
    return pltpu.CompilerParams(
        dimension_semantics=("arbitrary",) * n_axes,
        vmem_limit_bytes=VMEM_LIMIT_BYTES,
    )


def _mod_kernel(cond_ref, w_ref, b_ref, o_ref):
    c = cond_ref[...]
    s = (c * jax.nn.sigmoid(c)).astype(BF16)
    o_ref[...] = _dot(s, w_ref[...].astype(BF16)) + b_ref[...]


def _modulation_table(cond, w_mod, b_mod):
    n_out = w_mod.shape[1]
    bn = 1152
    return pl.pallas_call(
        _mod_kernel,
        grid=(n_out // bn,),
        in_specs=[
            pl.BlockSpec((MOD_ROWS, D_MODEL), lambda j: (0, 0)),
            pl.BlockSpec((D_MODEL, bn), lambda j: (0, j)),
            pl.BlockSpec((1, bn), lambda j: (0, j)),
        ],
        out_specs=pl.BlockSpec((MOD_ROWS, bn), lambda j: (0, j)),
        out_shape=jax.ShapeDtypeStruct((MOD_ROWS, n_out), F32),
        compiler_params=_compiler_params(1),
        name="modulation",
    )(cond, w_mod, b_mod.reshape(1, n_out))


def _ffn_rows(xs, mod_ref, mod_base, w1_ref, w2_ref, ln_g, ln_b, alpha):
    shift = mod_ref[mod_base:mod_base + 1, :]
    scale = mod_ref[mod_base + 1:mod_base + 2, :]
    gate = mod_ref[mod_base + 2:mod_base + 3, :]
    hs = [_modulate_bf16(x, shift, scale) for x in xs]
    fs = [None] * len(xs)
    for start, width in FF_CHUNKS:
        us = []
        for h in hs:
            a = _dot(h, w1_ref[:, start:start + width])
            b = _dot(h, w1_ref[:, D_FF + start:D_FF + start + width])
            us.append((a * jax.nn.sigmoid(a) * b).astype(BF16))
        for i, u in enumerate(us):
            part = _dot(u, w2_ref[start:start + width, :])
            fs[i] = part if fs[i] is None else fs[i] + part
    return [_layer_norm(alpha * x + (0.5 * gate) * f, ln_g, ln_b) for x, f in zip(xs, fs)]


def _sub_tiles(n_rows):
    return [slice(r0, r0 + SUB_TILE) for r0 in range(0, n_rows, SUB_TILE)]


def _rms_head(xh, g):
    ms = jnp.mean(xh * xh, axis=-1, keepdims=True)
    return xh * lax.rsqrt(ms + RMS_EPS) * g


def _rope(xh, cos2, sin2):
    return xh * cos2 + pltpu.roll(xh, HEAD_DIM // 2, axis=1) * sin2


def _ones_column_block(rows):
    col = lax.broadcasted_iota(jnp.int32, (rows, V_AUG - HEAD_DIM), 1)
    return jnp.where(col == 0, 1.0, 0.0).astype(BF16)


def _stack_heads(heads):
    return jnp.concatenate(heads, axis=0)


def _head_cols(hh):
    return slice(hh * HEAD_DIM, (hh + 1) * HEAD_DIM)


def _front_ctx_kernel(x_ref, mod_ref, w1_ref, w2_ref, lng_ref, lnb_ref, win_ref, qg_ref, kg_ref,
                      x1_ref, a_ref, k_ref, v_ref, pin_ref, *, alpha):
    qg = qg_ref[...] * Q_SCALE
    kg = kg_ref[...]
    tiles = _sub_tiles(x_ref.shape[0])
    x1s = _ffn_rows([x_ref[rows, :] for rows in tiles], mod_ref, 0, w1_ref, w2_ref,
                    lng_ref[...], lnb_ref[...], alpha)
    gw = GROUP * HEAD_DIM
    scores = []
    for rows, x1 in zip(tiles, x1s):
        x1_ref[rows, :] = x1
        h = _modulate_bf16(x1, mod_ref[3:4, :], mod_ref[4:5, :])
        q_cols = [_dot(h, win_ref[:, :gw]), _dot(h, win_ref[:, gw:ATTN_WIDTH])]
        q4s = [_stack_heads([_rms_head(q_cols[0][:, _head_cols(g)], qg).astype(BF16)
                             for g in range(GROUP)])]
        kv_cols = _dot(h, win_ref[:, ATTN_WIDTH:QKV_WIDTH])
        q4s.append(_stack_heads([_rms_head(q_cols[1][:, _head_cols(g)], qg).astype(BF16)
                                 for g in range(GROUP)]))
        pin_ref[rows, :] = _dot(h, win_ref[:, QKV_WIDTH:])
        for kv in range(N_KV_HEADS):
            kn = _rms_head(kv_cols[:, _head_cols(kv)], kg)
            v_kv = kv_cols[:, KV_WIDTH + kv * HEAD_DIM:KV_WIDTH + (kv + 1) * HEAD_DIM]
            k_ref[rows, kv, :] = kn
            v_ref[rows, kv, :] = v_kv
            scores.append((rows, kv, _dot_nt(q4s[kv], kn.astype(BF16)), v_kv.astype(BF16)))
    probs = []
    for rows, kv, s, vb_kv in scores:
        p = jnp.exp2(s - jnp.max(s, axis=-1, keepdims=True))
        probs.append((rows, kv, p.astype(BF16), jnp.sum(p, axis=-1, keepdims=True), vb_kv))
    for rows, kv, p, l, vb_kv in probs:
        o = _dot(p, vb_kv) / l
        for g in range(GROUP):
            a_ref[rows, _head_cols(kv * GROUP + g)] = (
                o[g * SUB_TILE:(g + 1) * SUB_TILE].astype(BF16))


def _front_lat_kernel(x_ref, mod_ref, w1_ref, w2_ref, lng_ref, lnb_ref, win_ref, qg_ref, kg_ref,
                      cos_ref, sin_ref, x1_ref, q_ref, kb_ref, vb_ref, pin_ref, *, alpha):
    qg = qg_ref[...] * Q_SCALE
    kg = kg_ref[...]
    ones_col = _ones_column_block(SUB_TILE)
    tiles = _sub_tiles(x_ref.shape[0])
    x1s = _ffn_rows([x_ref[rows, :] for rows in tiles], mod_ref, 0, w1_ref, w2_ref,
                    lng_ref[...], lnb_ref[...], alpha)
    for rows, x1 in zip(tiles, x1s):
        x1_ref[rows, :] = x1
        proj = _dot(_modulate_bf16(x1, mod_ref[3:4, :], mod_ref[4:5, :]), win_ref[...])
        pin_ref[rows, :] = proj[:, QKV_WIDTH:]
        cos2 = cos_ref[rows, :]
        sin2 = sin_ref[rows, :]
        for hh in range(N_HEADS):
            q_ref[rows, _head_cols(hh)] = _rope(
                _rms_head(proj[:, _head_cols(hh)], qg), cos2, sin2).astype(BF16)
        for kv in range(N_KV_HEADS):
            kn = _rms_head(proj[:, ATTN_WIDTH + kv * HEAD_DIM:ATTN_WIDTH + (kv + 1) * HEAD_DIM], kg)
            kb_ref[rows, _head_cols(kv)] = _rope(kn, cos2, sin2).astype(BF16)
            v0 = ATTN_WIDTH + KV_WIDTH + kv * HEAD_DIM
            vb_ref[rows, kv * V_AUG:kv * V_AUG + HEAD_DIM] = proj[:, v0:v0 + HEAD_DIM].astype(BF16)
            vb_ref[rows, kv * V_AUG + HEAD_DIM:(kv + 1) * V_AUG] = ones_col


def _front_in_specs(tm, mod_row_of_token, w1, w2, w_front):
    return [
        pl.BlockSpec((tm, D_MODEL), lambda i: (i, 0)),
        pl.BlockSpec((None, N_MOD, D_MODEL), lambda i: (mod_row_of_token(i * tm), 0, 0)),
        _resident(w1.shape),
        _resident(w2.shape),
        _resident((1, D_MODEL)),
        _resident((1, D_MODEL)),
        _resident(w_front.shape),
        _resident((1, HEAD_DIM)),
        _resident((1, HEAD_DIM)),
    ]


def _front_ctx(x2d, mod, mod_row_of_token, w1, w2, ln_g, ln_b, w_front, q_g, k_g, *, alpha):
    n_tok = x2d.shape[0]
    tm = TOKEN_TILE
    tok = lambda width: pl.BlockSpec((tm, width), lambda i: (i, 0))
    cache_spec = pl.BlockSpec((tm, N_KV_HEADS, HEAD_DIM), lambda i: (i, 0, 0))
    cache_shape = jax.ShapeDtypeStruct((n_tok, N_KV_HEADS, HEAD_DIM), F32)
    return pl.pallas_call(
        functools.partial(_front_ctx_kernel, alpha=alpha),
        grid=(n_tok // tm,),
        in_specs=_front_in_specs(tm, mod_row_of_token, w1, w2, w_front),
        out_specs=[tok(D_MODEL), tok(ATTN_WIDTH), cache_spec, cache_spec, tok(POOL_WIDTH)],
        out_shape=[
            jax.ShapeDtypeStruct((n_tok, D_MODEL), F32),
            jax.ShapeDtypeStruct((n_tok, ATTN_WIDTH), BF16),
            cache_shape,
            cache_shape,
            jax.ShapeDtypeStruct((n_tok, POOL_WIDTH), F32),
        ],
        compiler_params=_compiler_params(1),
        name="front_ctx",
    )(x2d, mod, w1, w2, ln_g.reshape(1, D_MODEL), ln_b.reshape(1, D_MODEL), w_front,
      q_g.reshape(1, HEAD_DIM), k_g.reshape(1, HEAD_DIM))


def _front_lat(x2d, mod, mod_row_of_token, w1, w2, ln_g, ln_b, w_front, q_g, k_g, cos2, sin2,
               *, n_lat, alpha):
    n_tok = x2d.shape[0]
    tm = TOKEN_TILE
    tiles_per_seq = n_lat // tm
    tok = lambda width: pl.BlockSpec((tm, width), lambda i: (i, 0))
    rope_spec = pl.BlockSpec((tm, HEAD_DIM), lambda i: (i % tiles_per_seq, 0))
    return pl.pallas_call(
        functools.partial(_front_lat_kernel, alpha=alpha),
        grid=(n_tok // tm,),
        in_specs=_front_in_specs(tm, mod_row_of_token, w1, w2, w_front) + [rope_spec, rope_spec],
        out_specs=[tok(D_MODEL), tok(ATTN_WIDTH), tok(KV_WIDTH), tok(N_KV_HEADS * V_AUG),
                   tok(POOL_WIDTH)],
        out_shape=[
            jax.ShapeDtypeStruct((n_tok, D_MODEL), F32),
            jax.ShapeDtypeStruct((n_tok, ATTN_WIDTH), BF16),
            jax.ShapeDtypeStruct((n_tok, KV_WIDTH), BF16),
            jax.ShapeDtypeStruct((n_tok, N_KV_HEADS * V_AUG), BF16),
            jax.ShapeDtypeStruct((n_tok, POOL_WIDTH), F32),
        ],
        compiler_params=_compiler_params(1),
        name="front_lat",
    )(x2d, mod, w1, w2, ln_g.reshape(1, D_MODEL), ln_b.reshape(1, D_MODEL), w_front,
      q_g.reshape(1, HEAD_DIM), k_g.reshape(1, HEAD_DIM), cos2, sin2)


def _attn_lat_kernel(q_ref, kc_ref, vc_ref, kl_ref, vl_ref, o_ref):
    t = q_ref.shape[0]
    n_lat = kl_ref.shape[0]
    past = kc_ref.shape[0]
    q4 = _stack_heads([q_ref[:, _head_cols(g)] for g in range(GROUP)])
    vc = jnp.concatenate([vc_ref[...].astype(BF16), _ones_column_block(past)], axis=1)
    chunks = [(kc_ref[...].astype(BF16), vc)]
    for c0 in range(0, n_lat, KV_CHUNK):
        chunks.append((kl_ref[c0:c0 + KV_CHUNK, :], vl_ref[c0:c0 + KV_CHUNK, :]))
    m = None
    acc = None
    for k_c, v_c in chunks:
        s = _dot_nt(q4, k_c)
        m_c = jnp.max(s, axis=-1, keepdims=True)
        if m is None:
            m = m_c
            acc = _dot(jnp.exp2(s - m).astype(BF16), v_c)
        else:
            m_new = jnp.maximum(m, m_c)
            acc = jnp.exp2(m - m_new) * acc + _dot(jnp.exp2(s - m_new).astype(BF16), v_c)
            m = m_new
    o = acc[:, :HEAD_DIM] / acc[:, HEAD_DIM:HEAD_DIM + 1]
    for g in range(GROUP):
        o_ref[:, _head_cols(g)] = o[g * t:(g + 1) * t].astype(BF16)


def _attn_lat(q, kb, vb, cache_k, cache_v, n_lat):
    n_tok = q.shape[0]
    n_batch = n_tok // n_lat
    past = cache_k.shape[1]
    tq = Q_TILE
    q_tiles = n_lat // tq
    gw = GROUP * HEAD_DIM
    return pl.pallas_call(
        _attn_lat_kernel,
        grid=(n_batch, N_KV_HEADS, q_tiles),
        in_specs=[
            pl.BlockSpec((tq, gw), lambda b, kv, i: (b * q_tiles + i, kv)),
            pl.BlockSpec((None, past, HEAD_DIM), lambda b, kv, i: (b, 0, kv)),
            pl.BlockSpec((None, past, HEAD_DIM), lambda b, kv, i: (b, 0, kv)),
            pl.BlockSpec((n_lat, HEAD_DIM), lambda b, kv, i: (b, kv)),
            pl.BlockSpec((n_lat, V_AUG), lambda b, kv, i: (b, kv)),
        ],
        out_specs=pl.BlockSpec((tq, gw), lambda b, kv, i: (b * q_tiles + i, kv)),
        out_shape=jax.ShapeDtypeStruct((n_tok, ATTN_WIDTH), BF16),
        compiler_params=_compiler_params(3),
        name="attn_lat",
    )(q, cache_k, cache_v, kb, vb)


def _window_sums(e):
    n = e.shape[0]
    back = lambda a, k: pltpu.roll(a, k, axis=0)
    fwd = lambda a, k: pltpu.roll(a, n - k, axis=0)
    c2 = e + back(e, 1)
    c4 = back(c2, 1) + fwd(c2, 1)
    c8 = back(c4, 2) + fwd(c4, 2)
    c16 = back(c8, 4) + fwd(c8, 4)
    return c2, c4, c8, c16


def _back_kernel(x_ref, a_ref, pin_ref, pp_ref, pn_ref, mod_ref, wg_ref, poolw_ref, pscale_ref,
                 wua_ref, wup_ref, wo_ref, lng1_ref, lnb1_ref, w1_ref, w2_ref, lng2_ref, lnb2_ref,
                 o_ref, *, seq_len, alpha):
    tm = x_ref.shape[0]
    tile_start = pl.program_id(0) * tm
    shift = mod_ref[3:4, :]
    scale = mod_ref[4:5, :]
    gate = mod_ref[5:6, :]
    p_ext = jnp.concatenate([pp_ref[...], pin_ref[...], pn_ref[...]], axis=0)

    tiles = _sub_tiles(tm)
    xs = [x_ref[rows, :] for rows in tiles]

    gated = []
    pooled_up = []
    half = D_MODEL // 2
    for rows, x in zip(tiles, xs):
        h = _modulate_bf16(x, shift, scale)
        a_blk = a_ref[rows, :]
        big_operands = [
            (h, wg_ref, slice(0, D_MODEL)),
            (h, wg_ref, slice(D_MODEL, 2 * D_MODEL)),
            (a_blk, wua_ref, slice(0, half)),
            (a_blk, wua_ref, slice(half, D_MODEL)),
        ]
        big = []

        pos = (tile_start + rows.start) % seq_len
        e_all = p_ext[rows.start:rows.start + SUB_TILE + 2 * HALO, :]
        row = lax.broadcasted_iota(jnp.int32, (SUB_TILE + 2 * HALO, 1), 0)
        first_row = jnp.where(pos == 0, HALO, 0)
        end_row = jnp.where(pos + SUB_TILE == seq_len, HALO + SUB_TILE, SUB_TILE + 2 * HALO)
        e_all = jnp.where(jnp.logical_or(row < first_row, row >= end_row), 0.0, e_all)
        t = pos + lax.broadcasted_iota(jnp.int32, (SUB_TILE, 1), 0)
        pgs = []
        for g, w in enumerate(POOL_WINDOWS):
            lhs, w_ref, cols = big_operands[g]
            big.append(_dot(lhs, w_ref[:, cols]))
            sl = slice(g * POOL_GROUP_DIM, (g + 1) * POOL_GROUP_DIM)
            e = e_all[:, sl]
            win = _window_sums(e)[g][HALO:HALO + SUB_TILE]
            lo = jnp.maximum(t - w // 2, 0)
            hi = jnp.minimum(t + w // 2, seq_len)
            inv_cnt = 1.0 / (hi - lo).astype(F32)
            pooled = (win * inv_cnt - e[HALO:HALO + SUB_TILE]).astype(BF16)
            pgs.append(_dot(pooled, poolw_ref[g]))

        ga, gp, au_lo, au_hi = big
        pg = jnp.concatenate(pgs, axis=1)
        gated.append((jax.nn.sigmoid(ga) * jnp.concatenate([au_lo, au_hi], axis=1), gp))
        pooled_up.append(_dot((pg * pscale_ref[...]).astype(BF16), wup_ref[...]))

    merged = [(ga_up + jax.nn.sigmoid(gp) * pu).astype(BF16)
              for (ga_up, gp), pu in zip(gated, pooled_up)]
    mixes = [_dot(m, wo_ref[...]) for m in merged]
    x2s = [_layer_norm(alpha * x + gate * mix, lng1_ref[...], lnb1_ref[...])
           for x, mix in zip(xs, mixes)]

    ys = _ffn_rows(x2s, mod_ref, 6, w1_ref, w2_ref, lng2_ref[...], lnb2_ref[...], alpha)
    for rows, y in zip(tiles, ys):
        o_ref[rows, :] = y


def _back(x2d, attn, pin, mod, mod_row_of_token, w_g, pool_w, pool_scale, w_up_attn, w_up_pool,
          w_out, ln_g1, ln_b1, w1, w2, ln_g2, ln_b2, *, seq_len, alpha):
    n_tok = x2d.shape[0]
    tm = TOKEN_TILE
    halo_per_tile = tm // HALO
    n_halo_blocks = n_tok // HALO
    row_vec = lambda v: v.reshape(1, D_MODEL)
    return pl.pallas_call(
        functools.partial(_back_kernel, seq_len=seq_len, alpha=alpha),
        grid=(n_tok // tm,),
        in_specs=[
            pl.BlockSpec((tm, D_MODEL), lambda i: (i, 0)),
            pl.BlockSpec((tm, ATTN_WIDTH), lambda i: (i, 0)),
            pl.BlockSpec((tm, POOL_WIDTH), lambda i: (i, 0)),
            pl.BlockSpec((HALO, POOL_WIDTH), lambda i: (jnp.maximum(i * halo_per_tile - 1, 0), 0)),
            pl.BlockSpec((HALO, POOL_WIDTH),
                         lambda i: (jnp.minimum((i + 1) * halo_per_tile, n_halo_blocks - 1), 0)),
            pl.BlockSpec((None, N_MOD, D_MODEL), lambda i: (mod_row_of_token(i * tm), 0, 0)),
            _resident(w_g.shape),
            _resident(pool_w.shape),
            _resident((1, POOL_WIDTH)),
            _resident(w_up_attn.shape),
            _resident(w_up_pool.shape),
            _resident(w_out.shape),
            _resident((1, D_MODEL)),
            _resident((1, D_MODEL)),
            _resident(w1.shape),
            _resident(w2.shape),
            _resident((1, D_MODEL)),
            _resident((1, D_MODEL)),
        ],
        out_specs=pl.BlockSpec((tm, D_MODEL), lambda i: (i, 0)),
        out_shape=jax.ShapeDtypeStruct((n_tok, D_MODEL), F32),
        compiler_params=_compiler_params(1),
        name="back",
    )(x2d, attn, pin, pin, pin, mod, w_g, pool_w, pool_scale.reshape(1, POOL_WIDTH),
      w_up_attn, w_up_pool, w_out, row_vec(ln_g1), row_vec(ln_b1), w1, w2,
      row_vec(ln_g2), row_vec(ln_b2))


def _rope_tables(n_tokens):
    t = np.arange(n_tokens)
    n_freq = HEAD_DIM // 4
    inv_freq = ROPE_THETA ** (-np.arange(n_freq, dtype=np.float64) / n_freq)
    ang = np.concatenate([(t // GRID_W)[:, None] * inv_freq, (t % GRID_W)[:, None] * inv_freq],
                         axis=-1)
    cos, sin = np.cos(ang), np.sin(ang)
    return (jnp.asarray(np.concatenate([cos, cos], axis=-1), F32),
            jnp.asarray(np.concatenate([-sin, sin], axis=-1), F32))


def kernel(x_prompt, x_sample, cache_k, cache_v, c, c_ctx, w_mod, b_mod, ln_g, ln_b, ffn1_w1, ffn1_w2, w_in, q_norm_g, k_norm_g, pool_w, pool_scale, w_up_attn, w_up_pool, w_out, ffn2_w1, ffn2_w2):
    batch, seq, _ = x_prompt.shape
    dec_batch, n_lat, _ = x_sample.shape
    depth = w_mod.shape[0]
    past = cache_k.shape[2]
    alpha = (2.0 * depth) ** 0.25
    assert seq == SUB_TILE and (batch * seq) % TOKEN_TILE == 0
    assert n_lat % TOKEN_TILE == 0 and n_lat % Q_TILE == 0 and n_lat % KV_CHUNK == 0
    assert 1 + dec_batch <= MOD_ROWS

    cos2, sin2 = _rope_tables((n_lat // GRID_W) * GRID_W)
    cond = jnp.zeros((MOD_ROWS, D_MODEL), F32).at[0].set(c_ctx).at[1:1 + dec_batch].set(c)

    ctx_row = lambda t: 0
    lat_row = lambda t: 1 + t // n_lat

    yp = x_prompt.reshape(batch * seq, D_MODEL)
    ys = x_sample.reshape(dec_batch * n_lat, D_MODEL)
    new_k, new_v = [], []
    for l in range(depth):
        w1a = ffn1_w1[l].astype(BF16)
        w2a = ffn1_w2[l].astype(BF16)
        w1b = ffn2_w1[l].astype(BF16)
        w2b = ffn2_w2[l].astype(BF16)
        w_front = w_in[l, :, :FRONT_WIDTH].astype(BF16)
        w_g = w_in[l, :, FRONT_WIDTH:].astype(BF16)
        pool_w_b = pool_w[l].astype(BF16)
        w_ua = w_up_attn[l].astype(BF16)
        w_upl = w_up_pool[l].astype(BF16)
        w_o = w_out[l].astype(BF16)

        mod = _modulation_table(cond, w_mod[l], b_mod[l]).reshape(MOD_ROWS, N_MOD, D_MODEL)
        back = functools.partial(
            _back, w_g=w_g, pool_w=pool_w_b, pool_scale=pool_scale[l], w_up_attn=w_ua,
            w_up_pool=w_upl, w_out=w_o, ln_g1=ln_g[l, 1], ln_b1=ln_b[l, 1], w1=w1b, w2=w2b,
            ln_g2=ln_g[l, 2], ln_b2=ln_b[l, 2], alpha=alpha)

        x1, attn, k_l, v_l, pin = _front_ctx(
            yp, mod, ctx_row, w1a, w2a, ln_g[l, 0], ln_b[l, 0], w_front,
            q_norm_g[l], k_norm_g[l], alpha=alpha)
        yp = back(x1, attn, pin, mod, ctx_row, seq_len=seq)
        new_k.append(k_l.reshape(batch, seq, N_KV_HEADS, HEAD_DIM))
        new_v.append(v_l.reshape(batch, seq, N_KV_HEADS, HEAD_DIM))

        x1, q, kb, vb, pin = _front_lat(
            ys, mod, lat_row, w1a, w2a, ln_g[l, 0], ln_b[l, 0], w_front,
            q_norm_g[l], k_norm_g[l], cos2, sin2, n_lat=n_lat, alpha=alpha)
        attn = _attn_lat(q, kb, vb,
                         cache_k[:, l].reshape(dec_batch, past, KV_WIDTH),
                         cache_v[:, l].reshape(dec_batch, past, KV_WIDTH), n_lat)
        ys = back(x1, attn, pin, mod, lat_row, seq_len=n_lat)

    return (yp.reshape(batch, seq, D_MODEL),
            ys.reshape(dec_batch, n_lat, D_MODEL),
            jnp.stack(new_k, axis=1),
            jnp.stack(new_v, axis=1))
```

```python
import functools
import math

import numpy as np

import jax
import jax.numpy as jnp
from jax import lax
from jax.experimental import pallas as pl
from jax.experimental.pallas import tpu as pltpu

D_MODEL = 1024
GRID_W = 64
N_HEADS = 8
N_KV_HEADS = 2
HEAD_DIM = 128
GROUP = N_HEADS // N_KV_HEADS
ATTN_WIDTH = N_HEADS * HEAD_DIM
KV_WIDTH = N_KV_HEADS * HEAD_DIM
POOL_WINDOWS = (2, 4, 8, 16)
N_POOL_GROUPS = 4
POOL_GROUP_DIM = 128
POOL_WIDTH = N_POOL_GROUPS * POOL_GROUP_DIM
QKV_WIDTH = ATTN_WIDTH + 2 * KV_WIDTH
FRONT_WIDTH = QKV_WIDTH + POOL_WIDTH
D_FF = 2816
N_MOD = 9
ROPE_THETA = 10000.0
LN_EPS = 1e-6
RMS_EPS = 1e-6

SUBLANES = 8
LANES = 128
MXU_DIM = 256
VMEM_LIMIT_BYTES = 56 * 1024 * 1024

HALO = SUBLANES
MOD_ROWS = SUBLANES
TOKEN_TILE = 512
SUB_TILE = 256
Q_TILE = 512
Q_SUB = 256
KV_CHUNK = 256
FF_CHUNKS = ((0, 768), (768, 1024), (1792, 1024))
Q_SCALE = HEAD_DIM ** -0.5 * math.log2(math.e)
V_AUG = MXU_DIM

BF16 = jnp.bfloat16
F32 = jnp.float32


def _dot(a, b):
    return jnp.dot(a, b, preferred_element_type=F32)


def _dot_nt(a, b):
    return lax.dot_general(a, b, (((1,), (1,)), ((), ())), preferred_element_type=F32)


def _layer_norm(x, g, b):
    mu = jnp.mean(x, axis=-1, keepdims=True)
    xc = x - mu
    var = jnp.mean(xc * xc, axis=-1, keepdims=True)
    return xc * lax.rsqrt(var + LN_EPS) * g + b


def _modulate_bf16(x, shift, scale):
    return (x * (1.0 + scale) + shift).astype(BF16)


def _resident(shape):
    zeros = (0,) * len(shape)
    return pl.BlockSpec(shape, lambda *_: zeros, pipeline_mode=pl.Buffered(1))


def _compiler_params(n_axes):
    return pltpu.CompilerParams(
        dimension_semantics=("arbitrary",) * n_axes,
        vmem_limit_bytes=VMEM_LIMIT_BYTES,
    )


def _mod_kernel(cond_ref, w_ref, b_ref, o_ref):
    c = cond_ref[...]
    s = (c * jax.nn.sigmoid(c)).astype(BF16)
    o_ref[...] = _dot(s, w_ref[...].astype(BF16)) + b_ref[...]


def _modulation_table(cond, w_mod, b_mod):
    n_out = w_mod.shape[1]
    bn = 1152
    return pl.pallas_call(
        _mod_kernel,
        grid=(n_out // bn,),
        in_specs=[
            pl.BlockSpec((MOD_ROWS, D_MODEL), lambda j: (0, 0)),
            pl.BlockSpec((D_MODEL, bn), lambda j: (0, j)),
            pl.BlockSpec((1, bn), lambda j: (0, j)),
        ],
        out_specs=pl.BlockSpec((MOD_ROWS, bn), lambda j: (0, j)),
        out_shape=jax.ShapeDtypeStruct((MOD_ROWS, n_out), F32),
        compiler_params=_compiler_params(1),
        name="modulation",
    )(cond, w_mod, b_mod.reshape(1, n_out))


def _ffn_rows(xs, mod_ref, mod_base, w1_ref, w2_ref, ln_g, ln_b, alpha):
    shift = mod_ref[mod_base:mod_base + 1, :]
    scale = mod_ref[mod_base + 1:mod_base + 2, :]
    gate = mod_ref[mod_base + 2:mod_base + 3, :]
    hs = [_modulate_bf16(x, shift, scale) for x in xs]
    fs = [None] * len(xs)
    for start, width in FF_CHUNKS:
        us = []
        for h in hs:
            a = _dot(h, w1_ref[:, start:start + width])
            b = _dot(h, w1_ref[:, D_FF + start:D_FF + start + width])
            us.append((a * jax.nn.sigmoid(a) * b).astype(BF16))
        for i, u in enumerate(us):
            part = _dot(u, w2_ref[start:start + width, :])
            fs[i] = part if fs[i] is None else fs[i] + part
    return [_layer_norm(alpha * x + (0.5 * gate) * f, ln_g, ln_b) for x, f in zip(xs, fs)]


def _sub_tiles(n_rows):
    return [slice(r0, r0 + SUB_TILE) for r0 in range(0, n_rows, SUB_TILE)]


def _rms_head(xh, g):
    ms = jnp.mean(xh * xh, axis=-1, keepdims=True)
    return xh * lax.rsqrt(ms + RMS_EPS) * g


def _rope(xh, cos2, sin2):
    return xh * cos2 + pltpu.roll(xh, HEAD_DIM // 2, axis=1) * sin2


def _ones_column_block(rows):
    col = lax.broadcasted_iota(jnp.int32, (rows, V_AUG - HEAD_DIM), 1)
    return jnp.where(col == 0, 1.0, 0.0).astype(BF16)


def _stack_heads(heads):
    return jnp.concatenate(heads, axis=0)


def _head_cols(hh):
    return slice(hh * HEAD_DIM, (hh + 1) * HEAD_DIM)


def _front_ctx_kernel(x_ref, mod_ref, w1_ref, w2_ref, lng_ref, lnb_ref, win_ref, qg_ref, kg_ref,
                      x1_ref, a_ref, k_ref, v_ref, pin_ref, *, alpha):
    qg = qg_ref[...] * Q_SCALE
    kg = kg_ref[...]
    tiles = _sub_tiles(x_ref.shape[0])
    x1s = _ffn_rows([x_ref[rows, :] for rows in tiles], mod_ref, 0, w1_ref, w2_ref,
                    lng_ref[...], lnb_ref[...], alpha)
    gw = GROUP * HEAD_DIM
    scores = []
    for rows, x1 in zip(tiles, x1s):
        x1_ref[rows, :] = x1
        h = _modulate_bf16(x1, mod_ref[3:4, :], mod_ref[4:5, :])
        q_cols = [_dot(h, win_ref[:, :gw]), _dot(h, win_ref[:, gw:ATTN_WIDTH])]
        q4s = [_stack_heads([_rms_head(q_cols[0][:, _head_cols(g)], qg).astype(BF16)
                             for g in range(GROUP)])]
        kv_cols = _dot(h, win_ref[:, ATTN_WIDTH:QKV_WIDTH])
        q4s.append(_stack_heads([_rms_head(q_cols[1][:, _head_cols(g)], qg).astype(BF16)
                                 for g in range(GROUP)]))
        pin_ref[rows, :] = _dot(h, win_ref[:, QKV_WIDTH:])
        for kv in range(N_KV_HEADS):
            kn = _rms_head(kv_cols[:, _head_cols(kv)], kg)
            v_kv = kv_cols[:, KV_WIDTH + kv * HEAD_DIM:KV_WIDTH + (kv + 1) * HEAD_DIM]
            k_ref[rows, kv, :] = kn
            v_ref[rows, kv, :] = v_kv
            scores.append((rows, kv, _dot_nt(kn.astype(BF16), q4s[kv]), v_kv.T.astype(BF16)))
    probs = []
    for rows, kv, st, vt_kv in scores:
        p = jnp.exp2(st - jnp.max(st, axis=0, keepdims=True))
        probs.append((rows, kv, p.astype(BF16), jnp.sum(p, axis=0, keepdims=True), vt_kv))
    for rows, kv, p, l, vt_kv in probs:
        ot = _dot(vt_kv, p) / l
        for g in range(GROUP):
            a_ref[rows, _head_cols(kv * GROUP + g)] = (
                ot[:, g * SUB_TILE:(g + 1) * SUB_TILE].T.astype(BF16))


def _front_lat_kernel(x_ref, mod_ref, w1_ref, w2_ref, lng_ref, lnb_ref, win_ref, qg_ref, kg_ref,
                      cos_ref, sin_ref, x1_ref, q_ref, kb_ref, vb_ref, pin_ref, *, alpha):
    qg = qg_ref[...] * Q_SCALE
    kg = kg_ref[...]
    ones_col = _ones_column_block(SUB_TILE)
    tiles = _sub_tiles(x_ref.shape[0])
    x1s = _ffn_rows([x_ref[rows, :] for rows in tiles], mod_ref, 0, w1_ref, w2_ref,
                    lng_ref[...], lnb_ref[...], alpha)
    for rows, x1 in zip(tiles, x1s):
        x1_ref[rows, :] = x1
        proj = _dot(_modulate_bf16(x1, mod_ref[3:4, :], mod_ref[4:5, :]), win_ref[...])
        pin_ref[rows, :] = proj[:, QKV_WIDTH:]
        cos2 = cos_ref[rows, :]
        sin2 = sin_ref[rows, :]
        for hh in range(N_HEADS):
            q_ref[rows, _head_cols(hh)] = _rope(
                _rms_head(proj[:, _head_cols(hh)], qg), cos2, sin2).astype(BF16)
        for kv in range(N_KV_HEADS):
            kn = _rms_head(proj[:, ATTN_WIDTH + kv * HEAD_DIM:ATTN_WIDTH + (kv + 1) * HEAD_DIM], kg)
            kb_ref[rows, _head_cols(kv)] = _rope(kn, cos2, sin2).astype(BF16)
            v0 = ATTN_WIDTH + KV_WIDTH + kv * HEAD_DIM
            vb_ref[rows, kv * V_AUG:kv * V_AUG + HEAD_DIM] = proj[:, v0:v0 + HEAD_DIM].astype(BF16)
            vb_ref[rows, kv * V_AUG + HEAD_DIM:(kv + 1) * V_AUG] = ones_col


def _front_in_specs(tm, mod_row_of_token, w1, w2, w_front):
    return [
        pl.BlockSpec((tm, D_MODEL), lambda i: (i, 0)),
        pl.BlockSpec((None, N_MOD, D_MODEL), lambda i: (mod_row_of_token(i * tm), 0, 0)),
        _resident(w1.shape),
        _resident(w2.shape),
        _resident((1, D_MODEL)),
        _resident((1, D_MODEL)),
        _resident(w_front.shape),
        _resident((1, HEAD_DIM)),
        _resident((1, HEAD_DIM)),
    ]


def _front_ctx(x2d, mod, mod_row_of_token, w1, w2, ln_g, ln_b, w_front, q_g, k_g, *, alpha):
    n_tok = x2d.shape[0]
    tm = TOKEN_TILE
    tok = lambda width: pl.BlockSpec((tm, width), lambda i: (i, 0))
    cache_spec = pl.BlockSpec((tm, N_KV_HEADS, HEAD_DIM), lambda i: (i, 0, 0))
    cache_shape = jax.ShapeDtypeStruct((n_tok, N_KV_HEADS, HEAD_DIM), F32)
    return pl.pallas_call(
        functools.partial(_front_ctx_kernel, alpha=alpha),
        grid=(n_tok // tm,),
        in_specs=_front_in_specs(tm, mod_row_of_token, w1, w2, w_front),
        out_specs=[tok(D_MODEL), tok(ATTN_WIDTH), cache_spec, cache_spec, tok(POOL_WIDTH)],
        out_shape=[
            jax.ShapeDtypeStruct((n_tok, D_MODEL), F32),
            jax.ShapeDtypeStruct((n_tok, ATTN_WIDTH), BF16),
            cache_shape,
            cache_shape,
            jax.ShapeDtypeStruct((n_tok, POOL_WIDTH), F32),
        ],
        compiler_params=_compiler_params(1),
        name="front_ctx",
    )(x2d, mod, w1, w2, ln_g.reshape(1, D_MODEL), ln_b.reshape(1, D_MODEL), w_front,
      q_g.reshape(1, HEAD_DIM), k_g.reshape(1, HEAD_DIM))


def _front_lat(x2d, mod, mod_row_of_token, w1, w2, ln_g, ln_b, w_front, q_g, k_g, cos2, sin2,
               *, n_lat, alpha):
    n_tok = x2d.shape[0]
    tm = TOKEN_TILE
    tiles_per_seq = n_lat // tm
    tok = lambda width: pl.BlockSpec((tm, width), lambda i: (i, 0))
    rope_spec = pl.BlockSpec((tm, HEAD_DIM), lambda i: (i % tiles_per_seq, 0))
    return pl.pallas_call(
        functools.partial(_front_lat_kernel, alpha=alpha),
        grid=(n_tok // tm,),
        in_specs=_front_in_specs(tm, mod_row_of_token, w1, w2, w_front) + [rope_spec, rope_spec],
        out_specs=[tok(D_MODEL), tok(ATTN_WIDTH), tok(KV_WIDTH), tok(N_KV_HEADS * V_AUG),
                   tok(POOL_WIDTH)],
        out_shape=[
            jax.ShapeDtypeStruct((n_tok, D_MODEL), F32),
            jax.ShapeDtypeStruct((n_tok, ATTN_WIDTH), BF16),
            jax.ShapeDtypeStruct((n_tok, KV_WIDTH), BF16),
            jax.ShapeDtypeStruct((n_tok, N_KV_HEADS * V_AUG), BF16),
            jax.ShapeDtypeStruct((n_tok, POOL_WIDTH), F32),
        ],
        compiler_params=_compiler_params(1),
        name="front_lat",
    )(x2d, mod, w1, w2, ln_g.reshape(1, D_MODEL), ln_b.reshape(1, D_MODEL), w_front,
      q_g.reshape(1, HEAD_DIM), k_g.reshape(1, HEAD_DIM), cos2, sin2)


def _attn_lat_kernel(q_ref, kc_ref, vc_ref, kl_ref, vl_ref, o_ref):
    n_lat = kl_ref.shape[0]
    past = kc_ref.shape[0]
    vc = jnp.concatenate([vc_ref[...].astype(BF16), _ones_column_block(past)], axis=1)
    chunks = [(kc_ref[...].astype(BF16), vc)]
    for c0 in range(0, n_lat, KV_CHUNK):
        chunks.append((kl_ref[c0:c0 + KV_CHUNK, :], vl_ref[c0:c0 + KV_CHUNK, :]))
    for r0 in range(0, q_ref.shape[0], Q_SUB):
        rows = slice(r0, r0 + Q_SUB)
        q4 = _stack_heads([q_ref[rows, _head_cols(g)] for g in range(GROUP)])
        m = None
        acc = None
        for k_c, v_c in chunks:
            s = _dot_nt(q4, k_c)
            m_c = jnp.max(s, axis=-1, keepdims=True)
            if m is None:
                m = m_c
                acc = _dot(jnp.exp2(s - m).astype(BF16), v_c)
            else:
                m_new = jnp.maximum(m, m_c)
                acc = jnp.exp2(m - m_new) * acc + _dot(jnp.exp2(s - m_new).astype(BF16), v_c)
                m = m_new
        o = acc[:, :HEAD_DIM] / acc[:, HEAD_DIM:HEAD_DIM + 1]
        for g in range(GROUP):
            o_ref[rows, _head_cols(g)] = o[g * Q_SUB:(g + 1) * Q_SUB].astype(BF16)


def _attn_lat(q, kb, vb, cache_k, cache_v, n_lat):
    n_tok = q.shape[0]
    n_batch = n_tok // n_lat
    past = cache_k.shape[1]
    tq = Q_TILE
    q_tiles = n_lat // tq
    gw = GROUP * HEAD_DIM
    return pl.pallas_call(
        _attn_lat_kernel,
        grid=(n_batch, N_KV_HEADS, q_tiles),
        in_specs=[
            pl.BlockSpec((tq, gw), lambda b, kv, i: (b * q_tiles + i, kv)),
            pl.BlockSpec((None, past, HEAD_DIM), lambda b, kv, i: (b, 0, kv)),
            pl.BlockSpec((None, past, HEAD_DIM), lambda b, kv, i: (b, 0, kv)),
            pl.BlockSpec((n_lat, HEAD_DIM), lambda b, kv, i: (b, kv)),
            pl.BlockSpec((n_lat, V_AUG), lambda b, kv, i: (b, kv)),
        ],
        out_specs=pl.BlockSpec((tq, gw), lambda b, kv, i: (b * q_tiles + i, kv)),
        out_shape=jax.ShapeDtypeStruct((n_tok, ATTN_WIDTH), BF16),
        compiler_params=_compiler_params(3),
        name="attn_lat",
    )(q, cache_k, cache_v, kb, vb)


def _window_sums(e):
    n = e.shape[0]
    back = lambda a, k: pltpu.roll(a, k, axis=0)
    fwd = lambda a, k: pltpu.roll(a, n - k, axis=0)
    c2 = e + back(e, 1)
    c4 = back(c2, 1) + fwd(c2, 1)
    c8 = back(c4, 2) + fwd(c4, 2)
    c16 = back(c8, 4) + fwd(c8, 4)
    return c2, c4, c8, c16


def _back_kernel(x_ref, a_ref, pin_ref, pp_ref, pn_ref, mod_ref, wg_ref, poolw_ref, pscale_ref,
                 wua_ref, wup_ref, wo_ref, lng1_ref, lnb1_ref, w1_ref, w2_ref, lng2_ref, lnb2_ref,
                 o_ref, *, seq_len, alpha):
    tm = x_ref.shape[0]
    tile_start = pl.program_id(0) * tm
    shift = mod_ref[3:4, :]
    scale = mod_ref[4:5, :]
    gate = mod_ref[5:6, :]
    p_ext = jnp.concatenate([pp_ref[...], pin_ref[...], pn_ref[...]], axis=0)

    tiles = _sub_tiles(tm)
    xs = [x_ref[rows, :] for rows in tiles]

    gated = []
    pooled_up = []
    half = D_MODEL // 2
    for rows, x in zip(tiles, xs):
        h = _modulate_bf16(x, shift, scale)
        a_blk = a_ref[rows, :]
        big_operands = [
            (h, wg_ref, slice(0, D_MODEL)),
            (h, wg_ref, slice(D_MODEL, 2 * D_MODEL)),
            (a_blk, wua_ref, slice(0, half)),
            (a_blk, wua_ref, slice(half, D_MODEL)),
        ]
        big = []

        pos = (tile_start + rows.start) % seq_len
        e_all = p_ext[rows.start:rows.start + SUB_TILE + 2 * HALO, :]
        row = lax.broadcasted_iota(jnp.int32, (SUB_TILE + 2 * HALO, 1), 0)
        first_row = jnp.where(pos == 0, HALO, 0)
        end_row = jnp.where(pos + SUB_TILE == seq_len, HALO + SUB_TILE, SUB_TILE + 2 * HALO)
        e_all = jnp.where(jnp.logical_or(row < first_row, row >= end_row), 0.0, e_all)
        t = pos + lax.broadcasted_iota(jnp.int32, (SUB_TILE, 1), 0)
        pgs = []
        for g, w in enumerate(POOL_WINDOWS):
            lhs, w_ref, cols = big_operands[g]
            big.append(_dot(lhs, w_ref[:, cols]))
            sl = slice(g * POOL_GROUP_DIM, (g + 1) * POOL_GROUP_DIM)
            e = e_all[:, sl]
            win = _window_sums(e)[g][HALO:HALO + SUB_TILE]
            lo = jnp.maximum(t - w // 2, 0)
            hi = jnp.minimum(t + w // 2, seq_len)
            inv_cnt = 1.0 / (hi - lo).astype(F32)
            pooled = (win * inv_cnt - e[HALO:HALO + SUB_TILE]).astype(BF16)
            pgs.append(_dot(pooled, poolw_ref[g]))

        ga, gp, au_lo, au_hi = big
        pg = jnp.concatenate(pgs, axis=1)
        gated.append((jax.nn.sigmoid(ga) * jnp.concatenate([au_lo, au_hi], axis=1), gp))
        pooled_up.append(_dot((pg * pscale_ref[...]).astype(BF16), wup_ref[...]))

    merged = [(ga_up + jax.nn.sigmoid(gp) * pu).astype(BF16)
              for (ga_up, gp), pu in zip(gated, pooled_up)]
    mixes = [_dot(m, wo_ref[...]) for m in merged]
    x2s = [_layer_norm(alpha * x + gate * mix, lng1_ref[...], lnb1_ref[...])
           for x, mix in zip(xs, mixes)]

    ys = _ffn_rows(x2s, mod_ref, 6, w1_ref, w2_ref, lng2_ref[...], lnb2_ref[...], alpha)
    for rows, y in zip(tiles, ys):
        o_ref[rows, :] = y


def _back(x2d, attn, pin, mod, mod_row_of_token, w_g, pool_w, pool_scale, w_up_attn, w_up_pool,
          w_out, ln_g1, ln_b1, w1, w2, ln_g2, ln_b2, *, seq_len, alpha):
    n_tok = x2d.shape[0]
    tm = TOKEN_TILE
    halo_per_tile = tm // HALO
    n_halo_blocks = n_tok // HALO
    row_vec = lambda v: v.reshape(1, D_MODEL)
    return pl.pallas_call(
        functools.partial(_back_kernel, seq_len=seq_len, alpha=alpha),
        grid=(n_tok // tm,),
        in_specs=[
            pl.BlockSpec((tm, D_MODEL), lambda i: (i, 0)),
            pl.BlockSpec((tm, ATTN_WIDTH), lambda i: (i, 0)),
            pl.BlockSpec((tm, POOL_WIDTH), lambda i: (i, 0)),
            pl.BlockSpec((HALO, POOL_WIDTH), lambda i: (jnp.maximum(i * halo_per_tile - 1, 0), 0)),
            pl.BlockSpec((HALO, POOL_WIDTH),
                         lambda i: (jnp.minimum((i + 1) * halo_per_tile, n_halo_blocks - 1), 0)),
            pl.BlockSpec((None, N_MOD, D_MODEL), lambda i: (mod_row_of_token(i * tm), 0, 0)),
            _resident(w_g.shape),
            _resident(pool_w.shape),
            _resident((1, POOL_WIDTH)),
            _resident(w_up_attn.shape),
            _resident(w_up_pool.shape),
            _resident(w_out.shape),
            _resident((1, D_MODEL)),
            _resident((1, D_MODEL)),
            _resident(w1.shape),
            _resident(w2.shape),
            _resident((1, D_MODEL)),
            _resident((1, D_MODEL)),
        ],
        out_specs=pl.BlockSpec((tm, D_MODEL), lambda i: (i, 0)),
        out_shape=jax.ShapeDtypeStruct((n_tok, D_MODEL), F32),
        compiler_params=_compiler_params(1),
        name="back",
    )(x2d, attn, pin, pin, pin, mod, w_g, pool_w, pool_scale.reshape(1, POOL_WIDTH),
      w_up_attn, w_up_pool, w_out, row_vec(ln_g1), row_vec(ln_b1), w1, w2,
      row_vec(ln_g2), row_vec(ln_b2))


def _rope_tables(n_tokens):
    t = np.arange(n_tokens)
    n_freq = HEAD_DIM // 4
    inv_freq = ROPE_THETA ** (-np.arange(n_freq, dtype=np.float64) / n_freq)
    ang = np.concatenate([(t // GRID_W)[:, None] * inv_freq, (t % GRID_W)[:, None] * inv_freq],
                         axis=-1)
    cos, sin = np.cos(ang), np.sin(ang)
    return (jnp.asarray(np.concatenate([cos, cos], axis=-1), F32),
            jnp.asarray(np.concatenate([-sin, sin], axis=-1), F32))


def kernel(x_prompt, x_sample, cache_k, cache_v, c, c_ctx, w_mod, b_mod, ln_g, ln_b, ffn1_w1, ffn1_w2, w_in, q_norm_g, k_norm_g, pool_w, pool_scale, w_up_attn, w_up_pool, w_out, ffn2_w1, ffn2_w2):
    batch, seq, _ = x_prompt.shape
    dec_batch, n_lat, _ = x_sample.shape
    depth = w_mod.shape[0]
    past = cache_k.shape[2]
    alpha = (2.0 * depth) ** 0.25
    assert seq == SUB_TILE and (batch * seq) % TOKEN_TILE == 0
    assert n_lat % TOKEN_TILE == 0 and n_lat % Q_TILE == 0 and n_lat % KV_CHUNK == 0
    assert 1 + dec_batch <= MOD_ROWS

    cos2, sin2 = _rope_tables((n_lat // GRID_W) * GRID_W)
    cond = jnp.zeros((MOD_ROWS, D_MODEL), F32).at[0].set(c_ctx).at[1:1 + dec_batch].set(c)

    ctx_row = lambda t: 0
    lat_row = lambda t: 1 + t // n_lat

    yp = x_prompt.reshape(batch * seq, D_MODEL)
    ys = x_sample.reshape(dec_batch * n_lat, D_MODEL)
    new_k, new_v = [], []
    for l in range(depth):
        w1a = ffn1_w1[l].astype(BF16)
        w2a = ffn1_w2[l].astype(BF16)
        w1b = ffn2_w1[l].astype(BF16)
        w2b = ffn2_w2[l].astype(BF16)
        w_front = w_in[l, :, :FRONT_WIDTH].astype(BF16)
        w_g = w_in[l, :, FRONT_WIDTH:].astype(BF16)
        pool_w_b = pool_w[l].astype(BF16)
        w_ua = w_up_attn[l].astype(BF16)
        w_upl = w_up_pool[l].astype(BF16)
        w_o = w_out[l].astype(BF16)

        mod = _modulation_table(cond, w_mod[l], b_mod[l]).reshape(MOD_ROWS, N_MOD, D_MODEL)
        back = functools.partial(
            _back, w_g=w_g, pool_w=pool_w_b, pool_scale=pool_scale[l], w_up_attn=w_ua,
            w_up_pool=w_upl, w_out=w_o, ln_g1=ln_g[l, 1], ln_b1=ln_b[l, 1], w1=w1b, w2=w2b,
            ln_g2=ln_g[l, 2], ln_b2=ln_b[l, 2], alpha=alpha)

        x1, attn, k_l, v_l, pin = _front_ctx(
            yp, mod, ctx_row, w1a, w2a, ln_g[l, 0], ln_b[l, 0], w_front,
            q_norm_g[l], k_norm_g[l], alpha=alpha)
        yp = back(x1, attn, pin, mod, ctx_row, seq_len=seq)
        new_k.append(k_l.reshape(batch, seq, N_KV_HEADS, HEAD_DIM))
        new_v.append(v_l.reshape(batch, seq, N_KV_HEADS, HEAD_DIM))

        x1, q, kb, vb, pin = _front_lat(
            ys, mod, lat_row, w1a, w2a, ln_g[l, 0], ln_b[l, 0], w_front,
            q_norm_g[l], k_norm_g[l], cos2, sin2, n_lat=n_lat, alpha=alpha)
        attn = _attn_lat(q, kb, vb,
                         cache_k[:, l].reshape(dec_batch, past, KV_WIDTH),
                         cache_v[:, l].reshape(dec_batch, past, KV_WIDTH), n_lat)
        ys = back(x1, attn, pin, mod, lat_row, seq_len=n_lat)

    return (yp.reshape(batch, seq, D_MODEL),
            ys.reshape(dec_batch, n_lat, D_MODEL),
            jnp.stack(new_k, axis=1),
            jnp.stack(new_v, axis=1))
```

```python
import functools
import math

import numpy as np

import jax
import jax.numpy as jnp
from jax import lax
from jax.experimental import pallas as pl
from jax.experimental.pallas import tpu as pltpu

D_MODEL = 1024
GRID_W = 64
N_HEADS = 8
N_KV_HEADS = 2
HEAD_DIM = 128
GROUP = N_HEADS // N_KV_HEADS
ATTN_WIDTH = N_HEADS * HEAD_DIM
KV_WIDTH = N_KV_HEADS * HEAD_DIM
POOL_WINDOWS = (2, 4, 8, 16)
N_POOL_GROUPS = 4
POOL_GROUP_DIM = 128
POOL_WIDTH = N_POOL_GROUPS * POOL_GROUP_DIM
QKV_WIDTH = ATTN_WIDTH + 2 * KV_WIDTH
FRONT_WIDTH = QKV_WIDTH + POOL_WIDTH
D_FF = 2816
N_MOD = 9
ROPE_THETA = 10000.0
LN_EPS = 1e-6
RMS_EPS = 1e-6

SUBLANES = 8
LANES = 128
MXU_DIM = 256
VMEM_LIMIT_BYTES = 63 * 1024 * 1024

HALO = SUBLANES
MOD_ROWS = SUBLANES
MOD_BLOCK = 2304
TOKEN_TILE = 512
BACK_TILE = 1024
SUB_TILE = 256
Q_TILE = 512
Q_SUB = 256
KV_CHUNK = 256
FF_CHUNKS = ((0, 768), (768, 1024), (1792, 1024))
Q_SCALE = HEAD_DIM ** -0.5 * math.log2(math.e)
V_AUG = MXU_DIM

BF16 = jnp.bfloat16
F32 = jnp.float32


def _dot(a, b):
    return jnp.dot(a, b, preferred_element_type=F32)


def _dot_nt(a, b):
    return lax.dot_general(a, b, (((1,), (1,)), ((), ())), preferred_element_type=F32)


def _layer_norm(x, g, b):
    mu = jnp.mean(x, axis=-1, keepdims=True)
    xc = x - mu
    var = jnp.mean(xc * xc, axis=-1, keepdims=True)
    return xc * lax.rsqrt(var + LN_EPS) * g + b


def _modulate_bf16(x, shift, scale):
    return (x * (1.0 + scale) + shift).astype(BF16)


def _resident(shape):
    zeros = (0,) * len(shape)
    return pl.BlockSpec(shape, lambda *_: zeros, pipeline_mode=pl.Buffered(1))


def _compiler_params(n_axes):
    return pltpu.CompilerParams(
        dimension_semantics=("arbitrary",) * n_axes,
        vmem_limit_bytes=VMEM_LIMIT_BYTES,
    )


def _mod_kernel(cond_ref, w_ref, b_ref, o_ref):
    c = cond_ref[...]
    s = (c * jax.nn.sigmoid(c)).astype(BF16)
    o_ref[...] = _dot(s, w_ref[...].astype(BF16)) + b_ref[...]


def _modulation_table(cond, w_mod, b_mod):
    n_out = w_mod.shape[1]
    bn = MOD_BLOCK
    return pl.pallas_call(
        _mod_kernel,
        grid=(n_out // bn,),
        in_specs=[
            pl.BlockSpec((MOD_ROWS, D_MODEL), lambda j: (0, 0)),
            pl.BlockSpec((D_MODEL, bn), lambda j: (0, j)),
            pl.BlockSpec((1, bn), lambda j: (0, j)),
        ],
        out_specs=pl.BlockSpec((MOD_ROWS, bn), lambda j: (0, j)),
        out_shape=jax.ShapeDtypeStruct((MOD_ROWS, n_out), F32),
        compiler_params=_compiler_params(1),
        name="modulation",
    )(cond, w_mod, b_mod.reshape(1, n_out))


def _ffn_rows(xs, mod_ref, mod_base, w1_ref, w2_ref, ln_g, ln_b, alpha):
    shift = mod_ref[mod_base:mod_base + 1, :]
    scale = mod_ref[mod_base + 1:mod_base + 2, :]
    gate = mod_ref[mod_base + 2:mod_base + 3, :]
    hs = [_modulate_bf16(x, shift, scale) for x in xs]
    fs = [None] * len(xs)
    for start, width in FF_CHUNKS:
        us = []
        for h in hs:
            a = _dot(h, w1_ref[:, start:start + width])
            b = _dot(h, w1_ref[:, D_FF + start:D_FF + start + width])
            us.append((a * jax.nn.sigmoid(a) * b).astype(BF16))
        for i, u in enumerate(us):
            part = _dot(u, w2_ref[start:start + width, :])
            fs[i] = part if fs[i] is None else fs[i] + part
    return [_layer_norm(alpha * x + (0.5 * gate) * f, ln_g, ln_b) for x, f in zip(xs, fs)]


def _sub_tiles(n_rows):
    return [slice(r0, r0 + SUB_TILE) for r0 in range(0, n_rows, SUB_TILE)]


def _rms_head(xh, g):
    ms = jnp.mean(xh * xh, axis=-1, keepdims=True)
    return xh * lax.rsqrt(ms + RMS_EPS) * g


def _rope(xh, cos2, sin2):
    return xh * cos2 + pltpu.roll(xh, HEAD_DIM // 2, axis=1) * sin2


def _ones_column_block(rows):
    col = lax.broadcasted_iota(jnp.int32, (rows, V_AUG - HEAD_DIM), 1)
    return jnp.where(col == 0, 1.0, 0.0).astype(BF16)


def _stack_heads(heads):
    return jnp.concatenate(heads, axis=0)


def _head_cols(hh):
    return slice(hh * HEAD_DIM, (hh + 1) * HEAD_DIM)


def _front_ctx_kernel(x_ref, mod_ref, w1_ref, w2_ref, lng_ref, lnb_ref, win_ref, qg_ref, kg_ref,
                      x1_ref, a_ref, k_ref, v_ref, pin_ref, *, alpha):
    qg = qg_ref[...] * Q_SCALE
    kg = kg_ref[...]
    tiles = _sub_tiles(x_ref.shape[0])
    x1s = _ffn_rows([x_ref[rows, :] for rows in tiles], mod_ref, 0, w1_ref, w2_ref,
                    lng_ref[...], lnb_ref[...], alpha)
    gw = GROUP * HEAD_DIM
    scores = []
    for rows, x1 in zip(tiles, x1s):
        x1_ref[rows, :] = x1
        h = _modulate_bf16(x1, mod_ref[3:4, :], mod_ref[4:5, :])
        q_cols = [_dot(h, win_ref[:, :gw]), _dot(h, win_ref[:, gw:ATTN_WIDTH])]
        q4s = [_stack_heads([_rms_head(q_cols[0][:, _head_cols(g)], qg).astype(BF16)
                             for g in range(GROUP)])]
        kv_cols = _dot(h, win_ref[:, ATTN_WIDTH:QKV_WIDTH])
        q4s.append(_stack_heads([_rms_head(q_cols[1][:, _head_cols(g)], qg).astype(BF16)
                                 for g in range(GROUP)]))
        pin_ref[rows, :] = _dot(h, win_ref[:, QKV_WIDTH:])
        for kv in range(N_KV_HEADS):
            kn = _rms_head(kv_cols[:, _head_cols(kv)], kg)
            v_kv = kv_cols[:, KV_WIDTH + kv * HEAD_DIM:KV_WIDTH + (kv + 1) * HEAD_DIM]
            k_ref[rows, kv, :] = kn
            v_ref[rows, kv, :] = v_kv
            scores.append((rows, kv, _dot_nt(kn.astype(BF16), q4s[kv]), v_kv.T.astype(BF16)))
    probs = []
    for rows, kv, st, vt_kv in scores:
        p = jnp.exp2(st - jnp.max(st, axis=0, keepdims=True))
        probs.append((rows, kv, p.astype(BF16), jnp.sum(p, axis=0, keepdims=True), vt_kv))
    for rows, kv, p, l, vt_kv in probs:
        ot = _dot(vt_kv, p) / l
        for g in range(GROUP):
            a_ref[rows, _head_cols(kv * GROUP + g)] = (
                ot[:, g * SUB_TILE:(g + 1) * SUB_TILE].T.astype(BF16))


def _front_lat_kernel(x_ref, mod_ref, w1_ref, w2_ref, lng_ref, lnb_ref, win_ref, qg_ref, kg_ref,
                      cos_ref, sin_ref, x1_ref, q_ref, kb_ref, vb_ref, pin_ref, *, alpha):
    qg = qg_ref[...] * Q_SCALE
    kg = kg_ref[...]
    ones_col = _ones_column_block(SUB_TILE)
    tiles = _sub_tiles(x_ref.shape[0])
    x1s = _ffn_rows([x_ref[rows, :] for rows in tiles], mod_ref, 0, w1_ref, w2_ref,
                    lng_ref[...], lnb_ref[...], alpha)
    for rows, x1 in zip(tiles, x1s):
        x1_ref[rows, :] = x1
        proj = _dot(_modulate_bf16(x1, mod_ref[3:4, :], mod_ref[4:5, :]), win_ref[...])
        pin_ref[rows, :] = proj[:, QKV_WIDTH:]
        cos2 = cos_ref[rows, :]
        sin2 = sin_ref[rows, :]
        for hh in range(N_HEADS):
            q_ref[rows, _head_cols(hh)] = _rope(
                _rms_head(proj[:, _head_cols(hh)], qg), cos2, sin2).astype(BF16)
        for kv in range(N_KV_HEADS):
            kn = _rms_head(proj[:, ATTN_WIDTH + kv * HEAD_DIM:ATTN_WIDTH + (kv + 1) * HEAD_DIM], kg)
            kb_ref[rows, _head_cols(kv)] = _rope(kn, cos2, sin2).astype(BF16)
            v0 = ATTN_WIDTH + KV_WIDTH + kv * HEAD_DIM
            vb_ref[rows, kv * V_AUG:kv * V_AUG + HEAD_DIM] = proj[:, v0:v0 + HEAD_DIM].astype(BF16)
            vb_ref[rows, kv * V_AUG + HEAD_DIM:(kv + 1) * V_AUG] = ones_col


def _front_in_specs(tm, mod_row_of_token, w1, w2, w_front):
    return [
        pl.BlockSpec((tm, D_MODEL), lambda i: (i, 0)),
        pl.BlockSpec((None, N_MOD, D_MODEL), lambda i: (mod_row_of_token(i * tm), 0, 0)),
        _resident(w1.shape),
        _resident(w2.shape),
        _resident((1, D_MODEL)),
        _resident((1, D_MODEL)),
        _resident(w_front.shape),
        _resident((1, HEAD_DIM)),
        _resident((1, HEAD_DIM)),
    ]


def _front_ctx(x2d, mod, mod_row_of_token, w1, w2, ln_g, ln_b, w_front, q_g, k_g, *, alpha):
    n_tok = x2d.shape[0]
    tm = TOKEN_TILE
    tok = lambda width: pl.BlockSpec((tm, width), lambda i: (i, 0))
    cache_spec = pl.BlockSpec((tm, N_KV_HEADS, HEAD_DIM), lambda i: (i, 0, 0))
    cache_shape = jax.ShapeDtypeStruct((n_tok, N_KV_HEADS, HEAD_DIM), F32)
    return pl.pallas_call(
        functools.partial(_front_ctx_kernel, alpha=alpha),
        grid=(n_tok // tm,),
        in_specs=_front_in_specs(tm, mod_row_of_token, w1, w2, w_front),
        out_specs=[tok(D_MODEL), tok(ATTN_WIDTH), cache_spec, cache_spec, tok(POOL_WIDTH)],
        out_shape=[
            jax.ShapeDtypeStruct((n_tok, D_MODEL), F32),
            jax.ShapeDtypeStruct((n_tok, ATTN_WIDTH), BF16),
            cache_shape,
            cache_shape,
            jax.ShapeDtypeStruct((n_tok, POOL_WIDTH), F32),
        ],
        compiler_params=_compiler_params(1),
        name="front_ctx",
    )(x2d, mod, w1, w2, ln_g.reshape(1, D_MODEL), ln_b.reshape(1, D_MODEL), w_front,
      q_g.reshape(1, HEAD_DIM), k_g.reshape(1, HEAD_DIM))


def _front_lat(x2d, mod, mod_row_of_token, w1, w2, ln_g, ln_b, w_front, q_g, k_g, cos2, sin2,
               *, n_lat, alpha):
    n_tok = x2d.shape[0]
    tm = TOKEN_TILE
    tiles_per_seq = n_lat // tm
    tok = lambda width: pl.BlockSpec((tm, width), lambda i: (i, 0))
    rope_spec = pl.BlockSpec((tm, HEAD_DIM), lambda i: (i % tiles_per_seq, 0))
    return pl.pallas_call(
        functools.partial(_front_lat_kernel, alpha=alpha),
        grid=(n_tok // tm,),
        in_specs=_front_in_specs(tm, mod_row_of_token, w1, w2, w_front) + [rope_spec, rope_spec],
        out_specs=[tok(D_MODEL), tok(ATTN_WIDTH), tok(KV_WIDTH), tok(N_KV_HEADS * V_AUG),
                   tok(POOL_WIDTH)],
        out_shape=[
            jax.ShapeDtypeStruct((n_tok, D_MODEL), F32),
            jax.ShapeDtypeStruct((n_tok, ATTN_WIDTH), BF16),
            jax.ShapeDtypeStruct((n_tok, KV_WIDTH), BF16),
            jax.ShapeDtypeStruct((n_tok, N_KV_HEADS * V_AUG), BF16),
            jax.ShapeDtypeStruct((n_tok, POOL_WIDTH), F32),
        ],
        compiler_params=_compiler_params(1),
        name="front_lat",
    )(x2d, mod, w1, w2, ln_g.reshape(1, D_MODEL), ln_b.reshape(1, D_MODEL), w_front,
      q_g.reshape(1, HEAD_DIM), k_g.reshape(1, HEAD_DIM), cos2, sin2)


def _attn_lat_kernel(q_ref, kc_ref, vc_ref, kl_ref, vl_ref, o_ref):
    n_lat = kl_ref.shape[0]
    past = kc_ref.shape[0]
    vc = jnp.concatenate([vc_ref[...].astype(BF16), _ones_column_block(past)], axis=1)
    chunks = [(kc_ref[...].astype(BF16), vc)]
    for c0 in range(0, n_lat, KV_CHUNK):
        chunks.append((kl_ref[c0:c0 + KV_CHUNK, :], vl_ref[c0:c0 + KV_CHUNK, :]))
    for r0 in range(0, q_ref.shape[0], Q_SUB):
        rows = slice(r0, r0 + Q_SUB)
        q4 = _stack_heads([q_ref[rows, _head_cols(g)] for g in range(GROUP)])
        m = None
        acc = None
        for k_c, v_c in chunks:
            s = _dot_nt(q4, k_c)
            m_c = jnp.max(s, axis=-1, keepdims=True)
            if m is None:
                m = m_c
                acc = _dot(jnp.exp2(s - m).astype(BF16), v_c)
            else:
                m_new = jnp.maximum(m, m_c)
                acc = jnp.exp2(m - m_new) * acc + _dot(jnp.exp2(s - m_new).astype(BF16), v_c)
                m = m_new
        o = acc[:, :HEAD_DIM] / acc[:, HEAD_DIM:HEAD_DIM + 1]
        for g in range(GROUP):
            o_ref[rows, _head_cols(g)] = o[g * Q_SUB:(g + 1) * Q_SUB].astype(BF16)


def _attn_lat(q, kb, vb, cache_k, cache_v, n_lat):
    n_tok = q.shape[0]
    n_batch = n_tok // n_lat
    past = cache_k.shape[1]
    tq = Q_TILE
    q_tiles = n_lat // tq
    gw = GROUP * HEAD_DIM
    cache_spec = pl.BlockSpec((None, past, HEAD_DIM), lambda b, kv, i: (b, 0, kv))
    return pl.pallas_call(
        _attn_lat_kernel,
        grid=(n_batch, N_KV_HEADS, q_tiles),
        in_specs=[
            pl.BlockSpec((tq, gw), lambda b, kv, i: (b * q_tiles + i, kv)),
            cache_spec,
            cache_spec,
            pl.BlockSpec((n_lat, HEAD_DIM), lambda b, kv, i: (b, kv)),
            pl.BlockSpec((n_lat, V_AUG), lambda b, kv, i: (b, kv)),
        ],
        out_specs=pl.BlockSpec((tq, gw), lambda b, kv, i: (b * q_tiles + i, kv)),
        out_shape=jax.ShapeDtypeStruct((n_tok, ATTN_WIDTH), BF16),
        compiler_params=_compiler_params(3),
        name="attn_lat",
    )(q, cache_k, cache_v, kb, vb)


def _window_sums(e):
    n = e.shape[0]
    back = lambda a, k: pltpu.roll(a, k, axis=0)
    fwd = lambda a, k: pltpu.roll(a, n - k, axis=0)
    c2 = e + back(e, 1)
    c4 = back(c2, 1) + fwd(c2, 1)
    c8 = back(c4, 2) + fwd(c4, 2)
    c16 = back(c8, 4) + fwd(c8, 4)
    return c2, c4, c8, c16


def _back_kernel(x_ref, a_ref, pin_ref, pp_ref, pn_ref, mod_ref, wg_ref, poolw_ref, pscale_ref,
                 wua_ref, wup_ref, wo_ref, lng1_ref, lnb1_ref, w1_ref, w2_ref, lng2_ref, lnb2_ref,
                 o_ref, *, seq_len, alpha):
    tm = x_ref.shape[0]
    tile_start = pl.program_id(0) * tm
    shift = mod_ref[3:4, :]
    scale = mod_ref[4:5, :]
    gate = mod_ref[5:6, :]
    p_ext = jnp.concatenate([pp_ref[...], pin_ref[...], pn_ref[...]], axis=0)

    tiles = _sub_tiles(tm)
    xs = [x_ref[rows, :] for rows in tiles]

    gated = []
    pooled_up = []
    half = D_MODEL // 2
    for rows, x in zip(tiles, xs):
        h = _modulate_bf16(x, shift, scale)
        a_blk = a_ref[rows, :]
        big_operands = [
            (h, wg_ref, slice(0, D_MODEL)),
            (h, wg_ref, slice(D_MODEL, 2 * D_MODEL)),
            (a_blk, wua_ref, slice(0, half)),
            (a_blk, wua_ref, slice(half, D_MODEL)),
        ]
        big = []

        pos = (tile_start + rows.start) % seq_len
        e_all = p_ext[rows.start:rows.start + SUB_TILE + 2 * HALO, :]
        row = lax.broadcasted_iota(jnp.int32, (SUB_TILE + 2 * HALO, 1), 0)
        first_row = jnp.where(pos == 0, HALO, 0)
        end_row = jnp.where(pos + SUB_TILE == seq_len, HALO + SUB_TILE, SUB_TILE + 2 * HALO)
        e_all = jnp.where(jnp.logical_or(row < first_row, row >= end_row), 0.0, e_all)
        t = pos + lax.broadcasted_iota(jnp.int32, (SUB_TILE, 1), 0)
        pgs = []
        for g, w in enumerate(POOL_WINDOWS):
            lhs, w_ref, cols = big_operands[g]
            big.append(_dot(lhs, w_ref[:, cols]))
            sl = slice(g * POOL_GROUP_DIM, (g + 1) * POOL_GROUP_DIM)
            e = e_all[:, sl]
            win = _window_sums(e)[g][HALO:HALO + SUB_TILE]
            lo = jnp.maximum(t - w // 2, 0)
            hi = jnp.minimum(t + w // 2, seq_len)
            inv_cnt = 1.0 / (hi - lo).astype(F32)
            pooled = (win * inv_cnt - e[HALO:HALO + SUB_TILE]).astype(BF16)
            pgs.append(_dot(pooled, poolw_ref[g]))

        ga, gp, au_lo, au_hi = big
        pg = jnp.concatenate(pgs, axis=1)
        gated.append((jax.nn.sigmoid(ga) * jnp.concatenate([au_lo, au_hi], axis=1), gp))
        pooled_up.append(_dot((pg * pscale_ref[...]).astype(BF16), wup_ref[...]))

    merged = [(ga_up + jax.nn.sigmoid(gp) * pu).astype(BF16)
              for (ga_up, gp), pu in zip(gated, pooled_up)]
    mixes = [_dot(m, wo_ref[...]) for m in merged]
    x2s = [_layer_norm(alpha * x + gate * mix, lng1_ref[...], lnb1_ref[...])
           for x, mix in zip(xs, mixes)]

    ys = _ffn_rows(x2s, mod_ref, 6, w1_ref, w2_ref, lng2_ref[...], lnb2_ref[...], alpha)
    for rows, y in zip(tiles, ys):
        o_ref[rows, :] = y


def _back(x2d, attn, pin, mod, mod_row_of_token, w_g, pool_w, pool_scale, w_up_attn, w_up_pool,
          w_out, ln_g1, ln_b1, w1, w2, ln_g2, ln_b2, *, seq_len, alpha):
    n_tok = x2d.shape[0]
    tm = BACK_TILE
    halo_per_tile = tm // HALO
    n_halo_blocks = n_tok // HALO
    row_vec = lambda v: v.reshape(1, D_MODEL)
    return pl.pallas_call(
        functools.partial(_back_kernel, seq_len=seq_len, alpha=alpha),
        grid=(n_tok // tm,),
        in_specs=[
            pl.BlockSpec((tm, D_MODEL), lambda i: (i, 0)),
            pl.BlockSpec((tm, ATTN_WIDTH), lambda i: (i, 0)),
            pl.BlockSpec((tm, POOL_WIDTH), lambda i: (i, 0)),
            pl.BlockSpec((HALO, POOL_WIDTH), lambda i: (jnp.maximum(i * halo_per_tile - 1, 0), 0)),
            pl.BlockSpec((HALO, POOL_WIDTH),
                         lambda i: (jnp.minimum((i + 1) * halo_per_tile, n_halo_blocks - 1), 0)),
            pl.BlockSpec((None, N_MOD, D_MODEL), lambda i: (mod_row_of_token(i * tm), 0, 0)),
            _resident(w_g.shape),
            _resident(pool_w.shape),
            _resident((1, POOL_WIDTH)),
            _resident(w_up_attn.shape),
            _resident(w_up_pool.shape),
            _resident(w_out.shape),
            _resident((1, D_MODEL)),
            _resident((1, D_MODEL)),
            _resident(w1.shape),
            _resident(w2.shape),
            _resident((1, D_MODEL)),
            _resident((1, D_MODEL)),
        ],
        out_specs=pl.BlockSpec((tm, D_MODEL), lambda i: (i, 0)),
        out_shape=jax.ShapeDtypeStruct((n_tok, D_MODEL), F32),
        compiler_params=_compiler_params(1),
        name="back",
    )(x2d, attn, pin, pin, pin, mod, w_g, pool_w, pool_scale.reshape(1, POOL_WIDTH),
      w_up_attn, w_up_pool, w_out, row_vec(ln_g1), row_vec(ln_b1), w1, w2,
      row_vec(ln_g2), row_vec(ln_b2))


def _rope_tables(n_tokens):
    t = np.arange(n_tokens)
    n_freq = HEAD_DIM // 4
    inv_freq = ROPE_THETA ** (-np.arange(n_freq, dtype=np.float64) / n_freq)
    ang = np.concatenate([(t // GRID_W)[:, None] * inv_freq, (t % GRID_W)[:, None] * inv_freq],
                         axis=-1)
    cos, sin = np.cos(ang), np.sin(ang)
    return (jnp.asarray(np.concatenate([cos, cos], axis=-1), F32),
            jnp.asarray(np.concatenate([-sin, sin], axis=-1), F32))


def kernel(x_prompt, x_sample, cache_k, cache_v, c, c_ctx, w_mod, b_mod, ln_g, ln_b, ffn1_w1, ffn1_w2, w_in, q_norm_g, k_norm_g, pool_w, pool_scale, w_up_attn, w_up_pool, w_out, ffn2_w1, ffn2_w2):
    batch, seq, _ = x_prompt.shape
    dec_batch, n_lat, _ = x_sample.shape
    depth = w_mod.shape[0]
    alpha = (2.0 * depth) ** 0.25
    assert seq == SUB_TILE and (batch * seq) % TOKEN_TILE == 0
    assert n_lat % TOKEN_TILE == 0 and n_lat % Q_TILE == 0 and n_lat % KV_CHUNK == 0
    assert 1 + dec_batch <= MOD_ROWS

    cos2, sin2 = _rope_tables((n_lat // GRID_W) * GRID_W)
    cond = jnp.zeros((MOD_ROWS, D_MODEL), F32).at[0].set(c_ctx).at[1:1 + dec_batch].set(c)

    ctx_row = lambda t: 0
    lat_row = lambda t: 1 + t // n_lat

    yp = x_prompt.reshape(batch * seq, D_MODEL)
    ys = x_sample.reshape(dec_batch * n_lat, D_MODEL)
    new_k, new_v = [], []
    for l in range(depth):
        w1a = ffn1_w1[l].astype(BF16)
        w2a = ffn1_w2[l].astype(BF16)
        w1b = ffn2_w1[l].astype(BF16)
        w2b = ffn2_w2[l].astype(BF16)
        w_front = w_in[l, :, :FRONT_WIDTH].astype(BF16)
        w_g = w_in[l, :, FRONT_WIDTH:].astype(BF16)
        pool_w_b = pool_w[l].astype(BF16)
        w_ua = w_up_attn[l].astype(BF16)
        w_upl = w_up_pool[l].astype(BF16)
        w_o = w_out[l].astype(BF16)

        mod = _modulation_table(cond, w_mod[l], b_mod[l]).reshape(MOD_ROWS, N_MOD, D_MODEL)
        back = functools.partial(
            _back, w_g=w_g, pool_w=pool_w_b, pool_scale=pool_scale[l], w_up_attn=w_ua,
            w_up_pool=w_upl, w_out=w_o, ln_g1=ln_g[l, 1], ln_b1=ln_b[l, 1], w1=w1b, w2=w2b,
            ln_g2=ln_g[l, 2], ln_b2=ln_b[l, 2], alpha=alpha)

        x1, attn, k_l, v_l, pin = _front_ctx(
            yp, mod, ctx_row, w1a, w2a, ln_g[l, 0], ln_b[l, 0], w_front,
            q_norm_g[l], k_norm_g[l], alpha=alpha)
        yp = back(x1, attn, pin, mod, ctx_row, seq_len=seq)
        new_k.append(k_l.reshape(batch, seq, N_KV_HEADS, HEAD_DIM))
        new_v.append(v_l.reshape(batch, seq, N_KV_HEADS, HEAD_DIM))

        x1, q, kb, vb, pin = _front_lat(
            ys, mod, lat_row, w1a, w2a, ln_g[l, 0], ln_b[l, 0], w_front,
            q_norm_g[l], k_norm_g[l], cos2, sin2, n_lat=n_lat, alpha=alpha)
        past = cache_k.shape[2]
        attn = _attn_lat(q, kb, vb,
                         cache_k[:, l].reshape(dec_batch, past, KV_WIDTH),
                         cache_v[:, l].reshape(dec_batch, past, KV_WIDTH), n_lat)
        ys = back(x1, attn, pin, mod, lat_row, seq_len=n_lat)

    return (yp.reshape(batch, seq, D_MODEL),
            ys.reshape(dec_batch, n_lat, D_MODEL),
            jnp.stack(new_k, axis=1),
            jnp.stack(new_v, axis=1))
```

```python
import functools
import math

import numpy as np

import jax
import jax.numpy as jnp
from jax import lax
from jax.experimental import pallas as pl
from jax.experimental.pallas import tpu as pltpu

D_MODEL = 1024
GRID_W = 64
N_HEADS = 8
N_KV_HEADS = 2
HEAD_DIM = 128
GROUP = N_HEADS // N_KV_HEADS
ATTN_WIDTH = N_HEADS * HEAD_DIM
KV_WIDTH = N_KV_HEADS * HEAD_DIM
POOL_WINDOWS = (2, 4, 8, 16)
N_POOL_GROUPS = 4
POOL_GROUP_DIM = 128
POOL_WIDTH = N_POOL_GROUPS * POOL_GROUP_DIM
QKV_WIDTH = ATTN_WIDTH + 2 * KV_WIDTH
FRONT_WIDTH = QKV_WIDTH + POOL_WIDTH
D_FF = 2816
N_MOD = 9
ROPE_THETA = 10000.0
LN_EPS = 1e-6
RMS_EPS = 1e-6

SUBLANES = 8
LANES = 128
MXU_DIM = 256
VMEM_LIMIT_BYTES = 56 * 1024 * 1024

HALO = SUBLANES
MOD_ROWS = SUBLANES
MOD_BLOCK = 2304
TOKEN_TILE = 512
SUB_TILE = 256
BF16_SUBLANES = 2 * SUBLANES
Q_TILE = 512
Q_SUB = 256
KV_CHUNK = 256
FF_CHUNKS = ((0, 768), (768, 1024), (1792, 1024))
Q_SCALE = HEAD_DIM ** -0.5 * math.log2(math.e)
V_AUG = MXU_DIM

BF16 = jnp.bfloat16
F32 = jnp.float32


def _dot(a, b):
    return jnp.dot(a, b, preferred_element_type=F32)


def _dot_nt(a, b):
    return lax.dot_general(a, b, (((1,), (1,)), ((), ())), preferred_element_type=F32)


def _layer_norm(x, g, b):
    mu = jnp.mean(x, axis=-1, keepdims=True)
    xc = x - mu
    var = jnp.mean(xc * xc, axis=-1, keepdims=True)
    return xc * lax.rsqrt(var + LN_EPS) * g + b


def _modulate_bf16(x, shift, scale):
    return (x * (1.0 + scale) + shift).astype(BF16)


def _resident(shape):
    zeros = (0,) * len(shape)
    return pl.BlockSpec(shape, lambda *_: zeros, pipeline_mode=pl.Buffered(1))


def _compiler_params(n_axes):
    return pltpu.CompilerParams(
        dimension_semantics=("arbitrary",) * n_axes,
        vmem_limit_bytes=VMEM_LIMIT_BYTES,
    )


def _mod_kernel(cond_ref, w_ref, b_ref, o_ref):
    c = cond_ref[...]
    s = (c * jax.nn.sigmoid(c)).astype(BF16)
    o_ref[...] = _dot(s, w_ref[...].astype(BF16)) + b_ref[...]


def _modulation_table(cond, w_mod, b_mod):
    n_out = w_mod.shape[1]
    bn = MOD_BLOCK
    return pl.pallas_call(
        _mod_kernel,
        grid=(n_out // bn,),
        in_specs=[
            pl.BlockSpec((MOD_ROWS, D_MODEL), lambda j: (0, 0)),
            pl.BlockSpec((D_MODEL, bn), lambda j: (0, j)),
            pl.BlockSpec((1, bn), lambda j: (0, j)),
        ],
        out_specs=pl.BlockSpec((MOD_ROWS, bn), lambda j: (0, j)),
        out_shape=jax.ShapeDtypeStruct((MOD_ROWS, n_out), F32),
        compiler_params=_compiler_params(1),
        name="modulation",
    )(cond, w_mod, b_mod.reshape(1, n_out))


def _ffn_rows(xs, mod_ref, mod_base, w1_ref, w2_ref, ln_g, ln_b, alpha):
    shift = mod_ref[mod_base:mod_base + 1, :]
    scale = mod_ref[mod_base + 1:mod_base + 2, :]
    gate = mod_ref[mod_base + 2:mod_base + 3, :]
    hs = [_modulate_bf16(x, shift, scale) for x in xs]
    fs = [None] * len(xs)
    for start, width in FF_CHUNKS:
        us = []
        for h in hs:
            a = _dot(h, w1_ref[:, start:start + width])
            b = _dot(h, w1_ref[:, D_FF + start:D_FF + start + width])
            us.append((a * jax.nn.sigmoid(a) * b).astype(BF16))
        for i, u in enumerate(us):
            part = _dot(u, w2_ref[start:start + width, :])
            fs[i] = part if fs[i] is None else fs[i] + part
    return [_layer_norm(alpha * x + (0.5 * gate) * f, ln_g, ln_b) for x, f in zip(xs, fs)]


def _sub_tiles(n_rows):
    return [slice(r0, r0 + SUB_TILE) for r0 in range(0, n_rows, SUB_TILE)]


def _rms_head(xh, g):
    ms = jnp.mean(xh * xh, axis=-1, keepdims=True)
    return xh * lax.rsqrt(ms + RMS_EPS) * g


def _rope(xh, cos2, sin2):
    return xh * cos2 + pltpu.roll(xh, HEAD_DIM // 2, axis=1) * sin2


def _ones_column_block(rows):
    col = lax.broadcasted_iota(jnp.int32, (rows, V_AUG - HEAD_DIM), 1)
    return jnp.where(col == 0, 1.0, 0.0).astype(BF16)


def _stack_heads(heads):
    return jnp.concatenate(heads, axis=0)


def _head_cols(hh):
    return slice(hh * HEAD_DIM, (hh + 1) * HEAD_DIM)


def _front_ctx_kernel(x_ref, mod_ref, w1_ref, w2_ref, lng_ref, lnb_ref, win_ref, qg_ref, kg_ref,
                      *refs, n_riders, alpha):
    rider_in = refs[:n_riders]
    x1_ref, a_ref, k_ref, v_ref, pin_ref = refs[n_riders:n_riders + 5]
    rider_out = refs[n_riders + 5:]
    for src, dst in zip(rider_in, rider_out):
        dst[...] = src[...].astype(BF16)
    qg = qg_ref[...] * Q_SCALE
    kg = kg_ref[...]
    tiles = _sub_tiles(x_ref.shape[0])
    x1s = _ffn_rows([x_ref[rows, :] for rows in tiles], mod_ref, 0, w1_ref, w2_ref,
                    lng_ref[...], lnb_ref[...], alpha)
    gw = GROUP * HEAD_DIM
    scores = []
    for rows, x1 in zip(tiles, x1s):
        x1_ref[rows, :] = x1
        h = _modulate_bf16(x1, mod_ref[3:4, :], mod_ref[4:5, :])
        q_cols = [_dot(h, win_ref[:, :gw]), _dot(h, win_ref[:, gw:ATTN_WIDTH])]
        q4s = [_stack_heads([_rms_head(q_cols[0][:, _head_cols(g)], qg).astype(BF16)
                             for g in range(GROUP)])]
        kv_cols = _dot(h, win_ref[:, ATTN_WIDTH:QKV_WIDTH])
        q4s.append(_stack_heads([_rms_head(q_cols[1][:, _head_cols(g)], qg).astype(BF16)
                                 for g in range(GROUP)]))
        pin_ref[rows, :] = _dot(h, win_ref[:, QKV_WIDTH:])
        for kv in range(N_KV_HEADS):
            kn = _rms_head(kv_cols[:, _head_cols(kv)], kg)
            v_kv = kv_cols[:, KV_WIDTH + kv * HEAD_DIM:KV_WIDTH + (kv + 1) * HEAD_DIM]
            k_ref[rows, kv, :] = kn
            v_ref[rows, kv, :] = v_kv
            scores.append((rows, kv, _dot_nt(kn.astype(BF16), q4s[kv]), v_kv.T.astype(BF16)))
    probs = []
    for rows, kv, st, vt_kv in scores:
        p = jnp.exp2(st - jnp.max(st, axis=0, keepdims=True))
        probs.append((rows, kv, p.astype(BF16), jnp.sum(p, axis=0, keepdims=True), vt_kv))
    for rows, kv, p, l, vt_kv in probs:
        ot = _dot(vt_kv, p) / l
        for g in range(GROUP):
            a_ref[rows, _head_cols(kv * GROUP + g)] = (
                ot[:, g * SUB_TILE:(g + 1) * SUB_TILE].T.astype(BF16))


def _front_lat_kernel(x_ref, mod_ref, w1_ref, w2_ref, lng_ref, lnb_ref, win_ref, qg_ref, kg_ref,
                      cos_ref, sin_ref, x1_ref, q_ref, kb_ref, vb_ref, pin_ref, *, alpha):
    qg = qg_ref[...] * Q_SCALE
    kg = kg_ref[...]
    ones_col = _ones_column_block(SUB_TILE)
    tiles = _sub_tiles(x_ref.shape[0])
    x1s = _ffn_rows([x_ref[rows, :] for rows in tiles], mod_ref, 0, w1_ref, w2_ref,
                    lng_ref[...], lnb_ref[...], alpha)
    for rows, x1 in zip(tiles, x1s):
        x1_ref[rows, :] = x1
        proj = _dot(_modulate_bf16(x1, mod_ref[3:4, :], mod_ref[4:5, :]), win_ref[...])
        pin_ref[rows, :] = proj[:, QKV_WIDTH:]
        cos2 = cos_ref[rows, :]
        sin2 = sin_ref[rows, :]
        for hh in range(N_HEADS):
            q_ref[rows, _head_cols(hh)] = _rope(
                _rms_head(proj[:, _head_cols(hh)], qg), cos2, sin2).astype(BF16)
        for kv in range(N_KV_HEADS):
            kn = _rms_head(proj[:, ATTN_WIDTH + kv * HEAD_DIM:ATTN_WIDTH + (kv + 1) * HEAD_DIM], kg)
            kb_ref[rows, _head_cols(kv)] = _rope(kn, cos2, sin2).astype(BF16)
            v0 = ATTN_WIDTH + KV_WIDTH + kv * HEAD_DIM
            vb_ref[rows, kv * V_AUG:kv * V_AUG + HEAD_DIM] = proj[:, v0:v0 + HEAD_DIM].astype(BF16)
            vb_ref[rows, kv * V_AUG + HEAD_DIM:(kv + 1) * V_AUG] = ones_col


def _front_in_specs(tm, mod_row_of_token, w1, w2, w_front):
    return [
        pl.BlockSpec((tm, D_MODEL), lambda i: (i, 0)),
        pl.BlockSpec((None, N_MOD, D_MODEL), lambda i: (mod_row_of_token(i * tm), 0, 0)),
        _resident(w1.shape),
        _resident(w2.shape),
        _resident((1, D_MODEL)),
        _resident((1, D_MODEL)),
        _resident(w_front.shape),
        _resident((1, HEAD_DIM)),
        _resident((1, HEAD_DIM)),
    ]


def _front_ctx(x2d, mod, mod_row_of_token, w1, w2, ln_g, ln_b, w_front, q_g, k_g, riders,
               *, alpha):
    n_tok = x2d.shape[0]
    tm = TOKEN_TILE
    n_steps = n_tok // tm
    tok = lambda width: pl.BlockSpec((tm, width), lambda i: (i, 0))
    cache_spec = pl.BlockSpec((tm, N_KV_HEADS, HEAD_DIM), lambda i: (i, 0, 0))
    cache_shape = jax.ShapeDtypeStruct((n_tok, N_KV_HEADS, HEAD_DIM), F32)
    rider_in_specs, rider_out_specs, rider_shapes = [], [], []
    for w, col_block, width in riders:
        chunk = w.shape[0] // n_steps
        assert chunk * n_steps == w.shape[0] and chunk % BF16_SUBLANES == 0
        rider_in_specs.append(pl.BlockSpec((chunk, width), lambda i, cb=col_block: (i, cb)))
        rider_out_specs.append(pl.BlockSpec((chunk, width), lambda i: (i, 0)))
        rider_shapes.append(jax.ShapeDtypeStruct((w.shape[0], width), BF16))
    return pl.pallas_call(
        functools.partial(_front_ctx_kernel, n_riders=len(riders), alpha=alpha),
        grid=(n_steps,),
        in_specs=_front_in_specs(tm, mod_row_of_token, w1, w2, w_front) + rider_in_specs,
        out_specs=[tok(D_MODEL), tok(ATTN_WIDTH), cache_spec, cache_spec, tok(POOL_WIDTH)]
        + rider_out_specs,
        out_shape=[
            jax.ShapeDtypeStruct((n_tok, D_MODEL), F32),
            jax.ShapeDtypeStruct((n_tok, ATTN_WIDTH), BF16),
            cache_shape,
            cache_shape,
            jax.ShapeDtypeStruct((n_tok, POOL_WIDTH), F32),
        ] + rider_shapes,
        compiler_params=_compiler_params(1),
        name="front_ctx",
    )(x2d, mod, w1, w2, ln_g.reshape(1, D_MODEL), ln_b.reshape(1, D_MODEL), w_front,
      q_g.reshape(1, HEAD_DIM), k_g.reshape(1, HEAD_DIM), *[w for w, _, _ in riders])


def _front_lat(x2d, mod, mod_row_of_token, w1, w2, ln_g, ln_b, w_front, q_g, k_g, cos2, sin2,
               *, n_lat, alpha):
    n_tok = x2d.shape[0]
    tm = TOKEN_TILE
    tiles_per_seq = n_lat // tm
    tok = lambda width: pl.BlockSpec((tm, width), lambda i: (i, 0))
    rope_spec = pl.BlockSpec((tm, HEAD_DIM), lambda i: (i % tiles_per_seq, 0))
    return pl.pallas_call(
        functools.partial(_front_lat_kernel, alpha=alpha),
        grid=(n_tok // tm,),
        in_specs=_front_in_specs(tm, mod_row_of_token, w1, w2, w_front) + [rope_spec, rope_spec],
        out_specs=[tok(D_MODEL), tok(ATTN_WIDTH), tok(KV_WIDTH), tok(N_KV_HEADS * V_AUG),
                   tok(POOL_WIDTH)],
        out_shape=[
            jax.ShapeDtypeStruct((n_tok, D_MODEL), F32),
            jax.ShapeDtypeStruct((n_tok, ATTN_WIDTH), BF16),
            jax.ShapeDtypeStruct((n_tok, KV_WIDTH), BF16),
            jax.ShapeDtypeStruct((n_tok, N_KV_HEADS * V_AUG), BF16),
            jax.ShapeDtypeStruct((n_tok, POOL_WIDTH), F32),
        ],
        compiler_params=_compiler_params(1),
        name="front_lat",
    )(x2d, mod, w1, w2, ln_g.reshape(1, D_MODEL), ln_b.reshape(1, D_MODEL), w_front,
      q_g.reshape(1, HEAD_DIM), k_g.reshape(1, HEAD_DIM), cos2, sin2)


def _attn_lat_kernel(q_ref, kc_ref, vc_ref, kl_ref, vl_ref, o_ref):
    n_lat = kl_ref.shape[0]
    past = kc_ref.shape[0]
    vc = jnp.concatenate([vc_ref[...].astype(BF16), _ones_column_block(past)], axis=1)
    chunks = [(kc_ref[...].astype(BF16), vc)]
    for c0 in range(0, n_lat, KV_CHUNK):
        chunks.append((kl_ref[c0:c0 + KV_CHUNK, :], vl_ref[c0:c0 + KV_CHUNK, :]))
    for r0 in range(0, q_ref.shape[0], Q_SUB):
        rows = slice(r0, r0 + Q_SUB)
        q4 = _stack_heads([q_ref[rows, _head_cols(g)] for g in range(GROUP)])
        m = None
        acc = None
        for k_c, v_c in chunks:
            s = _dot_nt(q4, k_c)
            m_c = jnp.max(s, axis=-1, keepdims=True)
            if m is None:
                m = m_c
                acc = _dot(jnp.exp2(s - m).astype(BF16), v_c)
            else:
                m_new = jnp.maximum(m, m_c)
                acc = jnp.exp2(m - m_new) * acc + _dot(jnp.exp2(s - m_new).astype(BF16), v_c)
                m = m_new
        o = acc[:, :HEAD_DIM] / acc[:, HEAD_DIM:HEAD_DIM + 1]
        for g in range(GROUP):
            o_ref[rows, _head_cols(g)] = o[g * Q_SUB:(g + 1) * Q_SUB].astype(BF16)


def _attn_lat(q, kb, vb, cache_k, cache_v, n_lat):
    n_tok = q.shape[0]
    n_batch = n_tok // n_lat
    past = cache_k.shape[1]
    tq = Q_TILE
    q_tiles = n_lat // tq
    gw = GROUP * HEAD_DIM
    cache_spec = pl.BlockSpec((None, past, HEAD_DIM), lambda b, kv, i: (b, 0, kv))
    return pl.pallas_call(
        _attn_lat_kernel,
        grid=(n_batch, N_KV_HEADS, q_tiles),
        in_specs=[
            pl.BlockSpec((tq, gw), lambda b, kv, i: (b * q_tiles + i, kv)),
            cache_spec,
            cache_spec,
            pl.BlockSpec((n_lat, HEAD_DIM), lambda b, kv, i: (b, kv)),
            pl.BlockSpec((n_lat, V_AUG), lambda b, kv, i: (b, kv)),
        ],
        out_specs=pl.BlockSpec((tq, gw), lambda b, kv, i: (b * q_tiles + i, kv)),
        out_shape=jax.ShapeDtypeStruct((n_tok, ATTN_WIDTH), BF16),
        compiler_params=_compiler_params(3),
        name="attn_lat",
    )(q, cache_k, cache_v, kb, vb)


def _window_sums(e):
    n = e.shape[0]
    back = lambda a, k: pltpu.roll(a, k, axis=0)
    fwd = lambda a, k: pltpu.roll(a, n - k, axis=0)
    c2 = e + back(e, 1)
    c4 = back(c2, 1) + fwd(c2, 1)
    c8 = back(c4, 2) + fwd(c4, 2)
    c16 = back(c8, 4) + fwd(c8, 4)
    return c2, c4, c8, c16


def _back_kernel(x_ref, a_ref, pin_ref, pp_ref, pn_ref, mod_ref, wg_ref, poolw_ref, pscale_ref,
                 wua_ref, wup_ref, wo_ref, lng1_ref, lnb1_ref, w1_ref, w2_ref, lng2_ref, lnb2_ref,
                 o_ref, *, seq_len, alpha):
    tm = x_ref.shape[0]
    tile_start = pl.program_id(0) * tm
    shift = mod_ref[3:4, :]
    scale = mod_ref[4:5, :]
    gate = mod_ref[5:6, :]
    p_ext = jnp.concatenate([pp_ref[...], pin_ref[...], pn_ref[...]], axis=0)

    tiles = _sub_tiles(tm)
    xs = [x_ref[rows, :] for rows in tiles]

    gated = []
    pooled_up = []
    half = D_MODEL // 2
    for rows, x in zip(tiles, xs):
        h = _modulate_bf16(x, shift, scale)
        a_blk = a_ref[rows, :]
        big_operands = [
            (h, wg_ref, slice(0, D_MODEL)),
            (h, wg_ref, slice(D_MODEL, 2 * D_MODEL)),
            (a_blk, wua_ref, slice(0, half)),
            (a_blk, wua_ref, slice(half, D_MODEL)),
        ]
        big = []

        pos = (tile_start + rows.start) % seq_len
        e_all = p_ext[rows.start:rows.start + SUB_TILE + 2 * HALO, :]
        row = lax.broadcasted_iota(jnp.int32, (SUB_TILE + 2 * HALO, 1), 0)
        first_row = jnp.where(pos == 0, HALO, 0)
        end_row = jnp.where(pos + SUB_TILE == seq_len, HALO + SUB_TILE, SUB_TILE + 2 * HALO)
        e_all = jnp.where(jnp.logical_or(row < first_row, row >= end_row), 0.0, e_all)
        t = pos + lax.broadcasted_iota(jnp.int32, (SUB_TILE, 1), 0)
        pgs = []
        for g, w in enumerate(POOL_WINDOWS):
            lhs, w_ref, cols = big_operands[g]
            big.append(_dot(lhs, w_ref[:, cols]))
            sl = slice(g * POOL_GROUP_DIM, (g + 1) * POOL_GROUP_DIM)
            e = e_all[:, sl]
            win = _window_sums(e)[g][HALO:HALO + SUB_TILE]
            lo = jnp.maximum(t - w // 2, 0)
            hi = jnp.minimum(t + w // 2, seq_len)
            inv_cnt = 1.0 / (hi - lo).astype(F32)
            pooled = (win * inv_cnt - e[HALO:HALO + SUB_TILE]).astype(BF16)
            pgs.append(_dot(pooled, poolw_ref[g]))

        ga, gp, au_lo, au_hi = big
        pg = jnp.concatenate(pgs, axis=1)
        gated.append((jax.nn.sigmoid(ga) * jnp.concatenate([au_lo, au_hi], axis=1), gp))
        pooled_up.append(_dot((pg * pscale_ref[...]).astype(BF16), wup_ref[...]))

    merged = [(ga_up + jax.nn.sigmoid(gp) * pu).astype(BF16)
              for (ga_up, gp), pu in zip(gated, pooled_up)]
    mixes = [_dot(m, wo_ref[...]) for m in merged]
    x2s = [_layer_norm(alpha * x + gate * mix, lng1_ref[...], lnb1_ref[...])
           for x, mix in zip(xs, mixes)]

    ys = _ffn_rows(x2s, mod_ref, 6, w1_ref, w2_ref, lng2_ref[...], lnb2_ref[...], alpha)
    for rows, y in zip(tiles, ys):
        o_ref[rows, :] = y


def _back(x2d, attn, pin, mod, mod_row_of_token, w_g, pool_w, pool_scale, w_up_attn, w_up_pool,
          w_out, ln_g1, ln_b1, w1, w2, ln_g2, ln_b2, *, seq_len, alpha):
    n_tok = x2d.shape[0]
    tm = TOKEN_TILE
    halo_per_tile = tm // HALO
    n_halo_blocks = n_tok // HALO
    row_vec = lambda v: v.reshape(1, D_MODEL)
    return pl.pallas_call(
        functools.partial(_back_kernel, seq_len=seq_len, alpha=alpha),
        grid=(n_tok // tm,),
        in_specs=[
            pl.BlockSpec((tm, D_MODEL), lambda i: (i, 0)),
            pl.BlockSpec((tm, ATTN_WIDTH), lambda i: (i, 0)),
            pl.BlockSpec((tm, POOL_WIDTH), lambda i: (i, 0)),
            pl.BlockSpec((HALO, POOL_WIDTH), lambda i: (jnp.maximum(i * halo_per_tile - 1, 0), 0)),
            pl.BlockSpec((HALO, POOL_WIDTH),
                         lambda i: (jnp.minimum((i + 1) * halo_per_tile, n_halo_blocks - 1), 0)),
            pl.BlockSpec((None, N_MOD, D_MODEL), lambda i: (mod_row_of_token(i * tm), 0, 0)),
            _resident(w_g.shape),
            _resident(pool_w.shape),
            _resident((1, POOL_WIDTH)),
            _resident(w_up_attn.shape),
            _resident(w_up_pool.shape),
            _resident(w_out.shape),
            _resident((1, D_MODEL)),
            _resident((1, D_MODEL)),
            _resident(w1.shape),
            _resident(w2.shape),
            _resident((1, D_MODEL)),
            _resident((1, D_MODEL)),
        ],
        out_specs=pl.BlockSpec((tm, D_MODEL), lambda i: (i, 0)),
        out_shape=jax.ShapeDtypeStruct((n_tok, D_MODEL), F32),
        compiler_params=_compiler_params(1),
        name="back",
    )(x2d, attn, pin, pin, pin, mod, w_g, pool_w, pool_scale.reshape(1, POOL_WIDTH),
      w_up_attn, w_up_pool, w_out, row_vec(ln_g1), row_vec(ln_b1), w1, w2,
      row_vec(ln_g2), row_vec(ln_b2))


def _rope_tables(n_tokens):
    t = np.arange(n_tokens)
    n_freq = HEAD_DIM // 4
    inv_freq = ROPE_THETA ** (-np.arange(n_freq, dtype=np.float64) / n_freq)
    ang = np.concatenate([(t // GRID_W)[:, None] * inv_freq, (t % GRID_W)[:, None] * inv_freq],
                         axis=-1)
    cos, sin = np.cos(ang), np.sin(ang)
    return (jnp.asarray(np.concatenate([cos, cos], axis=-1), F32),
            jnp.asarray(np.concatenate([-sin, sin], axis=-1), F32))


def kernel(x_prompt, x_sample, cache_k, cache_v, c, c_ctx, w_mod, b_mod, ln_g, ln_b, ffn1_w1, ffn1_w2, w_in, q_norm_g, k_norm_g, pool_w, pool_scale, w_up_attn, w_up_pool, w_out, ffn2_w1, ffn2_w2):
    batch, seq, _ = x_prompt.shape
    dec_batch, n_lat, _ = x_sample.shape
    depth = w_mod.shape[0]
    alpha = (2.0 * depth) ** 0.25
    assert seq == SUB_TILE and (batch * seq) % TOKEN_TILE == 0
    assert n_lat % TOKEN_TILE == 0 and n_lat % Q_TILE == 0 and n_lat % KV_CHUNK == 0
    assert 1 + dec_batch <= MOD_ROWS

    cos2, sin2 = _rope_tables((n_lat // GRID_W) * GRID_W)
    cond = jnp.zeros((MOD_ROWS, D_MODEL), F32).at[0].set(c_ctx).at[1:1 + dec_batch].set(c)

    ctx_row = lambda t: 0
    lat_row = lambda t: 1 + t // n_lat

    yp = x_prompt.reshape(batch * seq, D_MODEL)
    ys = x_sample.reshape(dec_batch * n_lat, D_MODEL)
    new_k, new_v = [], []
    for l in range(depth):
        w1a = ffn1_w1[l].astype(BF16)
        w2a = ffn1_w2[l].astype(BF16)
        w_front = w_in[l, :, :FRONT_WIDTH].astype(BF16)
        pool_w_b = pool_w[l].astype(BF16)
        assert w_in.shape[2] == 2 * FRONT_WIDTH
        riders = [(ffn2_w1[l], 0, 2 * D_FF), (ffn2_w2[l], 0, D_MODEL),
                  (w_in[l], 1, FRONT_WIDTH), (w_up_attn[l], 0, D_MODEL),
                  (w_up_pool[l], 0, D_MODEL), (w_out[l], 0, D_MODEL)]

        mod = _modulation_table(cond, w_mod[l], b_mod[l]).reshape(MOD_ROWS, N_MOD, D_MODEL)

        x1, attn, k_l, v_l, pin, w1b, w2b, w_g, w_ua, w_upl, w_o = _front_ctx(
            yp, mod, ctx_row, w1a, w2a, ln_g[l, 0], ln_b[l, 0], w_front,
            q_norm_g[l], k_norm_g[l], riders, alpha=alpha)
        back = functools.partial(
            _back, w_g=w_g, pool_w=pool_w_b, pool_scale=pool_scale[l], w_up_attn=w_ua,
            w_up_pool=w_upl, w_out=w_o, ln_g1=ln_g[l, 1], ln_b1=ln_b[l, 1], w1=w1b, w2=w2b,
            ln_g2=ln_g[l, 2], ln_b2=ln_b[l, 2], alpha=alpha)
        yp = back(x1, attn, pin, mod, ctx_row, seq_len=seq)
        new_k.append(k_l.reshape(batch, seq, N_KV_HEADS, HEAD_DIM))
        new_v.append(v_l.reshape(batch, seq, N_KV_HEADS, HEAD_DIM))

        x1, q, kb, vb, pin = _front_lat(
            ys, mod, lat_row, w1a, w2a, ln_g[l, 0], ln_b[l, 0], w_front,
            q_norm_g[l], k_norm_g[l], cos2, sin2, n_lat=n_lat, alpha=alpha)
        past = cache_k.shape[2]
        attn = _attn_lat(q, kb, vb,
                         cache_k[:, l].reshape(dec_batch, past, KV_WIDTH),
                         cache_v[:, l].reshape(dec_batch, past, KV_WIDTH), n_lat)
        ys = back(x1, attn, pin, mod, lat_row, seq_len=n_lat)

    return (yp.reshape(batch, seq, D_MODEL),
            ys.reshape(dec_batch, n_lat, D_MODEL),
            jnp.stack(new_k, axis=1),
            jnp.stack(new_v, axis=1))
```

```python
import functools
import math

import numpy as np

import jax
import jax.numpy as jnp
from jax import lax
from jax.experimental import pallas as pl
from jax.experimental.pallas import tpu as pltpu

D_MODEL = 1024
GRID_W = 64
N_HEADS = 8
N_KV_HEADS = 2
HEAD_DIM = 128
GROUP = N_HEADS // N_KV_HEADS
ATTN_WIDTH = N_HEADS * HEAD_DIM
KV_WIDTH = N_KV_HEADS * HEAD_DIM
POOL_WINDOWS = (2, 4, 8, 16)
N_POOL_GROUPS = 4
POOL_GROUP_DIM = 128
POOL_WIDTH = N_POOL_GROUPS * POOL_GROUP_DIM
QKV_WIDTH = ATTN_WIDTH + 2 * KV_WIDTH
FRONT_WIDTH = QKV_WIDTH + POOL_WIDTH
D_FF = 2816
N_MOD = 9
ROPE_THETA = 10000.0
LN_EPS = 1e-6
RMS_EPS = 1e-6

SUBLANES = 8
LANES = 128
MXU_DIM = 256
VMEM_LIMIT_BYTES = 58 * 1024 * 1024

HALO = SUBLANES
MOD_ROWS = SUBLANES
MOD_BLOCK = 2304
TOKEN_TILE = 512
SUB_TILE = 256
BF16_SUBLANES = 2 * SUBLANES
PRE_STEPS = 16
Q_TILE = 512
Q_SUB = 256
KV_CHUNK = 256
FF_CHUNKS = ((0, 768), (768, 1024), (1792, 1024))
Q_SCALE = HEAD_DIM ** -0.5 * math.log2(math.e)
V_AUG = MXU_DIM

BF16 = jnp.bfloat16
F32 = jnp.float32


def _dot(a, b):
    return jnp.dot(a, b, preferred_element_type=F32)


def _dot_nt(a, b):
    return lax.dot_general(a, b, (((1,), (1,)), ((), ())), preferred_element_type=F32)


def _layer_norm(x, g, b):
    mu = jnp.mean(x, axis=-1, keepdims=True)
    xc = x - mu
    var = jnp.mean(xc * xc, axis=-1, keepdims=True)
    return xc * lax.rsqrt(var + LN_EPS) * g + b


def _modulate_bf16(x, shift, scale):
    return (x * (1.0 + scale) + shift).astype(BF16)


def _resident(shape):
    zeros = (0,) * len(shape)
    return pl.BlockSpec(shape, lambda *_: zeros, pipeline_mode=pl.Buffered(1))


def _compiler_params(n_axes):
    return pltpu.CompilerParams(
        dimension_semantics=("arbitrary",) * n_axes,
        vmem_limit_bytes=VMEM_LIMIT_BYTES,
    )


def _mod_kernel(cond_ref, w_ref, b_ref, o_ref):
    c = cond_ref[...]
    s = (c * jax.nn.sigmoid(c)).astype(BF16)
    o_ref[...] = _dot(s, w_ref[...].astype(BF16)) + b_ref[...]


def _modulation_table(cond, w_mod, b_mod):
    n_out = w_mod.shape[1]
    bn = MOD_BLOCK
    return pl.pallas_call(
        _mod_kernel,
        grid=(n_out // bn,),
        in_specs=[
            pl.BlockSpec((MOD_ROWS, D_MODEL), lambda j: (0, 0)),
            pl.BlockSpec((D_MODEL, bn), lambda j: (0, j)),
            pl.BlockSpec((1, bn), lambda j: (0, j)),
        ],
        out_specs=pl.BlockSpec((MOD_ROWS, bn), lambda j: (0, j)),
        out_shape=jax.ShapeDtypeStruct((MOD_ROWS, n_out), F32),
        compiler_params=_compiler_params(1),
        name="modulation",
    )(cond, w_mod, b_mod.reshape(1, n_out))


def _ffn_rows(xs, mod_ref, mod_base, w1_ref, w2_ref, ln_g, ln_b, alpha):
    shift = mod_ref[mod_base:mod_base + 1, :]
    scale = mod_ref[mod_base + 1:mod_base + 2, :]
    gate = mod_ref[mod_base + 2:mod_base + 3, :]
    hs = [_modulate_bf16(x, shift, scale) for x in xs]
    fs = [None] * len(xs)
    for start, width in FF_CHUNKS:
        us = []
        for h in hs:
            a = _dot(h, w1_ref[:, start:start + width])
            b = _dot(h, w1_ref[:, D_FF + start:D_FF + start + width])
            us.append((a * jax.nn.sigmoid(a) * b).astype(BF16))
        for i, u in enumerate(us):
            part = _dot(u, w2_ref[start:start + width, :])
            fs[i] = part if fs[i] is None else fs[i] + part
    return [_layer_norm(alpha * x + (0.5 * gate) * f, ln_g, ln_b) for x, f in zip(xs, fs)]


def _sub_tiles(n_rows):
    return [slice(r0, r0 + SUB_TILE) for r0 in range(0, n_rows, SUB_TILE)]


def _rms_head(xh, g):
    ms = jnp.mean(xh * xh, axis=-1, keepdims=True)
    return xh * lax.rsqrt(ms + RMS_EPS) * g


def _rope(xh, cos2, sin2):
    return xh * cos2 + pltpu.roll(xh, HEAD_DIM // 2, axis=1) * sin2


def _ones_column_block(rows):
    col = lax.broadcasted_iota(jnp.int32, (rows, V_AUG - HEAD_DIM), 1)
    return jnp.where(col == 0, 1.0, 0.0).astype(BF16)


def _stack_heads(heads):
    return jnp.concatenate(heads, axis=0)


def _head_cols(hh):
    return slice(hh * HEAD_DIM, (hh + 1) * HEAD_DIM)


def _front_ctx_kernel(x_ref, mod_ref, lng_ref, lnb_ref, qg_ref, kg_ref, *refs,
                      n_own, n_riders, n_pre, alpha):
    own_in = refs[:n_own]
    rider_in = refs[n_own:n_own + n_riders]
    outs = refs[n_own + n_riders:]
    x1_ref, a_ref, k_ref, v_ref, pin_ref = outs[:5]
    rider_out = outs[5:5 + n_riders]
    own_out = outs[5 + n_riders:5 + n_riders + n_own]
    w1_ref, w2_ref, win_ref = outs[5 + n_riders + n_own:]
    step = pl.program_id(0)

    @pl.when(step < n_pre)
    def _load_own_weights():
        for src, dst, whole in zip(own_in, own_out, (w1_ref, w2_ref, win_ref)):
            chunk = src.shape[0]
            val = src[...].astype(BF16)
            dst[...] = val
            whole[pl.ds(pl.multiple_of(step * chunk, BF16_SUBLANES), chunk), :] = val

    @pl.when(step >= n_pre)
    def _token_step():
        for src, dst in zip(rider_in, rider_out):
            dst[...] = src[...].astype(BF16)
        _front_ctx_body(x_ref, mod_ref, w1_ref, w2_ref, lng_ref, lnb_ref, win_ref, qg_ref, kg_ref,
                        x1_ref, a_ref, k_ref, v_ref, pin_ref, alpha)


def _front_ctx_body(x_ref, mod_ref, w1_ref, w2_ref, lng_ref, lnb_ref, win_ref, qg_ref, kg_ref,
                    x1_ref, a_ref, k_ref, v_ref, pin_ref, alpha):
    qg = qg_ref[...] * Q_SCALE
    kg = kg_ref[...]
    tiles = _sub_tiles(x_ref.shape[0])
    x1s = _ffn_rows([x_ref[rows, :] for rows in tiles], mod_ref, 0, w1_ref, w2_ref,
                    lng_ref[...], lnb_ref[...], alpha)
    gw = GROUP * HEAD_DIM
    scores = []
    for rows, x1 in zip(tiles, x1s):
        x1_ref[rows, :] = x1
        h = _modulate_bf16(x1, mod_ref[3:4, :], mod_ref[4:5, :])
        q_cols = [_dot(h, win_ref[:, :gw]), _dot(h, win_ref[:, gw:ATTN_WIDTH])]
        q4s = [_stack_heads([_rms_head(q_cols[0][:, _head_cols(g)], qg).astype(BF16)
                             for g in range(GROUP)])]
        kv_cols = _dot(h, win_ref[:, ATTN_WIDTH:QKV_WIDTH])
        q4s.append(_stack_heads([_rms_head(q_cols[1][:, _head_cols(g)], qg).astype(BF16)
                                 for g in range(GROUP)]))
        pin_ref[rows, :] = _dot(h, win_ref[:, QKV_WIDTH:])
        for kv in range(N_KV_HEADS):
            kn = _rms_head(kv_cols[:, _head_cols(kv)], kg)
            v_kv = kv_cols[:, KV_WIDTH + kv * HEAD_DIM:KV_WIDTH + (kv + 1) * HEAD_DIM]
            k_ref[rows, kv, :] = kn
            v_ref[rows, kv, :] = v_kv
            scores.append((rows, kv, _dot_nt(kn.astype(BF16), q4s[kv]), v_kv.T.astype(BF16)))
    probs = []
    for rows, kv, st, vt_kv in scores:
        p = jnp.exp2(st - jnp.max(st, axis=0, keepdims=True))
        probs.append((rows, kv, p.astype(BF16), jnp.sum(p, axis=0, keepdims=True), vt_kv))
    for rows, kv, p, l, vt_kv in probs:
        ot = _dot(vt_kv, p) / l
        for g in range(GROUP):
            a_ref[rows, _head_cols(kv * GROUP + g)] = (
                ot[:, g * SUB_TILE:(g + 1) * SUB_TILE].T.astype(BF16))


def _front_lat_kernel(x_ref, mod_ref, w1_ref, w2_ref, lng_ref, lnb_ref, win_ref, qg_ref, kg_ref,
                      cos_ref, sin_ref, x1_ref, q_ref, kb_ref, vb_ref, pin_ref, *, alpha):
    qg = qg_ref[...] * Q_SCALE
    kg = kg_ref[...]
    ones_col = _ones_column_block(SUB_TILE)
    tiles = _sub_tiles(x_ref.shape[0])
    x1s = _ffn_rows([x_ref[rows, :] for rows in tiles], mod_ref, 0, w1_ref, w2_ref,
                    lng_ref[...], lnb_ref[...], alpha)
    for rows, x1 in zip(tiles, x1s):
        x1_ref[rows, :] = x1
        proj = _dot(_modulate_bf16(x1, mod_ref[3:4, :], mod_ref[4:5, :]), win_ref[...])
        pin_ref[rows, :] = proj[:, QKV_WIDTH:]
        cos2 = cos_ref[rows, :]
        sin2 = sin_ref[rows, :]
        for hh in range(N_HEADS):
            q_ref[rows, _head_cols(hh)] = _rope(
                _rms_head(proj[:, _head_cols(hh)], qg), cos2, sin2).astype(BF16)
        for kv in range(N_KV_HEADS):
            kn = _rms_head(proj[:, ATTN_WIDTH + kv * HEAD_DIM:ATTN_WIDTH + (kv + 1) * HEAD_DIM], kg)
            kb_ref[rows, _head_cols(kv)] = _rope(kn, cos2, sin2).astype(BF16)
            v0 = ATTN_WIDTH + KV_WIDTH + kv * HEAD_DIM
            vb_ref[rows, kv * V_AUG:kv * V_AUG + HEAD_DIM] = proj[:, v0:v0 + HEAD_DIM].astype(BF16)
            vb_ref[rows, kv * V_AUG + HEAD_DIM:(kv + 1) * V_AUG] = ones_col


def _front_in_specs(tm, mod_row_of_token, w1, w2, w_front):
    return [
        pl.BlockSpec((tm, D_MODEL), lambda i: (i, 0)),
        pl.BlockSpec((None, N_MOD, D_MODEL), lambda i: (mod_row_of_token(i * tm), 0, 0)),
        _resident(w1.shape),
        _resident(w2.shape),
        _resident((1, D_MODEL)),
        _resident((1, D_MODEL)),
        _resident(w_front.shape),
        _resident((1, HEAD_DIM)),
        _resident((1, HEAD_DIM)),
    ]


def _front_ctx(x2d, mod, mod_row_of_token, ln_g, ln_b, q_g, k_g, own, riders, *, alpha):
    n_tok = x2d.shape[0]
    tm = TOKEN_TILE
    n_steps = n_tok // tm
    n_pre = PRE_STEPS
    tile_of = lambda i: jnp.maximum(i - n_pre, 0)
    pre_of = lambda i: jnp.minimum(i, n_pre - 1)
    tok = lambda width: pl.BlockSpec((tm, width), lambda i: (tile_of(i), 0))
    cache_spec = pl.BlockSpec((tm, N_KV_HEADS, HEAD_DIM), lambda i: (tile_of(i), 0, 0))
    cache_shape = jax.ShapeDtypeStruct((n_tok, N_KV_HEADS, HEAD_DIM), F32)

    def chunked(weights, n_chunks, chunk_of):
        in_specs, out_specs, shapes = [], [], []
        for w, col_block, width in weights:
            chunk = w.shape[0] // n_chunks
            assert chunk * n_chunks == w.shape[0] and chunk % BF16_SUBLANES == 0
            in_specs.append(
                pl.BlockSpec((chunk, width), lambda i, cb=col_block: (chunk_of(i), cb)))
            out_specs.append(pl.BlockSpec((chunk, width), lambda i: (chunk_of(i), 0)))
            shapes.append(jax.ShapeDtypeStruct((w.shape[0], width), BF16))
        return in_specs, out_specs, shapes

    own_in, own_out, own_shapes = chunked(own, n_pre, pre_of)
    rider_in, rider_out, rider_shapes = chunked(riders, n_steps, tile_of)
    return pl.pallas_call(
        functools.partial(_front_ctx_kernel, n_own=len(own), n_riders=len(riders), n_pre=n_pre,
                          alpha=alpha),
        grid=(n_pre + n_steps,),
        in_specs=[
            tok(D_MODEL),
            pl.BlockSpec((None, N_MOD, D_MODEL),
                         lambda i: (mod_row_of_token(tile_of(i) * tm), 0, 0)),
            _resident((1, D_MODEL)),
            _resident((1, D_MODEL)),
            _resident((1, HEAD_DIM)),
            _resident((1, HEAD_DIM)),
        ] + own_in + rider_in,
        out_specs=[tok(D_MODEL), tok(ATTN_WIDTH), cache_spec, cache_spec, tok(POOL_WIDTH)]
        + rider_out + own_out,
        out_shape=[
            jax.ShapeDtypeStruct((n_tok, D_MODEL), F32),
            jax.ShapeDtypeStruct((n_tok, ATTN_WIDTH), BF16),
            cache_shape,
            cache_shape,
            jax.ShapeDtypeStruct((n_tok, POOL_WIDTH), F32),
        ] + rider_shapes + own_shapes,
        scratch_shapes=[pltpu.VMEM(s.shape, BF16) for s in own_shapes],
        compiler_params=_compiler_params(1),
        name="front_ctx",
    )(x2d, mod, ln_g.reshape(1, D_MODEL), ln_b.reshape(1, D_MODEL),
      q_g.reshape(1, HEAD_DIM), k_g.reshape(1, HEAD_DIM),
      *[w for w, _, _ in own], *[w for w, _, _ in riders])


def _front_lat(x2d, mod, mod_row_of_token, w1, w2, ln_g, ln_b, w_front, q_g, k_g, cos2, sin2,
               *, n_lat, alpha):
    n_tok = x2d.shape[0]
    tm = TOKEN_TILE
    tiles_per_seq = n_lat // tm
    tok = lambda width: pl.BlockSpec((tm, width), lambda i: (i, 0))
    rope_spec = pl.BlockSpec((tm, HEAD_DIM), lambda i: (i % tiles_per_seq, 0))
    return pl.pallas_call(
        functools.partial(_front_lat_kernel, alpha=alpha),
        grid=(n_tok // tm,),
        in_specs=_front_in_specs(tm, mod_row_of_token, w1, w2, w_front) + [rope_spec, rope_spec],
        out_specs=[tok(D_MODEL), tok(ATTN_WIDTH), tok(KV_WIDTH), tok(N_KV_HEADS * V_AUG),
                   tok(POOL_WIDTH)],
        out_shape=[
            jax.ShapeDtypeStruct((n_tok, D_MODEL), F32),
            jax.ShapeDtypeStruct((n_tok, ATTN_WIDTH), BF16),
            jax.ShapeDtypeStruct((n_tok, KV_WIDTH), BF16),
            jax.ShapeDtypeStruct((n_tok, N_KV_HEADS * V_AUG), BF16),
            jax.ShapeDtypeStruct((n_tok, POOL_WIDTH), F32),
        ],
        compiler_params=_compiler_params(1),
        name="front_lat",
    )(x2d, mod, w1, w2, ln_g.reshape(1, D_MODEL), ln_b.reshape(1, D_MODEL), w_front,
      q_g.reshape(1, HEAD_DIM), k_g.reshape(1, HEAD_DIM), cos2, sin2)


def _attn_lat_kernel(q_ref, kc_ref, vc_ref, kl_ref, vl_ref, o_ref):
    n_lat = kl_ref.shape[0]
    past = kc_ref.shape[0]
    vc = jnp.concatenate([vc_ref[...].astype(BF16), _ones_column_block(past)], axis=1)
    chunks = [(kc_ref[...].astype(BF16), vc)]
    for c0 in range(0, n_lat, KV_CHUNK):
        chunks.append((kl_ref[c0:c0 + KV_CHUNK, :], vl_ref[c0:c0 + KV_CHUNK, :]))
    for r0 in range(0, q_ref.shape[0], Q_SUB):
        rows = slice(r0, r0 + Q_SUB)
        q4 = _stack_heads([q_ref[rows, _head_cols(g)] for g in range(GROUP)])
        m = None
        acc = None
        for k_c, v_c in chunks:
            s = _dot_nt(q4, k_c)
            m_c = jnp.max(s, axis=-1, keepdims=True)
            if m is None:
                m = m_c
                acc = _dot(jnp.exp2(s - m).astype(BF16), v_c)
            else:
                m_new = jnp.maximum(m, m_c)
                acc = jnp.exp2(m - m_new) * acc + _dot(jnp.exp2(s - m_new).astype(BF16), v_c)
                m = m_new
        o = acc[:, :HEAD_DIM] / acc[:, HEAD_DIM:HEAD_DIM + 1]
        for g in range(GROUP):
            o_ref[rows, _head_cols(g)] = o[g * Q_SUB:(g + 1) * Q_SUB].astype(BF16)


def _attn_lat(q, kb, vb, cache_k, cache_v, n_lat):
    n_tok = q.shape[0]
    n_batch = n_tok // n_lat
    past = cache_k.shape[1]
    tq = Q_TILE
    q_tiles = n_lat // tq
    gw = GROUP * HEAD_DIM
    cache_spec = pl.BlockSpec((None, past, HEAD_DIM), lambda b, kv, i: (b, 0, kv))
    return pl.pallas_call(
        _attn_lat_kernel,
        grid=(n_batch, N_KV_HEADS, q_tiles),
        in_specs=[
            pl.BlockSpec((tq, gw), lambda b, kv, i: (b * q_tiles + i, kv)),
            cache_spec,
            cache_spec,
            pl.BlockSpec((n_lat, HEAD_DIM), lambda b, kv, i: (b, kv)),
            pl.BlockSpec((n_lat, V_AUG), lambda b, kv, i: (b, kv)),
        ],
        out_specs=pl.BlockSpec((tq, gw), lambda b, kv, i: (b * q_tiles + i, kv)),
        out_shape=jax.ShapeDtypeStruct((n_tok, ATTN_WIDTH), BF16),
        compiler_params=_compiler_params(3),
        name="attn_lat",
    )(q, cache_k, cache_v, kb, vb)


def _window_sums(e):
    n = e.shape[0]
    back = lambda a, k: pltpu.roll(a, k, axis=0)
    fwd = lambda a, k: pltpu.roll(a, n - k, axis=0)
    c2 = e + back(e, 1)
    c4 = back(c2, 1) + fwd(c2, 1)
    c8 = back(c4, 2) + fwd(c4, 2)
    c16 = back(c8, 4) + fwd(c8, 4)
    return c2, c4, c8, c16


def _back_kernel(x_ref, a_ref, pin_ref, pp_ref, pn_ref, mod_ref, wg_ref, poolw_ref, pscale_ref,
                 wua_ref, wup_ref, wo_ref, lng1_ref, lnb1_ref, w1_ref, w2_ref, lng2_ref, lnb2_ref,
                 o_ref, *, seq_len, alpha):
    tm = x_ref.shape[0]
    tile_start = pl.program_id(0) * tm
    shift = mod_ref[3:4, :]
    scale = mod_ref[4:5, :]
    gate = mod_ref[5:6, :]
    p_ext = jnp.concatenate([pp_ref[...], pin_ref[...], pn_ref[...]], axis=0)

    tiles = _sub_tiles(tm)
    xs = [x_ref[rows, :] for rows in tiles]

    gated = []
    pooled_up = []
    half = D_MODEL // 2
    for rows, x in zip(tiles, xs):
        h = _modulate_bf16(x, shift, scale)
        a_blk = a_ref[rows, :]
        big_operands = [
            (h, wg_ref, slice(0, D_MODEL)),
            (h, wg_ref, slice(D_MODEL, 2 * D_MODEL)),
            (a_blk, wua_ref, slice(0, half)),
            (a_blk, wua_ref, slice(half, D_MODEL)),
        ]
        big = []

        pos = (tile_start + rows.start) % seq_len
        e_all = p_ext[rows.start:rows.start + SUB_TILE + 2 * HALO, :]
        row = lax.broadcasted_iota(jnp.int32, (SUB_TILE + 2 * HALO, 1), 0)
        first_row = jnp.where(pos == 0, HALO, 0)
        end_row = jnp.where(pos + SUB_TILE == seq_len, HALO + SUB_TILE, SUB_TILE + 2 * HALO)
        e_all = jnp.where(jnp.logical_or(row < first_row, row >= end_row), 0.0, e_all)
        t = pos + lax.broadcasted_iota(jnp.int32, (SUB_TILE, 1), 0)
        pgs = []
        for g, w in enumerate(POOL_WINDOWS):
            lhs, w_ref, cols = big_operands[g]
            big.append(_dot(lhs, w_ref[:, cols]))
            sl = slice(g * POOL_GROUP_DIM, (g + 1) * POOL_GROUP_DIM)
            e = e_all[:, sl]
            win = _window_sums(e)[g][HALO:HALO + SUB_TILE]
            lo = jnp.maximum(t - w // 2, 0)
            hi = jnp.minimum(t + w // 2, seq_len)
            inv_cnt = 1.0 / (hi - lo).astype(F32)
            pooled = (win * inv_cnt - e[HALO:HALO + SUB_TILE]).astype(BF16)
            pgs.append(_dot(pooled, poolw_ref[g]))

        ga, gp, au_lo, au_hi = big
        pg = jnp.concatenate(pgs, axis=1)
        gated.append((jax.nn.sigmoid(ga) * jnp.concatenate([au_lo, au_hi], axis=1), gp))
        pooled_up.append(_dot((pg * pscale_ref[...]).astype(BF16), wup_ref[...]))

    merged = [(ga_up + jax.nn.sigmoid(gp) * pu).astype(BF16)
              for (ga_up, gp), pu in zip(gated, pooled_up)]
    mixes = [_dot(m, wo_ref[...]) for m in merged]
    x2s = [_layer_norm(alpha * x + gate * mix, lng1_ref[...], lnb1_ref[...])
           for x, mix in zip(xs, mixes)]

    ys = _ffn_rows(x2s, mod_ref, 6, w1_ref, w2_ref, lng2_ref[...], lnb2_ref[...], alpha)
    for rows, y in zip(tiles, ys):
        o_ref[rows, :] = y


def _back(x2d, attn, pin, mod, mod_row_of_token, w_g, pool_w, pool_scale, w_up_attn, w_up_pool,
          w_out, ln_g1, ln_b1, w1, w2, ln_g2, ln_b2, *, seq_len, alpha):
    n_tok = x2d.shape[0]
    tm = TOKEN_TILE
    halo_per_tile = tm // HALO
    n_halo_blocks = n_tok // HALO
    row_vec = lambda v: v.reshape(1, D_MODEL)
    return pl.pallas_call(
        functools.partial(_back_kernel, seq_len=seq_len, alpha=alpha),
        grid=(n_tok // tm,),
        in_specs=[
            pl.BlockSpec((tm, D_MODEL), lambda i: (i, 0)),
            pl.BlockSpec((tm, ATTN_WIDTH), lambda i: (i, 0)),
            pl.BlockSpec((tm, POOL_WIDTH), lambda i: (i, 0)),
            pl.BlockSpec((HALO, POOL_WIDTH), lambda i: (jnp.maximum(i * halo_per_tile - 1, 0), 0)),
            pl.BlockSpec((HALO, POOL_WIDTH),
                         lambda i: (jnp.minimum((i + 1) * halo_per_tile, n_halo_blocks - 1), 0)),
            pl.BlockSpec((None, N_MOD, D_MODEL), lambda i: (mod_row_of_token(i * tm), 0, 0)),
            _resident(w_g.shape),
            _resident(pool_w.shape),
            _resident((1, POOL_WIDTH)),
            _resident(w_up_attn.shape),
            _resident(w_up_pool.shape),
            _resident(w_out.shape),
            _resident((1, D_MODEL)),
            _resident((1, D_MODEL)),
            _resident(w1.shape),
            _resident(w2.shape),
            _resident((1, D_MODEL)),
            _resident((1, D_MODEL)),
        ],
        out_specs=pl.BlockSpec((tm, D_MODEL), lambda i: (i, 0)),
        out_shape=jax.ShapeDtypeStruct((n_tok, D_MODEL), F32),
        compiler_params=_compiler_params(1),
        name="back",
    )(x2d, attn, pin, pin, pin, mod, w_g, pool_w, pool_scale.reshape(1, POOL_WIDTH),
      w_up_attn, w_up_pool, w_out, row_vec(ln_g1), row_vec(ln_b1), w1, w2,
      row_vec(ln_g2), row_vec(ln_b2))


def _rope_tables(n_tokens):
    t = np.arange(n_tokens)
    n_freq = HEAD_DIM // 4
    inv_freq = ROPE_THETA ** (-np.arange(n_freq, dtype=np.float64) / n_freq)
    ang = np.concatenate([(t // GRID_W)[:, None] * inv_freq, (t % GRID_W)[:, None] * inv_freq],
                         axis=-1)
    cos, sin = np.cos(ang), np.sin(ang)
    return (jnp.asarray(np.concatenate([cos, cos], axis=-1), F32),
            jnp.asarray(np.concatenate([-sin, sin], axis=-1), F32))


def kernel(x_prompt, x_sample, cache_k, cache_v, c, c_ctx, w_mod, b_mod, ln_g, ln_b, ffn1_w1, ffn1_w2, w_in, q_norm_g, k_norm_g, pool_w, pool_scale, w_up_attn, w_up_pool, w_out, ffn2_w1, ffn2_w2):
    batch, seq, _ = x_prompt.shape
    dec_batch, n_lat, _ = x_sample.shape
    depth = w_mod.shape[0]
    alpha = (2.0 * depth) ** 0.25
    assert seq == SUB_TILE and (batch * seq) % TOKEN_TILE == 0
    assert n_lat % TOKEN_TILE == 0 and n_lat % Q_TILE == 0 and n_lat % KV_CHUNK == 0
    assert 1 + dec_batch <= MOD_ROWS

    cos2, sin2 = _rope_tables((n_lat // GRID_W) * GRID_W)
    cond = jnp.zeros((MOD_ROWS, D_MODEL), F32).at[0].set(c_ctx).at[1:1 + dec_batch].set(c)

    ctx_row = lambda t: 0
    lat_row = lambda t: 1 + t // n_lat

    yp = x_prompt.reshape(batch * seq, D_MODEL)
    ys = x_sample.reshape(dec_batch * n_lat, D_MODEL)
    new_k, new_v = [], []
    for l in range(depth):
        pool_w_b = pool_w[l].astype(BF16)
        assert w_in.shape[2] == 2 * FRONT_WIDTH
        own = [(ffn1_w1[l], 0, 2 * D_FF), (ffn1_w2[l], 0, D_MODEL), (w_in[l], 0, FRONT_WIDTH)]
        riders = [(ffn2_w1[l], 0, 2 * D_FF), (ffn2_w2[l], 0, D_MODEL),
                  (w_in[l], 1, FRONT_WIDTH), (w_up_attn[l], 0, D_MODEL),
                  (w_up_pool[l], 0, D_MODEL), (w_out[l], 0, D_MODEL)]

        mod = _modulation_table(cond, w_mod[l], b_mod[l]).reshape(MOD_ROWS, N_MOD, D_MODEL)

        (x1, attn, k_l, v_l, pin, w1b, w2b, w_g, w_ua, w_upl, w_o,
         w1a, w2a, w_front) = _front_ctx(
            yp, mod, ctx_row, ln_g[l, 0], ln_b[l, 0], q_norm_g[l], k_norm_g[l], own, riders,
            alpha=alpha)
        back = functools.partial(
            _back, w_g=w_g, pool_w=pool_w_b, pool_scale=pool_scale[l], w_up_attn=w_ua,
            w_up_pool=w_upl, w_out=w_o, ln_g1=ln_g[l, 1], ln_b1=ln_b[l, 1], w1=w1b, w2=w2b,
            ln_g2=ln_g[l, 2], ln_b2=ln_b[l, 2], alpha=alpha)
        yp = back(x1, attn, pin, mod, ctx_row, seq_len=seq)
        new_k.append(k_l.reshape(batch, seq, N_KV_HEADS, HEAD_DIM))
        new_v.append(v_l.reshape(batch, seq, N_KV_HEADS, HEAD_DIM))

        x1, q, kb, vb, pin = _front_lat(
            ys, mod, lat_row, w1a, w2a, ln_g[l, 0], ln_b[l, 0], w_front,
            q_norm_g[l], k_norm_g[l], cos2, sin2, n_lat=n_lat, alpha=alpha)
        past = cache_k.shape[2]
        attn = _attn_lat(q, kb, vb,
                         cache_k[:, l].reshape(dec_batch, past, KV_WIDTH),
                         cache_v[:, l].reshape(dec_batch, past, KV_WIDTH), n_lat)
        ys = back(x1, attn, pin, mod, lat_row, seq_len=n_lat)

    return (yp.reshape(batch, seq, D_MODEL),
            ys.reshape(dec_batch, n_lat, D_MODEL),
            jnp.stack(new_k, axis=1),
            jnp.stack(new_v, axis=1))
```

```python
import functools
import math

import numpy as np

import jax
import jax.numpy as jnp
from jax import lax
from jax.experimental import pallas as pl
from jax.experimental.pallas import tpu as pltpu

D_MODEL = 1024
GRID_W = 64
N_HEADS = 8
N_KV_HEADS = 2
HEAD_DIM = 128
GROUP = N_HEADS // N_KV_HEADS
ATTN_WIDTH = N_HEADS * HEAD_DIM
KV_WIDTH = N_KV_HEADS * HEAD_DIM
POOL_WINDOWS = (2, 4, 8, 16)
N_POOL_GROUPS = 4
POOL_GROUP_DIM = 128
POOL_WIDTH = N_POOL_GROUPS * POOL_GROUP_DIM
QKV_WIDTH = ATTN_WIDTH + 2 * KV_WIDTH
FRONT_WIDTH = QKV_WIDTH + POOL_WIDTH
D_FF = 2816
N_MOD = 9
ROPE_THETA = 10000.0
LN_EPS = 1e-6
RMS_EPS = 1e-6

SUBLANES = 8
LANES = 128
MXU_DIM = 256
VMEM_LIMIT_BYTES = 58 * 1024 * 1024

HALO = SUBLANES
MOD_ROWS = SUBLANES
MOD_BLOCK = 2304
TOKEN_TILE = 512
SUB_TILE = 256
BF16_SUBLANES = 2 * SUBLANES
PRE_STEPS = 16
Q_TILE = 512
Q_SUB = 256
KV_CHUNK = 256
FF_CHUNKS = ((0, 768), (768, 1024), (1792, 1024))
Q_SCALE = HEAD_DIM ** -0.5 * math.log2(math.e)
V_AUG = MXU_DIM

BF16 = jnp.bfloat16
F32 = jnp.float32


def _dot(a, b):
    return jnp.dot(a, b, preferred_element_type=F32)


def _dot_nt(a, b):
    return lax.dot_general(a, b, (((1,), (1,)), ((), ())), preferred_element_type=F32)


def _layer_norm(x, g, b):
    mu = jnp.mean(x, axis=-1, keepdims=True)
    xc = x - mu
    var = jnp.mean(xc * xc, axis=-1, keepdims=True)
    return xc * lax.rsqrt(var + LN_EPS) * g + b


def _modulate_bf16(x, shift, scale):
    return (x * (1.0 + scale) + shift).astype(BF16)


def _resident(shape):
    zeros = (0,) * len(shape)
    return pl.BlockSpec(shape, lambda *_: zeros, pipeline_mode=pl.Buffered(1))


def _compiler_params(n_axes):
    return pltpu.CompilerParams(
        dimension_semantics=("arbitrary",) * n_axes,
        vmem_limit_bytes=VMEM_LIMIT_BYTES,
    )


def _mod_kernel(cond_ref, w_ref, b_ref, o_ref):
    c = cond_ref[...]
    s = (c * jax.nn.sigmoid(c)).astype(BF16)
    o_ref[...] = _dot(s, w_ref[...].astype(BF16)) + b_ref[...]


def _modulation_table(cond, w_mod, b_mod):
    n_out = w_mod.shape[1]
    bn = MOD_BLOCK
    return pl.pallas_call(
        _mod_kernel,
        grid=(n_out // bn,),
        in_specs=[
            pl.BlockSpec((MOD_ROWS, D_MODEL), lambda j: (0, 0)),
            pl.BlockSpec((D_MODEL, bn), lambda j: (0, j)),
            pl.BlockSpec((1, bn), lambda j: (0, j)),
        ],
        out_specs=pl.BlockSpec((MOD_ROWS, bn), lambda j: (0, j)),
        out_shape=jax.ShapeDtypeStruct((MOD_ROWS, n_out), F32),
        compiler_params=_compiler_params(1),
        name="modulation",
    )(cond, w_mod, b_mod.reshape(1, n_out))


def _ffn_rows(xs, mod_ref, mod_base, w1_ref, w2_ref, ln_g, ln_b, alpha):
    shift = mod_ref[mod_base:mod_base + 1, :]
    scale = mod_ref[mod_base + 1:mod_base + 2, :]
    gate = mod_ref[mod_base + 2:mod_base + 3, :]
    hs = [_modulate_bf16(x, shift, scale) for x in xs]
    fs = [None] * len(xs)
    for start, width in FF_CHUNKS:
        us = []
        for h in hs:
            a = _dot(h, w1_ref[:, start:start + width])
            b = _dot(h, w1_ref[:, D_FF + start:D_FF + start + width])
            us.append((a * jax.nn.sigmoid(a) * b).astype(BF16))
        for i, u in enumerate(us):
            part = _dot(u, w2_ref[start:start + width, :])
            fs[i] = part if fs[i] is None else fs[i] + part
    return [_layer_norm(alpha * x + (0.5 * gate) * f, ln_g, ln_b) for x, f in zip(xs, fs)]


def _sub_tiles(n_rows):
    return [slice(r0, r0 + SUB_TILE) for r0 in range(0, n_rows, SUB_TILE)]


def _rms_head(xh, g):
    ms = jnp.mean(xh * xh, axis=-1, keepdims=True)
    return xh * lax.rsqrt(ms + RMS_EPS) * g


def _rope(xh, cos2, sin2):
    return xh * cos2 + pltpu.roll(xh, HEAD_DIM // 2, axis=1) * sin2


def _ones_column_block(rows):
    col = lax.broadcasted_iota(jnp.int32, (rows, V_AUG - HEAD_DIM), 1)
    return jnp.where(col == 0, 1.0, 0.0).astype(BF16)


def _stack_heads(heads):
    return jnp.concatenate(heads, axis=0)


def _head_cols(hh):
    return slice(hh * HEAD_DIM, (hh + 1) * HEAD_DIM)


def _front_ctx_kernel(x_ref, mod_ref, lng_ref, lnb_ref, qg_ref, kg_ref, *refs,
                      n_own, n_riders, n_pre, alpha):
    own_in = refs[:n_own]
    rider_in = refs[n_own:n_own + n_riders]
    outs = refs[n_own + n_riders:]
    x1_ref, a_ref, k_ref, v_ref, pin_ref = outs[:5]
    rider_out = outs[5:5 + n_riders]
    own_out = outs[5 + n_riders:5 + n_riders + n_own]
    w1_ref, w2_ref, win_ref = outs[5 + n_riders + n_own:]
    step = pl.program_id(0)

    @pl.when(step < n_pre)
    def _load_own_weights():
        for src, dst, whole in zip(own_in, own_out, (w1_ref, w2_ref, win_ref)):
            chunk = src.shape[0]
            val = src[...].astype(BF16)
            dst[...] = val
            whole[pl.ds(pl.multiple_of(step * chunk, BF16_SUBLANES), chunk), :] = val

    @pl.when(step >= n_pre)
    def _token_step():
        for src, dst in zip(rider_in, rider_out):
            dst[...] = src[...].astype(BF16)
        _front_ctx_body(x_ref, mod_ref, w1_ref, w2_ref, lng_ref, lnb_ref, win_ref, qg_ref, kg_ref,
                        x1_ref, a_ref, k_ref, v_ref, pin_ref, alpha)


def _front_ctx_body(x_ref, mod_ref, w1_ref, w2_ref, lng_ref, lnb_ref, win_ref, qg_ref, kg_ref,
                    x1_ref, a_ref, k_ref, v_ref, pin_ref, alpha):
    qg = qg_ref[...] * Q_SCALE
    kg = kg_ref[...]
    tiles = _sub_tiles(x_ref.shape[0])
    x1s = _ffn_rows([x_ref[rows, :] for rows in tiles], mod_ref, 0, w1_ref, w2_ref,
                    lng_ref[...], lnb_ref[...], alpha)
    gw = GROUP * HEAD_DIM
    scores = []
    for rows, x1 in zip(tiles, x1s):
        x1_ref[rows, :] = x1
        h = _modulate_bf16(x1, mod_ref[3:4, :], mod_ref[4:5, :])
        q_cols = [_dot(h, win_ref[:, :gw]), _dot(h, win_ref[:, gw:ATTN_WIDTH])]
        q4s = [_stack_heads([_rms_head(q_cols[0][:, _head_cols(g)], qg).astype(BF16)
                             for g in range(GROUP)])]
        kv_cols = _dot(h, win_ref[:, ATTN_WIDTH:QKV_WIDTH])
        q4s.append(_stack_heads([_rms_head(q_cols[1][:, _head_cols(g)], qg).astype(BF16)
                                 for g in range(GROUP)]))
        pin_ref[rows, :] = _dot(h, win_ref[:, QKV_WIDTH:])
        for kv in range(N_KV_HEADS):
            kn = _rms_head(kv_cols[:, _head_cols(kv)], kg)
            v_kv = kv_cols[:, KV_WIDTH + kv * HEAD_DIM:KV_WIDTH + (kv + 1) * HEAD_DIM]
            head_rows = pl.ds(N_KV_HEADS * rows.start + kv, SUB_TILE, stride=N_KV_HEADS)
            k_ref[head_rows, :] = kn
            v_ref[head_rows, :] = v_kv
            scores.append((rows, kv, _dot_nt(kn.astype(BF16), q4s[kv]), v_kv.T.astype(BF16)))
    probs = []
    for rows, kv, st, vt_kv in scores:
        p = jnp.exp2(st - jnp.max(st, axis=0, keepdims=True))
        probs.append((rows, kv, p.astype(BF16), jnp.sum(p, axis=0, keepdims=True), vt_kv))
    for rows, kv, p, l, vt_kv in probs:
        ot = _dot(vt_kv, p) / l
        for g in range(GROUP):
            a_ref[rows, _head_cols(kv * GROUP + g)] = (
                ot[:, g * SUB_TILE:(g + 1) * SUB_TILE].T.astype(BF16))


def _front_lat_kernel(x_ref, mod_ref, w1_ref, w2_ref, lng_ref, lnb_ref, win_ref, qg_ref, kg_ref,
                      cos_ref, sin_ref, x1_ref, q_ref, kb_ref, vb_ref, pin_ref, *, alpha):
    qg = qg_ref[...] * Q_SCALE
    kg = kg_ref[...]
    ones_col = _ones_column_block(SUB_TILE)
    tiles = _sub_tiles(x_ref.shape[0])
    x1s = _ffn_rows([x_ref[rows, :] for rows in tiles], mod_ref, 0, w1_ref, w2_ref,
                    lng_ref[...], lnb_ref[...], alpha)
    for rows, x1 in zip(tiles, x1s):
        x1_ref[rows, :] = x1
        proj = _dot(_modulate_bf16(x1, mod_ref[3:4, :], mod_ref[4:5, :]), win_ref[...])
        pin_ref[rows, :] = proj[:, QKV_WIDTH:]
        cos2 = cos_ref[rows, :]
        sin2 = sin_ref[rows, :]
        for hh in range(N_HEADS):
            q_ref[rows, _head_cols(hh)] = _rope(
                _rms_head(proj[:, _head_cols(hh)], qg), cos2, sin2).astype(BF16)
        for kv in range(N_KV_HEADS):
            kn = _rms_head(proj[:, ATTN_WIDTH + kv * HEAD_DIM:ATTN_WIDTH + (kv + 1) * HEAD_DIM], kg)
            kb_ref[rows, _head_cols(kv)] = _rope(kn, cos2, sin2).astype(BF16)
            v0 = ATTN_WIDTH + KV_WIDTH + kv * HEAD_DIM
            vb_ref[rows, kv * V_AUG:kv * V_AUG + HEAD_DIM] = proj[:, v0:v0 + HEAD_DIM].astype(BF16)
            vb_ref[rows, kv * V_AUG + HEAD_DIM:(kv + 1) * V_AUG] = ones_col


def _front_in_specs(tm, mod_row_of_token, w1, w2, w_front):
    return [
        pl.BlockSpec((tm, D_MODEL), lambda i: (i, 0)),
        pl.BlockSpec((None, N_MOD, D_MODEL), lambda i: (mod_row_of_token(i * tm), 0, 0)),
        _resident(w1.shape),
        _resident(w2.shape),
        _resident((1, D_MODEL)),
        _resident((1, D_MODEL)),
        _resident(w_front.shape),
        _resident((1, HEAD_DIM)),
        _resident((1, HEAD_DIM)),
    ]


def _front_ctx(x2d, mod, mod_row_of_token, ln_g, ln_b, q_g, k_g, own, riders, *, alpha):
    n_tok = x2d.shape[0]
    tm = TOKEN_TILE
    n_steps = n_tok // tm
    n_pre = PRE_STEPS
    tile_of = lambda i: jnp.maximum(i - n_pre, 0)
    pre_of = lambda i: jnp.minimum(i, n_pre - 1)
    tok = lambda width: pl.BlockSpec((tm, width), lambda i: (tile_of(i), 0))
    cache_spec = pl.BlockSpec((tm * N_KV_HEADS, HEAD_DIM), lambda i: (tile_of(i), 0))
    cache_shape = jax.ShapeDtypeStruct((n_tok * N_KV_HEADS, HEAD_DIM), F32)

    def chunked(weights, n_chunks, chunk_of):
        in_specs, out_specs, shapes = [], [], []
        for w, col_block, width in weights:
            chunk = w.shape[0] // n_chunks
            assert chunk * n_chunks == w.shape[0] and chunk % BF16_SUBLANES == 0
            in_specs.append(
                pl.BlockSpec((chunk, width), lambda i, cb=col_block: (chunk_of(i), cb)))
            out_specs.append(pl.BlockSpec((chunk, width), lambda i: (chunk_of(i), 0)))
            shapes.append(jax.ShapeDtypeStruct((w.shape[0], width), BF16))
        return in_specs, out_specs, shapes

    own_in, own_out, own_shapes = chunked(own, n_pre, pre_of)
    rider_in, rider_out, rider_shapes = chunked(riders, n_steps, tile_of)
    return pl.pallas_call(
        functools.partial(_front_ctx_kernel, n_own=len(own), n_riders=len(riders), n_pre=n_pre,
                          alpha=alpha),
        grid=(n_pre + n_steps,),
        in_specs=[
            tok(D_MODEL),
            pl.BlockSpec((None, N_MOD, D_MODEL),
                         lambda i: (mod_row_of_token(tile_of(i) * tm), 0, 0)),
            _resident((1, D_MODEL)),
            _resident((1, D_MODEL)),
            _resident((1, HEAD_DIM)),
            _resident((1, HEAD_DIM)),
        ] + own_in + rider_in,
        out_specs=[tok(D_MODEL), tok(ATTN_WIDTH), cache_spec, cache_spec, tok(POOL_WIDTH)]
        + rider_out + own_out,
        out_shape=[
            jax.ShapeDtypeStruct((n_tok, D_MODEL), F32),
            jax.ShapeDtypeStruct((n_tok, ATTN_WIDTH), BF16),
            cache_shape,
            cache_shape,
            jax.ShapeDtypeStruct((n_tok, POOL_WIDTH), F32),
        ] + rider_shapes + own_shapes,
        scratch_shapes=[pltpu.VMEM(s.shape, BF16) for s in own_shapes],
        compiler_params=_compiler_params(1),
        name="front_ctx",
    )(x2d, mod, ln_g.reshape(1, D_MODEL), ln_b.reshape(1, D_MODEL),
      q_g.reshape(1, HEAD_DIM), k_g.reshape(1, HEAD_DIM),
      *[w for w, _, _ in own], *[w for w, _, _ in riders])


def _front_lat(x2d, mod, mod_row_of_token, w1, w2, ln_g, ln_b, w_front, q_g, k_g, cos2, sin2,
               *, n_lat, alpha):
    n_tok = x2d.shape[0]
    tm = TOKEN_TILE
    tiles_per_seq = n_lat // tm
    tok = lambda width: pl.BlockSpec((tm, width), lambda i: (i, 0))
    rope_spec = pl.BlockSpec((tm, HEAD_DIM), lambda i: (i % tiles_per_seq, 0))
    return pl.pallas_call(
        functools.partial(_front_lat_kernel, alpha=alpha),
        grid=(n_tok // tm,),
        in_specs=_front_in_specs(tm, mod_row_of_token, w1, w2, w_front) + [rope_spec, rope_spec],
        out_specs=[tok(D_MODEL), tok(ATTN_WIDTH), tok(KV_WIDTH), tok(N_KV_HEADS * V_AUG),
                   tok(POOL_WIDTH)],
        out_shape=[
            jax.ShapeDtypeStruct((n_tok, D_MODEL), F32),
            jax.ShapeDtypeStruct((n_tok, ATTN_WIDTH), BF16),
            jax.ShapeDtypeStruct((n_tok, KV_WIDTH), BF16),
            jax.ShapeDtypeStruct((n_tok, N_KV_HEADS * V_AUG), BF16),
            jax.ShapeDtypeStruct((n_tok, POOL_WIDTH), F32),
        ],
        compiler_params=_compiler_params(1),
        name="front_lat",
    )(x2d, mod, w1, w2, ln_g.reshape(1, D_MODEL), ln_b.reshape(1, D_MODEL), w_front,
      q_g.reshape(1, HEAD_DIM), k_g.reshape(1, HEAD_DIM), cos2, sin2)


def _attn_lat_kernel(q_ref, kc_ref, vc_ref, kl_ref, vl_ref, o_ref):
    n_lat = kl_ref.shape[0]
    past = kc_ref.shape[0]
    vc = jnp.concatenate([vc_ref[...].astype(BF16), _ones_column_block(past)], axis=1)
    chunks = [(kc_ref[...].astype(BF16), vc)]
    for c0 in range(0, n_lat, KV_CHUNK):
        chunks.append((kl_ref[c0:c0 + KV_CHUNK, :], vl_ref[c0:c0 + KV_CHUNK, :]))
    for r0 in range(0, q_ref.shape[0], Q_SUB):
        rows = slice(r0, r0 + Q_SUB)
        q4 = _stack_heads([q_ref[rows, _head_cols(g)] for g in range(GROUP)])
        m = None
        acc = None
        for k_c, v_c in chunks:
            s = _dot_nt(q4, k_c)
            m_c = jnp.max(s, axis=-1, keepdims=True)
            if m is None:
                m = m_c
                acc = _dot(jnp.exp2(s - m).astype(BF16), v_c)
            else:
                m_new = jnp.maximum(m, m_c)
                acc = jnp.exp2(m - m_new) * acc + _dot(jnp.exp2(s - m_new).astype(BF16), v_c)
                m = m_new
        o = acc[:, :HEAD_DIM] / acc[:, HEAD_DIM:HEAD_DIM + 1]
        for g in range(GROUP):
            o_ref[rows, _head_cols(g)] = o[g * Q_SUB:(g + 1) * Q_SUB].astype(BF16)


def _attn_lat(q, kb, vb, cache_k, cache_v, n_lat):
    n_tok = q.shape[0]
    n_batch = n_tok // n_lat
    past = cache_k.shape[1]
    tq = Q_TILE
    q_tiles = n_lat // tq
    gw = GROUP * HEAD_DIM
    cache_spec = pl.BlockSpec((None, past, HEAD_DIM), lambda b, kv, i: (b, 0, kv))
    return pl.pallas_call(
        _attn_lat_kernel,
        grid=(n_batch, N_KV_HEADS, q_tiles),
        in_specs=[
            pl.BlockSpec((tq, gw), lambda b, kv, i: (b * q_tiles + i, kv)),
            cache_spec,
            cache_spec,
            pl.BlockSpec((n_lat, HEAD_DIM), lambda b, kv, i: (b, kv)),
            pl.BlockSpec((n_lat, V_AUG), lambda b, kv, i: (b, kv)),
        ],
        out_specs=pl.BlockSpec((tq, gw), lambda b, kv, i: (b * q_tiles + i, kv)),
        out_shape=jax.ShapeDtypeStruct((n_tok, ATTN_WIDTH), BF16),
        compiler_params=_compiler_params(3),
        name="attn_lat",
    )(q, cache_k, cache_v, kb, vb)


def _window_sums(e):
    n = e.shape[0]
    back = lambda a, k: pltpu.roll(a, k, axis=0)
    fwd = lambda a, k: pltpu.roll(a, n - k, axis=0)
    c2 = e + back(e, 1)
    c4 = back(c2, 1) + fwd(c2, 1)
    c8 = back(c4, 2) + fwd(c4, 2)
    c16 = back(c8, 4) + fwd(c8, 4)
    return c2, c4, c8, c16


def _back_kernel(x_ref, a_ref, pin_ref, pp_ref, pn_ref, mod_ref, wg_ref, poolw_ref, pscale_ref,
                 wua_ref, wup_ref, wo_ref, lng1_ref, lnb1_ref, w1_ref, w2_ref, lng2_ref, lnb2_ref,
                 o_ref, *, seq_len, alpha):
    tm = x_ref.shape[0]
    tile_start = pl.program_id(0) * tm
    shift = mod_ref[3:4, :]
    scale = mod_ref[4:5, :]
    gate = mod_ref[5:6, :]
    p_ext = jnp.concatenate([pp_ref[...], pin_ref[...], pn_ref[...]], axis=0)

    tiles = _sub_tiles(tm)
    xs = [x_ref[rows, :] for rows in tiles]

    gated = []
    pooled_up = []
    half = D_MODEL // 2
    for rows, x in zip(tiles, xs):
        h = _modulate_bf16(x, shift, scale)
        a_blk = a_ref[rows, :]
        big_operands = [
            (h, wg_ref, slice(0, D_MODEL)),
            (h, wg_ref, slice(D_MODEL, 2 * D_MODEL)),
            (a_blk, wua_ref, slice(0, half)),
            (a_blk, wua_ref, slice(half, D_MODEL)),
        ]
        big = []

        pos = (tile_start + rows.start) % seq_len
        e_all = p_ext[rows.start:rows.start + SUB_TILE + 2 * HALO, :]
        row = lax.broadcasted_iota(jnp.int32, (SUB_TILE + 2 * HALO, 1), 0)
        first_row = jnp.where(pos == 0, HALO, 0)
        end_row = jnp.where(pos + SUB_TILE == seq_len, HALO + SUB_TILE, SUB_TILE + 2 * HALO)
        e_all = jnp.where(jnp.logical_or(row < first_row, row >= end_row), 0.0, e_all)
        t = pos + lax.broadcasted_iota(jnp.int32, (SUB_TILE, 1), 0)
        pgs = []
        for g, w in enumerate(POOL_WINDOWS):
            lhs, w_ref, cols = big_operands[g]
            big.append(_dot(lhs, w_ref[:, cols]))
            sl = slice(g * POOL_GROUP_DIM, (g + 1) * POOL_GROUP_DIM)
            e = e_all[:, sl]
            win = _window_sums(e)[g][HALO:HALO + SUB_TILE]
            lo = jnp.maximum(t - w // 2, 0)
            hi = jnp.minimum(t + w // 2, seq_len)
            inv_cnt = 1.0 / (hi - lo).astype(F32)
            pooled = (win * inv_cnt - e[HALO:HALO + SUB_TILE]).astype(BF16)
            pgs.append(_dot(pooled, poolw_ref[g]))

        ga, gp, au_lo, au_hi = big
        pg = jnp.concatenate(pgs, axis=1)
        gated.append((jax.nn.sigmoid(ga) * jnp.concatenate([au_lo, au_hi], axis=1), gp))
        pooled_up.append(_dot((pg * pscale_ref[...]).astype(BF16), wup_ref[...]))

    merged = [(ga_up + jax.nn.sigmoid(gp) * pu).astype(BF16)
              for (ga_up, gp), pu in zip(gated, pooled_up)]
    mixes = [_dot(m, wo_ref[...]) for m in merged]
    x2s = [_layer_norm(alpha * x + gate * mix, lng1_ref[...], lnb1_ref[...])
           for x, mix in zip(xs, mixes)]

    ys = _ffn_rows(x2s, mod_ref, 6, w1_ref, w2_ref, lng2_ref[...], lnb2_ref[...], alpha)
    for rows, y in zip(tiles, ys):
        o_ref[rows, :] = y


def _back(x2d, attn, pin, mod, mod_row_of_token, w_g, pool_w, pool_scale, w_up_attn, w_up_pool,
          w_out, ln_g1, ln_b1, w1, w2, ln_g2, ln_b2, *, seq_len, alpha):
    n_tok = x2d.shape[0]
    tm = TOKEN_TILE
    halo_per_tile = tm // HALO
    n_halo_blocks = n_tok // HALO
    row_vec = lambda v: v.reshape(1, D_MODEL)
    return pl.pallas_call(
        functools.partial(_back_kernel, seq_len=seq_len, alpha=alpha),
        grid=(n_tok // tm,),
        in_specs=[
            pl.BlockSpec((tm, D_MODEL), lambda i: (i, 0)),
            pl.BlockSpec((tm, ATTN_WIDTH), lambda i: (i, 0)),
            pl.BlockSpec((tm, POOL_WIDTH), lambda i: (i, 0)),
            pl.BlockSpec((HALO, POOL_WIDTH), lambda i: (jnp.maximum(i * halo_per_tile - 1, 0), 0)),
            pl.BlockSpec((HALO, POOL_WIDTH),
                         lambda i: (jnp.minimum((i + 1) * halo_per_tile, n_halo_blocks - 1), 0)),
            pl.BlockSpec((None, N_MOD, D_MODEL), lambda i: (mod_row_of_token(i * tm), 0, 0)),
            _resident(w_g.shape),
            _resident(pool_w.shape),
            _resident((1, POOL_WIDTH)),
            _resident(w_up_attn.shape),
            _resident(w_up_pool.shape),
            _resident(w_out.shape),
            _resident((1, D_MODEL)),
            _resident((1, D_MODEL)),
            _resident(w1.shape),
            _resident(w2.shape),
            _resident((1, D_MODEL)),
            _resident((1, D_MODEL)),
        ],
        out_specs=pl.BlockSpec((tm, D_MODEL), lambda i: (i, 0)),
        out_shape=jax.ShapeDtypeStruct((n_tok, D_MODEL), F32),
        compiler_params=_compiler_params(1),
        name="back",
    )(x2d, attn, pin, pin, pin, mod, w_g, pool_w, pool_scale.reshape(1, POOL_WIDTH),
      w_up_attn, w_up_pool, w_out, row_vec(ln_g1), row_vec(ln_b1), w1, w2,
      row_vec(ln_g2), row_vec(ln_b2))


def _rope_tables(n_tokens):
    t = np.arange(n_tokens)
    n_freq = HEAD_DIM // 4
    inv_freq = ROPE_THETA ** (-np.arange(n_freq, dtype=np.float64) / n_freq)
    ang = np.concatenate([(t // GRID_W)[:, None] * inv_freq, (t % GRID_W)[:, None] * inv_freq],
                         axis=-1)
    cos, sin = np.cos(ang), np.sin(ang)
    return (jnp.asarray(np.concatenate([cos, cos], axis=-1), F32),
            jnp.asarray(np.concatenate([-sin, sin], axis=-1), F32))


def kernel(x_prompt, x_sample, cache_k, cache_v, c, c_ctx, w_mod, b_mod, ln_g, ln_b, ffn1_w1, ffn1_w2, w_in, q_norm_g, k_norm_g, pool_w, pool_scale, w_up_attn, w_up_pool, w_out, ffn2_w1, ffn2_w2):
    batch, seq, _ = x_prompt.shape
    dec_batch, n_lat, _ = x_sample.shape
    depth = w_mod.shape[0]
    alpha = (2.0 * depth) ** 0.25
    assert seq == SUB_TILE and (batch * seq) % TOKEN_TILE == 0
    assert n_lat % TOKEN_TILE == 0 and n_lat % Q_TILE == 0 and n_lat % KV_CHUNK == 0
    assert 1 + dec_batch <= MOD_ROWS

    cos2, sin2 = _rope_tables((n_lat // GRID_W) * GRID_W)
    cond = jnp.zeros((MOD_ROWS, D_MODEL), F32).at[0].set(c_ctx).at[1:1 + dec_batch].set(c)

    ctx_row = lambda t: 0
    lat_row = lambda t: 1 + t // n_lat

    yp = x_prompt.reshape(batch * seq, D_MODEL)
    ys = x_sample.reshape(dec_batch * n_lat, D_MODEL)
    new_k, new_v = [], []
    for l in range(depth):
        pool_w_b = pool_w[l].astype(BF16)
        assert w_in.shape[2] == 2 * FRONT_WIDTH
        own = [(ffn1_w1[l], 0, 2 * D_FF), (ffn1_w2[l], 0, D_MODEL), (w_in[l], 0, FRONT_WIDTH)]
        riders = [(ffn2_w1[l], 0, 2 * D_FF), (ffn2_w2[l], 0, D_MODEL),
                  (w_in[l], 1, FRONT_WIDTH), (w_up_attn[l], 0, D_MODEL),
                  (w_up_pool[l], 0, D_MODEL), (w_out[l], 0, D_MODEL)]

        mod = _modulation_table(cond, w_mod[l], b_mod[l]).reshape(MOD_ROWS, N_MOD, D_MODEL)

        (x1, attn, k_l, v_l, pin, w1b, w2b, w_g, w_ua, w_upl, w_o,
         w1a, w2a, w_front) = _front_ctx(
            yp, mod, ctx_row, ln_g[l, 0], ln_b[l, 0], q_norm_g[l], k_norm_g[l], own, riders,
            alpha=alpha)
        back = functools.partial(
            _back, w_g=w_g, pool_w=pool_w_b, pool_scale=pool_scale[l], w_up_attn=w_ua,
            w_up_pool=w_upl, w_out=w_o, ln_g1=ln_g[l, 1], ln_b1=ln_b[l, 1], w1=w1b, w2=w2b,
            ln_g2=ln_g[l, 2], ln_b2=ln_b[l, 2], alpha=alpha)
        yp = back(x1, attn, pin, mod, ctx_row, seq_len=seq)
        new_k.append(k_l.reshape(batch, seq, N_KV_HEADS, HEAD_DIM))
        new_v.append(v_l.reshape(batch, seq, N_KV_HEADS, HEAD_DIM))

        x1, q, kb, vb, pin = _front_lat(
            ys, mod, lat_row, w1a, w2a, ln_g[l, 0], ln_b[l, 0], w_front,
            q_norm_g[l], k_norm_g[l], cos2, sin2, n_lat=n_lat, alpha=alpha)
        past = cache_k.shape[2]
        attn = _attn_lat(q, kb, vb,
                         cache_k[:, l].reshape(dec_batch, past, KV_WIDTH),
                         cache_v[:, l].reshape(dec_batch, past, KV_WIDTH), n_lat)
        ys = back(x1, attn, pin, mod, lat_row, seq_len=n_lat)

    return (yp.reshape(batch, seq, D_MODEL),
            ys.reshape(dec_batch, n_lat, D_MODEL),
            jnp.stack(new_k, axis=1),
            jnp.stack(new_v, axis=1))
```

```python
import functools
import math

import numpy as np

import jax
import jax.numpy as jnp
from jax import lax
from jax.experimental import pallas as pl
from jax.experimental.pallas import tpu as pltpu

D_MODEL = 1024
GRID_W = 64
N_HEADS = 8
N_KV_HEADS = 2
HEAD_DIM = 128
GROUP = N_HEADS // N_KV_HEADS
ATTN_WIDTH = N_HEADS * HEAD_DIM
KV_WIDTH = N_KV_HEADS * HEAD_DIM
POOL_WINDOWS = (2, 4, 8, 16)
N_POOL_GROUPS = 4
POOL_GROUP_DIM = 128
POOL_WIDTH = N_POOL_GROUPS * POOL_GROUP_DIM
QKV_WIDTH = ATTN_WIDTH + 2 * KV_WIDTH
FRONT_WIDTH = QKV_WIDTH + POOL_WIDTH
D_FF = 2816
N_MOD = 9
ROPE_THETA = 10000.0
LN_EPS = 1e-6
RMS_EPS = 1e-6

SUBLANES = 8
LANES = 128
MXU_DIM = 256
VMEM_LIMIT_BYTES = 58 * 1024 * 1024

HALO = SUBLANES
MOD_ROWS = SUBLANES
MOD_BLOCK = 2304
TOKEN_TILE = 512
SUB_TILE = 256
BF16_SUBLANES = 2 * SUBLANES
PRE_STEPS = 16
Q_TILE = 512
Q_SUB = 256
KV_CHUNK = 256
FF_CHUNKS = ((0, 768), (768, 1024), (1792, 1024))
Q_SCALE = HEAD_DIM ** -0.5 * math.log2(math.e)
V_AUG = MXU_DIM

BF16 = jnp.bfloat16
F32 = jnp.float32


def _dot(a, b):
    return jnp.dot(a, b, preferred_element_type=F32)


def _dot_nt(a, b):
    return lax.dot_general(a, b, (((1,), (1,)), ((), ())), preferred_element_type=F32)


def _layer_norm(x, g, b):
    mu = jnp.mean(x, axis=-1, keepdims=True)
    xc = x - mu
    var = jnp.mean(xc * xc, axis=-1, keepdims=True)
    return xc * lax.rsqrt(var + LN_EPS) * g + b


def _modulate_bf16(x, shift, scale):
    return (x * (1.0 + scale) + shift).astype(BF16)


def _resident(shape):
    zeros = (0,) * len(shape)
    return pl.BlockSpec(shape, lambda *_: zeros, pipeline_mode=pl.Buffered(1))


def _compiler_params(n_axes):
    return pltpu.CompilerParams(
        dimension_semantics=("arbitrary",) * n_axes,
        vmem_limit_bytes=VMEM_LIMIT_BYTES,
    )


def _mod_kernel(cond_ref, w_ref, b_ref, o_ref):
    c = cond_ref[...]
    s = (c * jax.nn.sigmoid(c)).astype(BF16)
    o_ref[...] = _dot(s, w_ref[...].astype(BF16)) + b_ref[...]


def _modulation_table(cond, w_mod, b_mod):
    n_out = w_mod.shape[1]
    bn = MOD_BLOCK
    return pl.pallas_call(
        _mod_kernel,
        grid=(n_out // bn,),
        in_specs=[
            pl.BlockSpec((MOD_ROWS, D_MODEL), lambda j: (0, 0)),
            pl.BlockSpec((D_MODEL, bn), lambda j: (0, j)),
            pl.BlockSpec((1, bn), lambda j: (0, j)),
        ],
        out_specs=pl.BlockSpec((MOD_ROWS, bn), lambda j: (0, j)),
        out_shape=jax.ShapeDtypeStruct((MOD_ROWS, n_out), F32),
        compiler_params=_compiler_params(1),
        name="modulation",
    )(cond, w_mod, b_mod.reshape(1, n_out))


def _ffn_rows(xs, mod_ref, mod_base, w1_ref, w2_ref, ln_g, ln_b, alpha):
    shift = mod_ref[mod_base:mod_base + 1, :]
    scale = mod_ref[mod_base + 1:mod_base + 2, :]
    gate = mod_ref[mod_base + 2:mod_base + 3, :]
    hs = [_modulate_bf16(x, shift, scale) for x in xs]
    fs = [None] * len(xs)
    for start, width in FF_CHUNKS:
        us = []
        for h in hs:
            a = _dot(h, w1_ref[:, start:start + width])
            b = _dot(h, w1_ref[:, D_FF + start:D_FF + start + width])
            us.append((a * jax.nn.sigmoid(a) * b).astype(BF16))
        for i, u in enumerate(us):
            part = _dot(u, w2_ref[start:start + width, :])
            fs[i] = part if fs[i] is None else fs[i] + part
    return [_layer_norm(alpha * x + (0.5 * gate) * f, ln_g, ln_b) for x, f in zip(xs, fs)]


def _sub_tiles(n_rows):
    return [slice(r0, r0 + SUB_TILE) for r0 in range(0, n_rows, SUB_TILE)]


def _rms_head(xh, g):
    ms = jnp.mean(xh * xh, axis=-1, keepdims=True)
    return xh * lax.rsqrt(ms + RMS_EPS) * g


def _rope(xh, cos2, sin2):
    return xh * cos2 + pltpu.roll(xh, HEAD_DIM // 2, axis=1) * sin2


def _ones_column_block(rows):
    col = lax.broadcasted_iota(jnp.int32, (rows, V_AUG - HEAD_DIM), 1)
    return jnp.where(col == 0, 1.0, 0.0).astype(BF16)


def _stack_heads(heads):
    return jnp.concatenate(heads, axis=0)


def _head_cols(hh):
    return slice(hh * HEAD_DIM, (hh + 1) * HEAD_DIM)


def _front_ctx_kernel(x_ref, mod_ref, lng_ref, lnb_ref, qg_ref, kg_ref, *refs,
                      n_own, n_riders, n_pre, alpha):
    own_in = refs[:n_own]
    rider_in = refs[n_own:n_own + n_riders]
    outs = refs[n_own + n_riders:]
    x1_ref, a_ref, k_ref, v_ref, pin_ref = outs[:5]
    rider_out = outs[5:5 + n_riders]
    own_out = outs[5 + n_riders:5 + n_riders + n_own]
    w1_ref, w2_ref, win_ref = outs[5 + n_riders + n_own:]
    step = pl.program_id(0)

    @pl.when(step < n_pre)
    def _load_own_weights():
        for src, dst, whole in zip(own_in, own_out, (w1_ref, w2_ref, win_ref)):
            chunk = src.shape[0]
            val = src[...].astype(BF16)
            dst[...] = val
            whole[pl.ds(pl.multiple_of(step * chunk, BF16_SUBLANES), chunk), :] = val

    @pl.when(step >= n_pre)
    def _token_step():
        for src, dst in zip(rider_in, rider_out):
            dst[...] = src[...].astype(BF16)
        _front_ctx_body(x_ref, mod_ref, w1_ref, w2_ref, lng_ref, lnb_ref, win_ref, qg_ref, kg_ref,
                        x1_ref, a_ref, k_ref, v_ref, pin_ref, alpha)


def _front_ctx_body(x_ref, mod_ref, w1_ref, w2_ref, lng_ref, lnb_ref, win_ref, qg_ref, kg_ref,
                    x1_ref, a_ref, k_ref, v_ref, pin_ref, alpha):
    qg = qg_ref[...] * Q_SCALE
    kg = kg_ref[...]
    tiles = _sub_tiles(x_ref.shape[0])
    x1s = _ffn_rows([x_ref[rows, :] for rows in tiles], mod_ref, 0, w1_ref, w2_ref,
                    lng_ref[...], lnb_ref[...], alpha)
    gw = GROUP * HEAD_DIM
    scores = []
    for rows, x1 in zip(tiles, x1s):
        x1_ref[rows, :] = x1
        h = _modulate_bf16(x1, mod_ref[3:4, :], mod_ref[4:5, :])
        q_cols = [_dot(h, win_ref[:, :gw]), _dot(h, win_ref[:, gw:ATTN_WIDTH])]
        q4s = [_stack_heads([_rms_head(q_cols[0][:, _head_cols(g)], qg).astype(BF16)
                             for g in range(GROUP)])]
        kv_cols = _dot(h, win_ref[:, ATTN_WIDTH:QKV_WIDTH])
        q4s.append(_stack_heads([_rms_head(q_cols[1][:, _head_cols(g)], qg).astype(BF16)
                                 for g in range(GROUP)]))
        pin_ref[rows, :] = _dot(h, win_ref[:, QKV_WIDTH:])
        for kv in range(N_KV_HEADS):
            kn = _rms_head(kv_cols[:, _head_cols(kv)], kg)
            v_kv = kv_cols[:, KV_WIDTH + kv * HEAD_DIM:KV_WIDTH + (kv + 1) * HEAD_DIM]
            head_rows = pl.ds(N_KV_HEADS * rows.start + kv, SUB_TILE, stride=N_KV_HEADS)
            k_ref[head_rows, :] = kn
            v_ref[head_rows, :] = v_kv
            scores.append((rows, kv, _dot_nt(kn.astype(BF16), q4s[kv]), v_kv.T.astype(BF16)))
    probs = []
    for rows, kv, st, vt_kv in scores:
        p = jnp.exp2(st - jnp.max(st, axis=0, keepdims=True))
        probs.append((rows, kv, p.astype(BF16), jnp.sum(p, axis=0, keepdims=True), vt_kv))
    for rows, kv, p, l, vt_kv in probs:
        ot = _dot(vt_kv, p) / l
        for g in range(GROUP):
            a_ref[rows, _head_cols(kv * GROUP + g)] = (
                ot[:, g * SUB_TILE:(g + 1) * SUB_TILE].T.astype(BF16))


def _front_lat_kernel(x_ref, mod_ref, w1_ref, w2_ref, lng_ref, lnb_ref, win_ref, qg_ref, kg_ref,
                      cos_ref, sin_ref, x1_ref, q_ref, kb_ref, vb_ref, pin_ref, *, alpha):
    qg = qg_ref[...] * Q_SCALE
    kg = kg_ref[...]
    ones_col = _ones_column_block(SUB_TILE)
    tiles = _sub_tiles(x_ref.shape[0])
    x1s = _ffn_rows([x_ref[rows, :] for rows in tiles], mod_ref, 0, w1_ref, w2_ref,
                    lng_ref[...], lnb_ref[...], alpha)
    for rows, x1 in zip(tiles, x1s):
        x1_ref[rows, :] = x1
        proj = _dot(_modulate_bf16(x1, mod_ref[3:4, :], mod_ref[4:5, :]), win_ref[...])
        pin_ref[rows, :] = proj[:, QKV_WIDTH:]
        cos2 = cos_ref[rows, :]
        sin2 = sin_ref[rows, :]
        for hh in range(N_HEADS):
            q_ref[rows, _head_cols(hh)] = _rope(
                _rms_head(proj[:, _head_cols(hh)], qg), cos2, sin2).astype(BF16)
        for kv in range(N_KV_HEADS):
            kn = _rms_head(proj[:, ATTN_WIDTH + kv * HEAD_DIM:ATTN_WIDTH + (kv + 1) * HEAD_DIM], kg)
            kb_ref[rows, _head_cols(kv)] = _rope(kn, cos2, sin2).astype(BF16)
            v0 = ATTN_WIDTH + KV_WIDTH + kv * HEAD_DIM
            vb_ref[rows, kv * V_AUG:kv * V_AUG + HEAD_DIM] = proj[:, v0:v0 + HEAD_DIM].astype(BF16)
            vb_ref[rows, kv * V_AUG + HEAD_DIM:(kv + 1) * V_AUG] = ones_col


def _front_in_specs(tm, mod_row_of_token, w1, w2, w_front):
    return [
        pl.BlockSpec((tm, D_MODEL), lambda i: (i, 0)),
        pl.BlockSpec((None, N_MOD, D_MODEL), lambda i: (mod_row_of_token(i * tm), 0, 0)),
        _resident(w1.shape),
        _resident(w2.shape),
        _resident((1, D_MODEL)),
        _resident((1, D_MODEL)),
        _resident(w_front.shape),
        _resident((1, HEAD_DIM)),
        _resident((1, HEAD_DIM)),
    ]


def _front_ctx(x2d, mod, mod_row_of_token, ln_g, ln_b, q_g, k_g, own, riders, *, alpha):
    n_tok = x2d.shape[0]
    tm = TOKEN_TILE
    n_steps = n_tok // tm
    n_pre = PRE_STEPS
    tile_of = lambda i: jnp.maximum(i - n_pre, 0)
    pre_of = lambda i: jnp.minimum(i, n_pre - 1)
    tok = lambda width: pl.BlockSpec((tm, width), lambda i: (tile_of(i), 0))
    cache_spec = pl.BlockSpec((tm * N_KV_HEADS, HEAD_DIM), lambda i: (tile_of(i), 0))
    cache_shape = jax.ShapeDtypeStruct((n_tok * N_KV_HEADS, HEAD_DIM), F32)

    def chunked(weights, n_chunks, chunk_of):
        in_specs, out_specs, shapes = [], [], []
        for w, col_block, width in weights:
            chunk = w.shape[0] // n_chunks
            assert chunk * n_chunks == w.shape[0] and chunk % BF16_SUBLANES == 0
            in_specs.append(
                pl.BlockSpec((chunk, width), lambda i, cb=col_block: (chunk_of(i), cb)))
            out_specs.append(pl.BlockSpec((chunk, width), lambda i: (chunk_of(i), 0)))
            shapes.append(jax.ShapeDtypeStruct((w.shape[0], width), BF16))
        return in_specs, out_specs, shapes

    own_in, own_out, own_shapes = chunked(own, n_pre, pre_of)
    rider_in, rider_out, rider_shapes = chunked(riders, n_steps, tile_of)
    return pl.pallas_call(
        functools.partial(_front_ctx_kernel, n_own=len(own), n_riders=len(riders), n_pre=n_pre,
                          alpha=alpha),
        grid=(n_pre + n_steps,),
        in_specs=[
            tok(D_MODEL),
            pl.BlockSpec((None, N_MOD, D_MODEL),
                         lambda i: (mod_row_of_token(tile_of(i) * tm), 0, 0)),
            _resident((1, D_MODEL)),
            _resident((1, D_MODEL)),
            _resident((1, HEAD_DIM)),
            _resident((1, HEAD_DIM)),
        ] + own_in + rider_in,
        out_specs=[tok(D_MODEL), tok(ATTN_WIDTH), cache_spec, cache_spec, tok(POOL_WIDTH)]
        + rider_out + own_out,
        out_shape=[
            jax.ShapeDtypeStruct((n_tok, D_MODEL), F32),
            jax.ShapeDtypeStruct((n_tok, ATTN_WIDTH), BF16),
            cache_shape,
            cache_shape,
            jax.ShapeDtypeStruct((n_tok, POOL_WIDTH), F32),
        ] + rider_shapes + own_shapes,
        scratch_shapes=[pltpu.VMEM(s.shape, BF16) for s in own_shapes],
        compiler_params=_compiler_params(1),
        name="front_ctx",
    )(x2d, mod, ln_g.reshape(1, D_MODEL), ln_b.reshape(1, D_MODEL),
      q_g.reshape(1, HEAD_DIM), k_g.reshape(1, HEAD_DIM),
      *[w for w, _, _ in own], *[w for w, _, _ in riders])


def _front_lat(x2d, mod, mod_row_of_token, w1, w2, ln_g, ln_b, w_front, q_g, k_g, cos2, sin2,
               *, n_lat, alpha):
    n_tok = x2d.shape[0]
    tm = TOKEN_TILE
    tiles_per_seq = n_lat // tm
    tok = lambda width: pl.BlockSpec((tm, width), lambda i: (i, 0))
    rope_spec = pl.BlockSpec((tm, HEAD_DIM), lambda i: (i % tiles_per_seq, 0))
    return pl.pallas_call(
        functools.partial(_front_lat_kernel, alpha=alpha),
        grid=(n_tok // tm,),
        in_specs=_front_in_specs(tm, mod_row_of_token, w1, w2, w_front) + [rope_spec, rope_spec],
        out_specs=[tok(D_MODEL), tok(ATTN_WIDTH), tok(KV_WIDTH), tok(N_KV_HEADS * V_AUG),
                   tok(POOL_WIDTH)],
        out_shape=[
            jax.ShapeDtypeStruct((n_tok, D_MODEL), F32),
            jax.ShapeDtypeStruct((n_tok, ATTN_WIDTH), BF16),
            jax.ShapeDtypeStruct((n_tok, KV_WIDTH), BF16),
            jax.ShapeDtypeStruct((n_tok, N_KV_HEADS * V_AUG), BF16),
            jax.ShapeDtypeStruct((n_tok, POOL_WIDTH), F32),
        ],
        compiler_params=_compiler_params(1),
        name="front_lat",
    )(x2d, mod, w1, w2, ln_g.reshape(1, D_MODEL), ln_b.reshape(1, D_MODEL), w_front,
      q_g.reshape(1, HEAD_DIM), k_g.reshape(1, HEAD_DIM), cos2, sin2)


def _attn_lat_kernel(q_ref, kc_ref, vc_ref, kl_ref, vl_ref, o_ref):
    n_lat = kl_ref.shape[0]
    past = kc_ref.shape[0] // N_KV_HEADS
    head_rows = pl.ds(pl.program_id(1), past, stride=N_KV_HEADS)
    vc = jnp.concatenate([vc_ref[head_rows, :].astype(BF16), _ones_column_block(past)], axis=1)
    chunks = [(kc_ref[head_rows, :].astype(BF16), vc)]
    for c0 in range(0, n_lat, KV_CHUNK):
        chunks.append((kl_ref[c0:c0 + KV_CHUNK, :], vl_ref[c0:c0 + KV_CHUNK, :]))
    for r0 in range(0, q_ref.shape[0], Q_SUB):
        rows = slice(r0, r0 + Q_SUB)
        q4 = _stack_heads([q_ref[rows, _head_cols(g)] for g in range(GROUP)])
        m = None
        acc = None
        for k_c, v_c in chunks:
            s = _dot_nt(q4, k_c)
            m_c = jnp.max(s, axis=-1, keepdims=True)
            if m is None:
                m = m_c
                acc = _dot(jnp.exp2(s - m).astype(BF16), v_c)
            else:
                m_new = jnp.maximum(m, m_c)
                acc = jnp.exp2(m - m_new) * acc + _dot(jnp.exp2(s - m_new).astype(BF16), v_c)
                m = m_new
        o = acc[:, :HEAD_DIM] / acc[:, HEAD_DIM:HEAD_DIM + 1]
        for g in range(GROUP):
            o_ref[rows, _head_cols(g)] = o[g * Q_SUB:(g + 1) * Q_SUB].astype(BF16)


def _attn_lat(q, kb, vb, cache_k, cache_v, n_lat):
    n_tok = q.shape[0]
    n_batch = n_tok // n_lat
    tq = Q_TILE
    q_tiles = n_lat // tq
    gw = GROUP * HEAD_DIM
    cache_spec = pl.BlockSpec((None,) + cache_k.shape[1:], lambda b, kv, i: (b, 0, 0))
    return pl.pallas_call(
        _attn_lat_kernel,
        grid=(n_batch, N_KV_HEADS, q_tiles),
        in_specs=[
            pl.BlockSpec((tq, gw), lambda b, kv, i: (b * q_tiles + i, kv)),
            cache_spec,
            cache_spec,
            pl.BlockSpec((n_lat, HEAD_DIM), lambda b, kv, i: (b, kv)),
            pl.BlockSpec((n_lat, V_AUG), lambda b, kv, i: (b, kv)),
        ],
        out_specs=pl.BlockSpec((tq, gw), lambda b, kv, i: (b * q_tiles + i, kv)),
        out_shape=jax.ShapeDtypeStruct((n_tok, ATTN_WIDTH), BF16),
        compiler_params=_compiler_params(3),
        name="attn_lat",
    )(q, cache_k, cache_v, kb, vb)


def _window_sums(e):
    n = e.shape[0]
    back = lambda a, k: pltpu.roll(a, k, axis=0)
    fwd = lambda a, k: pltpu.roll(a, n - k, axis=0)
    c2 = e + back(e, 1)
    c4 = back(c2, 1) + fwd(c2, 1)
    c8 = back(c4, 2) + fwd(c4, 2)
    c16 = back(c8, 4) + fwd(c8, 4)
    return c2, c4, c8, c16


def _back_kernel(x_ref, a_ref, pin_ref, pp_ref, pn_ref, mod_ref, wg_ref, poolw_ref, pscale_ref,
                 wua_ref, wup_ref, wo_ref, lng1_ref, lnb1_ref, w1_ref, w2_ref, lng2_ref, lnb2_ref,
                 o_ref, *, seq_len, alpha):
    tm = x_ref.shape[0]
    tile_start = pl.program_id(0) * tm
    shift = mod_ref[3:4, :]
    scale = mod_ref[4:5, :]
    gate = mod_ref[5:6, :]
    p_ext = jnp.concatenate([pp_ref[...], pin_ref[...], pn_ref[...]], axis=0)

    tiles = _sub_tiles(tm)
    xs = [x_ref[rows, :] for rows in tiles]

    gated = []
    pooled_up = []
    half = D_MODEL // 2
    for rows, x in zip(tiles, xs):
        h = _modulate_bf16(x, shift, scale)
        a_blk = a_ref[rows, :]
        big_operands = [
            (h, wg_ref, slice(0, D_MODEL)),
            (h, wg_ref, slice(D_MODEL, 2 * D_MODEL)),
            (a_blk, wua_ref, slice(0, half)),
            (a_blk, wua_ref, slice(half, D_MODEL)),
        ]
        big = []

        pos = (tile_start + rows.start) % seq_len
        e_all = p_ext[rows.start:rows.start + SUB_TILE + 2 * HALO, :]
        row = lax.broadcasted_iota(jnp.int32, (SUB_TILE + 2 * HALO, 1), 0)
        first_row = jnp.where(pos == 0, HALO, 0)
        end_row = jnp.where(pos + SUB_TILE == seq_len, HALO + SUB_TILE, SUB_TILE + 2 * HALO)
        e_all = jnp.where(jnp.logical_or(row < first_row, row >= end_row), 0.0, e_all)
        t = pos + lax.broadcasted_iota(jnp.int32, (SUB_TILE, 1), 0)
        pgs = []
        for g, w in enumerate(POOL_WINDOWS):
            lhs, w_ref, cols = big_operands[g]
            big.append(_dot(lhs, w_ref[:, cols]))
            sl = slice(g * POOL_GROUP_DIM, (g + 1) * POOL_GROUP_DIM)
            e = e_all[:, sl]
            win = _window_sums(e)[g][HALO:HALO + SUB_TILE]
            lo = jnp.maximum(t - w // 2, 0)
            hi = jnp.minimum(t + w // 2, seq_len)
            inv_cnt = 1.0 / (hi - lo).astype(F32)
            pooled = (win * inv_cnt - e[HALO:HALO + SUB_TILE]).astype(BF16)
            pgs.append(_dot(pooled, poolw_ref[g]))

        ga, gp, au_lo, au_hi = big
        pg = jnp.concatenate(pgs, axis=1)
        gated.append((jax.nn.sigmoid(ga) * jnp.concatenate([au_lo, au_hi], axis=1), gp))
        pooled_up.append(_dot((pg * pscale_ref[...]).astype(BF16), wup_ref[...]))

    merged = [(ga_up + jax.nn.sigmoid(gp) * pu).astype(BF16)
              for (ga_up, gp), pu in zip(gated, pooled_up)]
    mixes = [_dot(m, wo_ref[...]) for m in merged]
    x2s = [_layer_norm(alpha * x + gate * mix, lng1_ref[...], lnb1_ref[...])
           for x, mix in zip(xs, mixes)]

    ys = _ffn_rows(x2s, mod_ref, 6, w1_ref, w2_ref, lng2_ref[...], lnb2_ref[...], alpha)
    for rows, y in zip(tiles, ys):
        o_ref[rows, :] = y


def _back(x2d, attn, pin, mod, mod_row_of_token, w_g, pool_w, pool_scale, w_up_attn, w_up_pool,
          w_out, ln_g1, ln_b1, w1, w2, ln_g2, ln_b2, *, seq_len, alpha):
    n_tok = x2d.shape[0]
    tm = TOKEN_TILE
    halo_per_tile = tm // HALO
    n_halo_blocks = n_tok // HALO
    row_vec = lambda v: v.reshape(1, D_MODEL)
    return pl.pallas_call(
        functools.partial(_back_kernel, seq_len=seq_len, alpha=alpha),
        grid=(n_tok // tm,),
        in_specs=[
            pl.BlockSpec((tm, D_MODEL), lambda i: (i, 0)),
            pl.BlockSpec((tm, ATTN_WIDTH), lambda i: (i, 0)),
            pl.BlockSpec((tm, POOL_WIDTH), lambda i: (i, 0)),
            pl.BlockSpec((HALO, POOL_WIDTH), lambda i: (jnp.maximum(i * halo_per_tile - 1, 0), 0)),
            pl.BlockSpec((HALO, POOL_WIDTH),
                         lambda i: (jnp.minimum((i + 1) * halo_per_tile, n_halo_blocks - 1), 0)),
            pl.BlockSpec((None, N_MOD, D_MODEL), lambda i: (mod_row_of_token(i * tm), 0, 0)),
            _resident(w_g.shape),
            _resident(pool_w.shape),
            _resident((1, POOL_WIDTH)),
            _resident(w_up_attn.shape),
            _resident(w_up_pool.shape),
            _resident(w_out.shape),
            _resident((1, D_MODEL)),
            _resident((1, D_MODEL)),
            _resident(w1.shape),
            _resident(w2.shape),
            _resident((1, D_MODEL)),
            _resident((1, D_MODEL)),
        ],
        out_specs=pl.BlockSpec((tm, D_MODEL), lambda i: (i, 0)),
        out_shape=jax.ShapeDtypeStruct((n_tok, D_MODEL), F32),
        compiler_params=_compiler_params(1),
        name="back",
    )(x2d, attn, pin, pin, pin, mod, w_g, pool_w, pool_scale.reshape(1, POOL_WIDTH),
      w_up_attn, w_up_pool, w_out, row_vec(ln_g1), row_vec(ln_b1), w1, w2,
      row_vec(ln_g2), row_vec(ln_b2))


def _rope_tables(n_tokens):
    t = np.arange(n_tokens)
    n_freq = HEAD_DIM // 4
    inv_freq = ROPE_THETA ** (-np.arange(n_freq, dtype=np.float64) / n_freq)
    ang = np.concatenate([(t // GRID_W)[:, None] * inv_freq, (t % GRID_W)[:, None] * inv_freq],
                         axis=-1)
    cos, sin = np.cos(ang), np.sin(ang)
    return (jnp.asarray(np.concatenate([cos, cos], axis=-1), F32),
            jnp.asarray(np.concatenate([-sin, sin], axis=-1), F32))


def kernel(x_prompt, x_sample, cache_k, cache_v, c, c_ctx, w_mod, b_mod, ln_g, ln_b, ffn1_w1, ffn1_w2, w_in, q_norm_g, k_norm_g, pool_w, pool_scale, w_up_attn, w_up_pool, w_out, ffn2_w1, ffn2_w2):
    batch, seq, _ = x_prompt.shape
    dec_batch, n_lat, _ = x_sample.shape
    depth = w_mod.shape[0]
    alpha = (2.0 * depth) ** 0.25
    assert seq == SUB_TILE and (batch * seq) % TOKEN_TILE == 0
    assert n_lat % TOKEN_TILE == 0 and n_lat % Q_TILE == 0 and n_lat % KV_CHUNK == 0
    assert 1 + dec_batch <= MOD_ROWS

    cos2, sin2 = _rope_tables((n_lat // GRID_W) * GRID_W)
    cond = jnp.zeros((MOD_ROWS, D_MODEL), F32).at[0].set(c_ctx).at[1:1 + dec_batch].set(c)

    ctx_row = lambda t: 0
    lat_row = lambda t: 1 + t // n_lat

    yp = x_prompt.reshape(batch * seq, D_MODEL)
    ys = x_sample.reshape(dec_batch * n_lat, D_MODEL)
    new_k, new_v = [], []
    for l in range(depth):
        pool_w_b = pool_w[l].astype(BF16)
        assert w_in.shape[2] == 2 * FRONT_WIDTH
        own = [(ffn1_w1[l], 0, 2 * D_FF), (ffn1_w2[l], 0, D_MODEL), (w_in[l], 0, FRONT_WIDTH)]
        riders = [(ffn2_w1[l], 0, 2 * D_FF), (ffn2_w2[l], 0, D_MODEL),
                  (w_in[l], 1, FRONT_WIDTH), (w_up_attn[l], 0, D_MODEL),
                  (w_up_pool[l], 0, D_MODEL), (w_out[l], 0, D_MODEL)]

        mod = _modulation_table(cond, w_mod[l], b_mod[l]).reshape(MOD_ROWS, N_MOD, D_MODEL)

        (x1, attn, k_l, v_l, pin, w1b, w2b, w_g, w_ua, w_upl, w_o,
         w1a, w2a, w_front) = _front_ctx(
            yp, mod, ctx_row, ln_g[l, 0], ln_b[l, 0], q_norm_g[l], k_norm_g[l], own, riders,
            alpha=alpha)
        back = functools.partial(
            _back, w_g=w_g, pool_w=pool_w_b, pool_scale=pool_scale[l], w_up_attn=w_ua,
            w_up_pool=w_upl, w_out=w_o, ln_g1=ln_g[l, 1], ln_b1=ln_b[l, 1], w1=w1b, w2=w2b,
            ln_g2=ln_g[l, 2], ln_b2=ln_b[l, 2], alpha=alpha)
        yp = back(x1, attn, pin, mod, ctx_row, seq_len=seq)
        new_k.append(k_l.reshape(batch, seq, N_KV_HEADS, HEAD_DIM))
        new_v.append(v_l.reshape(batch, seq, N_KV_HEADS, HEAD_DIM))

        x1, q, kb, vb, pin = _front_lat(
            ys, mod, lat_row, w1a, w2a, ln_g[l, 0], ln_b[l, 0], w_front,
            q_norm_g[l], k_norm_g[l], cos2, sin2, n_lat=n_lat, alpha=alpha)
        past = cache_k.shape[2]
        attn = _attn_lat(q, kb, vb,
                         cache_k[:, l].reshape(dec_batch, past * N_KV_HEADS, HEAD_DIM),
                         cache_v[:, l].reshape(dec_batch, past * N_KV_HEADS, HEAD_DIM), n_lat)
        ys = back(x1, attn, pin, mod, lat_row, seq_len=n_lat)

    return (yp.reshape(batch, seq, D_MODEL),
            ys.reshape(dec_batch, n_lat, D_MODEL),
            jnp.stack(new_k, axis=1),
            jnp.stack(new_v, axis=1))
```

```python
import functools
import math

import numpy as np

import jax
import jax.numpy as jnp
from jax import lax
from jax.experimental import pallas as pl
from jax.experimental.pallas import tpu as pltpu

D_MODEL = 1024
GRID_W = 64
N_HEADS = 8
N_KV_HEADS = 2
HEAD_DIM = 128
GROUP = N_HEADS // N_KV_HEADS
ATTN_WIDTH = N_HEADS * HEAD_DIM
KV_WIDTH = N_KV_HEADS * HEAD_DIM
POOL_WINDOWS = (2, 4, 8, 16)
N_POOL_GROUPS = 4
POOL_GROUP_DIM = 128
POOL_WIDTH = N_POOL_GROUPS * POOL_GROUP_DIM
QKV_WIDTH = ATTN_WIDTH + 2 * KV_WIDTH
FRONT_WIDTH = QKV_WIDTH + POOL_WIDTH
D_FF = 2816
N_MOD = 9
ROPE_THETA = 10000.0
LN_EPS = 1e-6
RMS_EPS = 1e-6

SUBLANES = 8
LANES = 128
MXU_DIM = 256
VMEM_LIMIT_BYTES = 58 * 1024 * 1024

HALO = SUBLANES
MOD_ROWS = SUBLANES
MOD_BLOCK = 2304
TOKEN_TILE = 512
SUB_TILE = 256
BF16_SUBLANES = 2 * SUBLANES
PRE_STEPS = 16
Q_TILE = 512
Q_SUB = 256
KV_CHUNK = 256
FF_CHUNKS = ((0, 768), (768, 1024), (1792, 1024))
Q_SCALE = HEAD_DIM ** -0.5 * math.log2(math.e)
V_AUG = MXU_DIM

BF16 = jnp.bfloat16
F32 = jnp.float32


def _dot(a, b):
    return jnp.dot(a, b, preferred_element_type=F32)


def _dot_nt(a, b):
    return lax.dot_general(a, b, (((1,), (1,)), ((), ())), preferred_element_type=F32)


def _layer_norm(x, g, b):
    mu = jnp.mean(x, axis=-1, keepdims=True)
    xc = x - mu
    var = jnp.mean(xc * xc, axis=-1, keepdims=True)
    return xc * lax.rsqrt(var + LN_EPS) * g + b


def _modulate_bf16(x, shift, scale):
    return (x * (1.0 + scale) + shift).astype(BF16)


def _resident(shape):
    zeros = (0,) * len(shape)
    return pl.BlockSpec(shape, lambda *_: zeros, pipeline_mode=pl.Buffered(1))


def _compiler_params(n_axes):
    return pltpu.CompilerParams(
        dimension_semantics=("arbitrary",) * n_axes,
        vmem_limit_bytes=VMEM_LIMIT_BYTES,
    )


def _mod_kernel(cond_ref, w_ref, b_ref, o_ref):
    c = cond_ref[...]
    s = (c * jax.nn.sigmoid(c)).astype(BF16)
    o_ref[...] = _dot(s, w_ref[...].astype(BF16)) + b_ref[...]


def _modulation_table(cond, w_mod, b_mod):
    n_out = w_mod.shape[1]
    bn = MOD_BLOCK
    return pl.pallas_call(
        _mod_kernel,
        grid=(n_out // bn,),
        in_specs=[
            pl.BlockSpec((MOD_ROWS, D_MODEL), lambda j: (0, 0)),
            pl.BlockSpec((D_MODEL, bn), lambda j: (0, j)),
            pl.BlockSpec((1, bn), lambda j: (0, j)),
        ],
        out_specs=pl.BlockSpec((MOD_ROWS, bn), lambda j: (0, j)),
        out_shape=jax.ShapeDtypeStruct((MOD_ROWS, n_out), F32),
        compiler_params=_compiler_params(1),
        name="modulation",
    )(cond, w_mod, b_mod.reshape(1, n_out))


def _ffn_rows(xs, mod_ref, mod_base, w1_ref, w2_ref, ln_g, ln_b, alpha):
    shift = mod_ref[mod_base:mod_base + 1, :]
    scale = mod_ref[mod_base + 1:mod_base + 2, :]
    gate = mod_ref[mod_base + 2:mod_base + 3, :]
    hs = [_modulate_bf16(x, shift, scale) for x in xs]
    fs = [None] * len(xs)
    for start, width in FF_CHUNKS:
        us = []
        for h in hs:
            a = _dot(h, w1_ref[:, start:start + width])
            b = _dot(h, w1_ref[:, D_FF + start:D_FF + start + width])
            us.append((a * jax.nn.sigmoid(a) * b).astype(BF16))
        for i, u in enumerate(us):
            part = _dot(u, w2_ref[start:start + width, :])
            fs[i] = part if fs[i] is None else fs[i] + part
    return [_layer_norm(alpha * x + (0.5 * gate) * f, ln_g, ln_b) for x, f in zip(xs, fs)]


def _sub_tiles(n_rows):
    return [slice(r0, r0 + SUB_TILE) for r0 in range(0, n_rows, SUB_TILE)]


def _rms_head(xh, g):
    ms = jnp.mean(xh * xh, axis=-1, keepdims=True)
    return xh * lax.rsqrt(ms + RMS_EPS) * g


def _rope(xh, cos2, sin2):
    return xh * cos2 + pltpu.roll(xh, HEAD_DIM // 2, axis=1) * sin2


def _ones_column_block(rows):
    col = lax.broadcasted_iota(jnp.int32, (rows, V_AUG - HEAD_DIM), 1)
    return jnp.where(col == 0, 1.0, 0.0).astype(BF16)


def _stack_heads(heads):
    return jnp.concatenate(heads, axis=0)


def _head_cols(hh):
    return slice(hh * HEAD_DIM, (hh + 1) * HEAD_DIM)


def _front_ctx_kernel(x_ref, mod_ref, lng_ref, lnb_ref, qg_ref, kg_ref, *refs,
                      n_own, n_riders, n_pre, alpha):
    own_in = refs[:n_own]
    rider_in = refs[n_own:n_own + n_riders]
    outs = refs[n_own + n_riders:]
    x1_ref, a_ref, k_ref, v_ref, pin_ref = outs[:5]
    rider_out = outs[5:5 + n_riders]
    own_out = outs[5 + n_riders:5 + n_riders + n_own]
    w1_ref, w2_ref, win_ref = outs[5 + n_riders + n_own:]
    step = pl.program_id(0)

    own_whole = (w1_ref, w2_ref, win_ref)

    def chunk_rows(ref, index):
        chunk = ref.shape[0]
        return pl.ds(pl.multiple_of(index * chunk, BF16_SUBLANES), chunk)

    @pl.when(step < n_pre)
    def _load_own_weights():
        for src, whole in zip(own_in, own_whole):
            whole[chunk_rows(src, step), :] = src[...].astype(BF16)

    @pl.when(step >= n_pre)
    def _token_step():
        for src, dst in zip(rider_in, rider_out):
            dst[...] = src[...].astype(BF16)
        for dst, whole in zip(own_out, own_whole):
            dst[...] = whole[chunk_rows(dst, step - n_pre), :]
        _front_ctx_body(x_ref, mod_ref, w1_ref, w2_ref, lng_ref, lnb_ref, win_ref, qg_ref, kg_ref,
                        x1_ref, a_ref, k_ref, v_ref, pin_ref, alpha)


def _front_ctx_body(x_ref, mod_ref, w1_ref, w2_ref, lng_ref, lnb_ref, win_ref, qg_ref, kg_ref,
                    x1_ref, a_ref, k_ref, v_ref, pin_ref, alpha):
    qg = qg_ref[...] * Q_SCALE
    kg = kg_ref[...]
    tiles = _sub_tiles(x_ref.shape[0])
    x1s = _ffn_rows([x_ref[rows, :] for rows in tiles], mod_ref, 0, w1_ref, w2_ref,
                    lng_ref[...], lnb_ref[...], alpha)
    gw = GROUP * HEAD_DIM
    scores = []
    for rows, x1 in zip(tiles, x1s):
        x1_ref[rows, :] = x1
        h = _modulate_bf16(x1, mod_ref[3:4, :], mod_ref[4:5, :])
        q_cols = [_dot(h, win_ref[:, :gw]), _dot(h, win_ref[:, gw:ATTN_WIDTH])]
        q4s = [_stack_heads([_rms_head(q_cols[0][:, _head_cols(g)], qg).astype(BF16)
                             for g in range(GROUP)])]
        kv_cols = _dot(h, win_ref[:, ATTN_WIDTH:QKV_WIDTH])
        q4s.append(_stack_heads([_rms_head(q_cols[1][:, _head_cols(g)], qg).astype(BF16)
                                 for g in range(GROUP)]))
        pin_ref[rows, :] = _dot(h, win_ref[:, QKV_WIDTH:])
        for kv in range(N_KV_HEADS):
            kn = _rms_head(kv_cols[:, _head_cols(kv)], kg)
            v_kv = kv_cols[:, KV_WIDTH + kv * HEAD_DIM:KV_WIDTH + (kv + 1) * HEAD_DIM]
            head_rows = pl.ds(N_KV_HEADS * rows.start + kv, SUB_TILE, stride=N_KV_HEADS)
            k_ref[head_rows, :] = kn
            v_ref[head_rows, :] = v_kv
            scores.append((rows, kv, _dot_nt(kn.astype(BF16), q4s[kv]), v_kv.T.astype(BF16)))
    probs = []
    for rows, kv, st, vt_kv in scores:
        p = jnp.exp2(st - jnp.max(st, axis=0, keepdims=True))
        probs.append((rows, kv, p.astype(BF16), jnp.sum(p, axis=0, keepdims=True), vt_kv))
    for rows, kv, p, l, vt_kv in probs:
        ot = _dot(vt_kv, p) / l
        for g in range(GROUP):
            a_ref[rows, _head_cols(kv * GROUP + g)] = (
                ot[:, g * SUB_TILE:(g + 1) * SUB_TILE].T.astype(BF16))


def _front_lat_kernel(x_ref, mod_ref, w1_ref, w2_ref, lng_ref, lnb_ref, win_ref, qg_ref, kg_ref,
                      cos_ref, sin_ref, x1_ref, q_ref, kb_ref, vb_ref, pin_ref, *, alpha):
    qg = qg_ref[...] * Q_SCALE
    kg = kg_ref[...]
    ones_col = _ones_column_block(SUB_TILE)
    tiles = _sub_tiles(x_ref.shape[0])
    x1s = _ffn_rows([x_ref[rows, :] for rows in tiles], mod_ref, 0, w1_ref, w2_ref,
                    lng_ref[...], lnb_ref[...], alpha)
    for rows, x1 in zip(tiles, x1s):
        x1_ref[rows, :] = x1
        proj = _dot(_modulate_bf16(x1, mod_ref[3:4, :], mod_ref[4:5, :]), win_ref[...])
        pin_ref[rows, :] = proj[:, QKV_WIDTH:]
        cos2 = cos_ref[rows, :]
        sin2 = sin_ref[rows, :]
        for hh in range(N_HEADS):
            q_ref[rows, _head_cols(hh)] = _rope(
                _rms_head(proj[:, _head_cols(hh)], qg), cos2, sin2).astype(BF16)
        for kv in range(N_KV_HEADS):
            kn = _rms_head(proj[:, ATTN_WIDTH + kv * HEAD_DIM:ATTN_WIDTH + (kv + 1) * HEAD_DIM], kg)
            kb_ref[rows, _head_cols(kv)] = _rope(kn, cos2, sin2).astype(BF16)
            v0 = ATTN_WIDTH + KV_WIDTH + kv * HEAD_DIM
            vb_ref[rows, kv * V_AUG:kv * V_AUG + HEAD_DIM] = proj[:, v0:v0 + HEAD_DIM].astype(BF16)
            vb_ref[rows, kv * V_AUG + HEAD_DIM:(kv + 1) * V_AUG] = ones_col


def _front_in_specs(tm, mod_row_of_token, w1, w2, w_front):
    return [
        pl.BlockSpec((tm, D_MODEL), lambda i: (i, 0)),
        pl.BlockSpec((None, N_MOD, D_MODEL), lambda i: (mod_row_of_token(i * tm), 0, 0)),
        _resident(w1.shape),
        _resident(w2.shape),
        _resident((1, D_MODEL)),
        _resident((1, D_MODEL)),
        _resident(w_front.shape),
        _resident((1, HEAD_DIM)),
        _resident((1, HEAD_DIM)),
    ]


def _front_ctx(x2d, mod, mod_row_of_token, ln_g, ln_b, q_g, k_g, own, riders, *, alpha):
    n_tok = x2d.shape[0]
    tm = TOKEN_TILE
    n_steps = n_tok // tm
    n_pre = PRE_STEPS
    tile_of = lambda i: jnp.maximum(i - n_pre, 0)
    pre_of = lambda i: jnp.minimum(i, n_pre - 1)
    tok = lambda width: pl.BlockSpec((tm, width), lambda i: (tile_of(i), 0))
    cache_spec = pl.BlockSpec((tm * N_KV_HEADS, HEAD_DIM), lambda i: (tile_of(i), 0))
    cache_shape = jax.ShapeDtypeStruct((n_tok * N_KV_HEADS, HEAD_DIM), F32)

    def chunked(weights, n_chunks, chunk_of):
        in_specs, out_specs, shapes = [], [], []
        for w, col_block, width in weights:
            chunk = w.shape[0] // n_chunks
            assert chunk * n_chunks == w.shape[0] and chunk % BF16_SUBLANES == 0
            in_specs.append(
                pl.BlockSpec((chunk, width), lambda i, cb=col_block: (chunk_of(i), cb)))
            out_specs.append(pl.BlockSpec((chunk, width), lambda i: (chunk_of(i), 0)))
            shapes.append(jax.ShapeDtypeStruct((w.shape[0], width), BF16))
        return in_specs, out_specs, shapes

    own_in, _, own_shapes = chunked(own, n_pre, pre_of)
    _, own_out, _ = chunked(own, n_steps, tile_of)
    rider_in, rider_out, rider_shapes = chunked(riders, n_steps, tile_of)
    return pl.pallas_call(
        functools.partial(_front_ctx_kernel, n_own=len(own), n_riders=len(riders), n_pre=n_pre,
                          alpha=alpha),
        grid=(n_pre + n_steps,),
        in_specs=[
            tok(D_MODEL),
            pl.BlockSpec((None, N_MOD, D_MODEL),
                         lambda i: (mod_row_of_token(tile_of(i) * tm), 0, 0)),
            _resident((1, D_MODEL)),
            _resident((1, D_MODEL)),
            _resident((1, HEAD_DIM)),
            _resident((1, HEAD_DIM)),
        ] + own_in + rider_in,
        out_specs=[tok(D_MODEL), tok(ATTN_WIDTH), cache_spec, cache_spec, tok(POOL_WIDTH)]
        + rider_out + own_out,
        out_shape=[
            jax.ShapeDtypeStruct((n_tok, D_MODEL), F32),
            jax.ShapeDtypeStruct((n_tok, ATTN_WIDTH), BF16),
            cache_shape,
            cache_shape,
            jax.ShapeDtypeStruct((n_tok, POOL_WIDTH), F32),
        ] + rider_shapes + own_shapes,
        scratch_shapes=[pltpu.VMEM(s.shape, BF16) for s in own_shapes],
        compiler_params=_compiler_params(1),
        name="front_ctx",
    )(x2d, mod, ln_g.reshape(1, D_MODEL), ln_b.reshape(1, D_MODEL),
      q_g.reshape(1, HEAD_DIM), k_g.reshape(1, HEAD_DIM),
      *[w for w, _, _ in own], *[w for w, _, _ in riders])


def _front_lat(x2d, mod, mod_row_of_token, w1, w2, ln_g, ln_b, w_front, q_g, k_g, cos2, sin2,
               *, n_lat, alpha):
    n_tok = x2d.shape[0]
    tm = TOKEN_TILE
    tiles_per_seq = n_lat // tm
    tok = lambda width: pl.BlockSpec((tm, width), lambda i: (i, 0))
    rope_spec = pl.BlockSpec((tm, HEAD_DIM), lambda i: (i % tiles_per_seq, 0))
    return pl.pallas_call(
        functools.partial(_front_lat_kernel, alpha=alpha),
        grid=(n_tok // tm,),
        in_specs=_front_in_specs(tm, mod_row_of_token, w1, w2, w_front) + [rope_spec, rope_spec],
        out_specs=[tok(D_MODEL), tok(ATTN_WIDTH), tok(KV_WIDTH), tok(N_KV_HEADS * V_AUG),
                   tok(POOL_WIDTH)],
        out_shape=[
            jax.ShapeDtypeStruct((n_tok, D_MODEL), F32),
            jax.ShapeDtypeStruct((n_tok, ATTN_WIDTH), BF16),
            jax.ShapeDtypeStruct((n_tok, KV_WIDTH), BF16),
            jax.ShapeDtypeStruct((n_tok, N_KV_HEADS * V_AUG), BF16),
            jax.ShapeDtypeStruct((n_tok, POOL_WIDTH), F32),
        ],
        compiler_params=_compiler_params(1),
        name="front_lat",
    )(x2d, mod, w1, w2, ln_g.reshape(1, D_MODEL), ln_b.reshape(1, D_MODEL), w_front,
      q_g.reshape(1, HEAD_DIM), k_g.reshape(1, HEAD_DIM), cos2, sin2)


def _attn_lat_kernel(q_ref, kc_ref, vc_ref, kl_ref, vl_ref, o_ref):
    n_lat = kl_ref.shape[0]
    past = kc_ref.shape[0] // N_KV_HEADS
    head_rows = pl.ds(pl.program_id(1), past, stride=N_KV_HEADS)
    vc = jnp.concatenate([vc_ref[head_rows, :].astype(BF16), _ones_column_block(past)], axis=1)
    chunks = [(kc_ref[head_rows, :].astype(BF16), vc)]
    for c0 in range(0, n_lat, KV_CHUNK):
        chunks.append((kl_ref[c0:c0 + KV_CHUNK, :], vl_ref[c0:c0 + KV_CHUNK, :]))
    for r0 in range(0, q_ref.shape[0], Q_SUB):
        rows = slice(r0, r0 + Q_SUB)
        q4 = _stack_heads([q_ref[rows, _head_cols(g)] for g in range(GROUP)])
        m = None
        acc = None
        for k_c, v_c in chunks:
            s = _dot_nt(q4, k_c)
            m_c = jnp.max(s, axis=-1, keepdims=True)
            if m is None:
                m = m_c
                acc = _dot(jnp.exp2(s - m).astype(BF16), v_c)
            else:
                m_new = jnp.maximum(m, m_c)
                acc = jnp.exp2(m - m_new) * acc + _dot(jnp.exp2(s - m_new).astype(BF16), v_c)
                m = m_new
        o = acc[:, :HEAD_DIM] / acc[:, HEAD_DIM:HEAD_DIM + 1]
        for g in range(GROUP):
            o_ref[rows, _head_cols(g)] = o[g * Q_SUB:(g + 1) * Q_SUB].astype(BF16)


def _attn_lat(q, kb, vb, cache_k, cache_v, n_lat):
    n_tok = q.shape[0]
    n_batch = n_tok // n_lat
    tq = Q_TILE
    q_tiles = n_lat // tq
    gw = GROUP * HEAD_DIM
    cache_spec = pl.BlockSpec((None,) + cache_k.shape[1:], lambda b, kv, i: (b, 0, 0))
    return pl.pallas_call(
        _attn_lat_kernel,
        grid=(n_batch, N_KV_HEADS, q_tiles),
        in_specs=[
            pl.BlockSpec((tq, gw), lambda b, kv, i: (b * q_tiles + i, kv)),
            cache_spec,
            cache_spec,
            pl.BlockSpec((n_lat, HEAD_DIM), lambda b, kv, i: (b, kv)),
            pl.BlockSpec((n_lat, V_AUG), lambda b, kv, i: (b, kv)),
        ],
        out_specs=pl.BlockSpec((tq, gw), lambda b, kv, i: (b * q_tiles + i, kv)),
        out_shape=jax.ShapeDtypeStruct((n_tok, ATTN_WIDTH), BF16),
        compiler_params=_compiler_params(3),
        name="attn_lat",
    )(q, cache_k, cache_v, kb, vb)


def _window_sums(e):
    n = e.shape[0]
    back = lambda a, k: pltpu.roll(a, k, axis=0)
    fwd = lambda a, k: pltpu.roll(a, n - k, axis=0)
    c2 = e + back(e, 1)
    c4 = back(c2, 1) + fwd(c2, 1)
    c8 = back(c4, 2) + fwd(c4, 2)
    c16 = back(c8, 4) + fwd(c8, 4)
    return c2, c4, c8, c16


def _back_kernel(x_ref, a_ref, pin_ref, pp_ref, pn_ref, mod_ref, wg_ref, poolw_ref, pscale_ref,
                 wua_ref, wup_ref, wo_ref, lng1_ref, lnb1_ref, w1_ref, w2_ref, lng2_ref, lnb2_ref,
                 o_ref, *, seq_len, alpha):
    tm = x_ref.shape[0]
    tile_start = pl.program_id(0) * tm
    shift = mod_ref[3:4, :]
    scale = mod_ref[4:5, :]
    gate = mod_ref[5:6, :]
    p_ext = jnp.concatenate([pp_ref[...], pin_ref[...], pn_ref[...]], axis=0)

    tiles = _sub_tiles(tm)
    xs = [x_ref[rows, :] for rows in tiles]

    gated = []
    pooled_up = []
    half = D_MODEL // 2
    for rows, x in zip(tiles, xs):
        h = _modulate_bf16(x, shift, scale)
        a_blk = a_ref[rows, :]
        big_operands = [
            (h, wg_ref, slice(0, D_MODEL)),
            (h, wg_ref, slice(D_MODEL, 2 * D_MODEL)),
            (a_blk, wua_ref, slice(0, half)),
            (a_blk, wua_ref, slice(half, D_MODEL)),
        ]
        big = []

        pos = (tile_start + rows.start) % seq_len
        e_all = p_ext[rows.start:rows.start + SUB_TILE + 2 * HALO, :]
        row = lax.broadcasted_iota(jnp.int32, (SUB_TILE + 2 * HALO, 1), 0)
        first_row = jnp.where(pos == 0, HALO, 0)
        end_row = jnp.where(pos + SUB_TILE == seq_len, HALO + SUB_TILE, SUB_TILE + 2 * HALO)
        e_all = jnp.where(jnp.logical_or(row < first_row, row >= end_row), 0.0, e_all)
        t = pos + lax.broadcasted_iota(jnp.int32, (SUB_TILE, 1), 0)
        pgs = []
        for g, w in enumerate(POOL_WINDOWS):
            lhs, w_ref, cols = big_operands[g]
            big.append(_dot(lhs, w_ref[:, cols]))
            sl = slice(g * POOL_GROUP_DIM, (g + 1) * POOL_GROUP_DIM)
            e = e_all[:, sl]
            win = _window_sums(e)[g][HALO:HALO + SUB_TILE]
            lo = jnp.maximum(t - w // 2, 0)
            hi = jnp.minimum(t + w // 2, seq_len)
            inv_cnt = 1.0 / (hi - lo).astype(F32)
            pooled = (win * inv_cnt - e[HALO:HALO + SUB_TILE]).astype(BF16)
            pgs.append(_dot(pooled, poolw_ref[g]))

        ga, gp, au_lo, au_hi = big
        pg = jnp.concatenate(pgs, axis=1)
        gated.append((jax.nn.sigmoid(ga) * jnp.concatenate([au_lo, au_hi], axis=1), gp))
        pooled_up.append(_dot((pg * pscale_ref[...]).astype(BF16), wup_ref[...]))

    merged = [(ga_up + jax.nn.sigmoid(gp) * pu).astype(BF16)
              for (ga_up, gp), pu in zip(gated, pooled_up)]
    mixes = [_dot(m, wo_ref[...]) for m in merged]
    x2s = [_layer_norm(alpha * x + gate * mix, lng1_ref[...], lnb1_ref[...])
           for x, mix in zip(xs, mixes)]

    ys = _ffn_rows(x2s, mod_ref, 6, w1_ref, w2_ref, lng2_ref[...], lnb2_ref[...], alpha)
    for rows, y in zip(tiles, ys):
        o_ref[rows, :] = y


def _back(x2d, attn, pin, mod, mod_row_of_token, w_g, pool_w, pool_scale, w_up_attn, w_up_pool,
          w_out, ln_g1, ln_b1, w1, w2, ln_g2, ln_b2, *, seq_len, alpha):
    n_tok = x2d.shape[0]
    tm = TOKEN_TILE
    halo_per_tile = tm // HALO
    n_halo_blocks = n_tok // HALO
    row_vec = lambda v: v.reshape(1, D_MODEL)
    return pl.pallas_call(
        functools.partial(_back_kernel, seq_len=seq_len, alpha=alpha),
        grid=(n_tok // tm,),
        in_specs=[
            pl.BlockSpec((tm, D_MODEL), lambda i: (i, 0)),
            pl.BlockSpec((tm, ATTN_WIDTH), lambda i: (i, 0)),
            pl.BlockSpec((tm, POOL_WIDTH), lambda i: (i, 0)),
            pl.BlockSpec((HALO, POOL_WIDTH), lambda i: (jnp.maximum(i * halo_per_tile - 1, 0), 0)),
            pl.BlockSpec((HALO, POOL_WIDTH),
                         lambda i: (jnp.minimum((i + 1) * halo_per_tile, n_halo_blocks - 1), 0)),
            pl.BlockSpec((None, N_MOD, D_MODEL), lambda i: (mod_row_of_token(i * tm), 0, 0)),
            _resident(w_g.shape),
            _resident(pool_w.shape),
            _resident((1, POOL_WIDTH)),
            _resident(w_up_attn.shape),
            _resident(w_up_pool.shape),
            _resident(w_out.shape),
            _resident((1, D_MODEL)),
            _resident((1, D_MODEL)),
            _resident(w1.shape),
            _resident(w2.shape),
            _resident((1, D_MODEL)),
            _resident((1, D_MODEL)),
        ],
        out_specs=pl.BlockSpec((tm, D_MODEL), lambda i: (i, 0)),
        out_shape=jax.ShapeDtypeStruct((n_tok, D_MODEL), F32),
        compiler_params=_compiler_params(1),
        name="back",
    )(x2d, attn, pin, pin, pin, mod, w_g, pool_w, pool_scale.reshape(1, POOL_WIDTH),
      w_up_attn, w_up_pool, w_out, row_vec(ln_g1), row_vec(ln_b1), w1, w2,
      row_vec(ln_g2), row_vec(ln_b2))


def _rope_tables(n_tokens):
    t = np.arange(n_tokens)
    n_freq = HEAD_DIM // 4
    inv_freq = ROPE_THETA ** (-np.arange(n_freq, dtype=np.float64) / n_freq)
    ang = np.concatenate([(t // GRID_W)[:, None] * inv_freq, (t % GRID_W)[:, None] * inv_freq],
                         axis=-1)
    cos, sin = np.cos(ang), np.sin(ang)
    return (jnp.asarray(np.concatenate([cos, cos], axis=-1), F32),
            jnp.asarray(np.concatenate([-sin, sin], axis=-1), F32))


def kernel(x_prompt, x_sample, cache_k, cache_v, c, c_ctx, w_mod, b_mod, ln_g, ln_b, ffn1_w1, ffn1_w2, w_in, q_norm_g, k_norm_g, pool_w, pool_scale, w_up_attn, w_up_pool, w_out, ffn2_w1, ffn2_w2):
    batch, seq, _ = x_prompt.shape
    dec_batch, n_lat, _ = x_sample.shape
    depth = w_mod.shape[0]
    alpha = (2.0 * depth) ** 0.25
    assert seq == SUB_TILE and (batch * seq) % TOKEN_TILE == 0
    assert n_lat % TOKEN_TILE == 0 and n_lat % Q_TILE == 0 and n_lat % KV_CHUNK == 0
    assert 1 + dec_batch <= MOD_ROWS

    cos2, sin2 = _rope_tables((n_lat // GRID_W) * GRID_W)
    cond = jnp.zeros((MOD_ROWS, D_MODEL), F32).at[0].set(c_ctx).at[1:1 + dec_batch].set(c)

    ctx_row = lambda t: 0
    lat_row = lambda t: 1 + t // n_lat

    yp = x_prompt.reshape(batch * seq, D_MODEL)
    ys = x_sample.reshape(dec_batch * n_lat, D_MODEL)
    new_k, new_v = [], []
    for l in range(depth):
        pool_w_b = pool_w[l].astype(BF16)
        assert w_in.shape[2] == 2 * FRONT_WIDTH
        own = [(ffn1_w1[l], 0, 2 * D_FF), (ffn1_w2[l], 0, D_MODEL), (w_in[l], 0, FRONT_WIDTH)]
        riders = [(ffn2_w1[l], 0, 2 * D_FF), (ffn2_w2[l], 0, D_MODEL),
                  (w_in[l], 1, FRONT_WIDTH), (w_up_attn[l], 0, D_MODEL),
                  (w_up_pool[l], 0, D_MODEL), (w_out[l], 0, D_MODEL)]

        mod = _modulation_table(cond, w_mod[l], b_mod[l]).reshape(MOD_ROWS, N_MOD, D_MODEL)

        (x1, attn, k_l, v_l, pin, w1b, w2b, w_g, w_ua, w_upl, w_o,
         w1a, w2a, w_front) = _front_ctx(
            yp, mod, ctx_row, ln_g[l, 0], ln_b[l, 0], q_norm_g[l], k_norm_g[l], own, riders,
            alpha=alpha)
        back = functools.partial(
            _back, w_g=w_g, pool_w=pool_w_b, pool_scale=pool_scale[l], w_up_attn=w_ua,
            w_up_pool=w_upl, w_out=w_o, ln_g1=ln_g[l, 1], ln_b1=ln_b[l, 1], w1=w1b, w2=w2b,
            ln_g2=ln_g[l, 2], ln_b2=ln_b[l, 2], alpha=alpha)
        yp = back(x1, attn, pin, mod, ctx_row, seq_len=seq)
        new_k.append(k_l.reshape(batch, seq, N_KV_HEADS, HEAD_DIM))
        new_v.append(v_l.reshape(batch, seq, N_KV_HEADS, HEAD_DIM))

        x1, q, kb, vb, pin = _front_lat(
            ys, mod, lat_row, w1a, w2a, ln_g[l, 0], ln_b[l, 0], w_front,
            q_norm_g[l], k_norm_g[l], cos2, sin2, n_lat=n_lat, alpha=alpha)
        past = cache_k.shape[2]
        attn = _attn_lat(q, kb, vb,
                         cache_k[:, l].reshape(dec_batch, past * N_KV_HEADS, HEAD_DIM),
                         cache_v[:, l].reshape(dec_batch, past * N_KV_HEADS, HEAD_DIM), n_lat)
        ys = back(x1, attn, pin, mod, lat_row, seq_len=n_lat)

    return (yp.reshape(batch, seq, D_MODEL),
            ys.reshape(dec_batch, n_lat, D_MODEL),
            jnp.stack(new_k, axis=1),
            jnp.stack(new_v, axis=1))
```

```python
import functools
import math

import numpy as np

import jax
import jax.numpy as jnp
from jax import lax
from jax.experimental import pallas as pl
from jax.experimental.pallas import tpu as pltpu

D_MODEL = 1024
GRID_W = 64
N_HEADS = 8
N_KV_HEADS = 2
HEAD_DIM = 128
GROUP = N_HEADS // N_KV_HEADS
ATTN_WIDTH = N_HEADS * HEAD_DIM
KV_WIDTH = N_KV_HEADS * HEAD_DIM
POOL_WINDOWS = (2, 4, 8, 16)
N_POOL_GROUPS = 4
POOL_GROUP_DIM = 128
POOL_WIDTH = N_POOL_GROUPS * POOL_GROUP_DIM
QKV_WIDTH = ATTN_WIDTH + 2 * KV_WIDTH
FRONT_WIDTH = QKV_WIDTH + POOL_WIDTH
D_FF = 2816
N_MOD = 9
ROPE_THETA = 10000.0
LN_EPS = 1e-6
RMS_EPS = 1e-6

SUBLANES = 8
LANES = 128
MXU_DIM = 256
VMEM_LIMIT_BYTES = 58 * 1024 * 1024

HALO = SUBLANES
MOD_ROWS = SUBLANES
TOKEN_TILE = 512
SUB_TILE = 256
BF16_SUBLANES = 2 * SUBLANES
PRE_STEPS = 16
Q_TILE = 512
Q_SUB = 256
KV_CHUNK = 256
FF_CHUNKS = ((0, 768), (768, 1024), (1792, 1024))
Q_SCALE = HEAD_DIM ** -0.5 * math.log2(math.e)
V_AUG = MXU_DIM

BF16 = jnp.bfloat16
F32 = jnp.float32


def _dot(a, b):
    return jnp.dot(a, b, preferred_element_type=F32)


def _dot_nt(a, b):
    return lax.dot_general(a, b, (((1,), (1,)), ((), ())), preferred_element_type=F32)


def _layer_norm(x, g, b):
    mu = jnp.mean(x, axis=-1, keepdims=True)
    xc = x - mu
    var = jnp.mean(xc * xc, axis=-1, keepdims=True)
    return xc * lax.rsqrt(var + LN_EPS) * g + b


def _modulate_bf16(x, shift, scale):
    return (x * (1.0 + scale) + shift).astype(BF16)


def _resident(shape):
    zeros = (0,) * len(shape)
    return pl.BlockSpec(shape, lambda *_: zeros, pipeline_mode=pl.Buffered(1))


def _compiler_params(n_axes):
    return pltpu.CompilerParams(
        dimension_semantics=("arbitrary",) * n_axes,
        vmem_limit_bytes=VMEM_LIMIT_BYTES,
    )


def _mod_kernel(cond_ref, w_ref, b_ref, o_ref):
    c = cond_ref[...]
    s = (c * jax.nn.sigmoid(c)).astype(BF16)
    vec = _dot(s, w_ref[...].astype(BF16)) + b_ref[...]
    o_ref[:, pl.ds(pl.program_id(0), 1), :] = vec[:, None, :]


def _modulation_table(cond, w_mod, b_mod):
    n_out = w_mod.shape[1]
    assert n_out == N_MOD * D_MODEL
    return pl.pallas_call(
        _mod_kernel,
        grid=(N_MOD,),
        in_specs=[
            pl.BlockSpec((MOD_ROWS, D_MODEL), lambda j: (0, 0)),
            pl.BlockSpec((D_MODEL, D_MODEL), lambda j: (0, j)),
            pl.BlockSpec((1, D_MODEL), lambda j: (0, j)),
        ],
        out_specs=pl.BlockSpec((MOD_ROWS, N_MOD, D_MODEL), lambda j: (0, 0, 0)),
        out_shape=jax.ShapeDtypeStruct((MOD_ROWS, N_MOD, D_MODEL), F32),
        compiler_params=_compiler_params(1),
        name="modulation",
    )(cond, w_mod, b_mod.reshape(1, n_out))


def _ffn_rows(xs, mod_ref, mod_base, w1_ref, w2_ref, ln_g, ln_b, alpha):
    shift = mod_ref[mod_base:mod_base + 1, :]
    scale = mod_ref[mod_base + 1:mod_base + 2, :]
    gate = mod_ref[mod_base + 2:mod_base + 3, :]
    hs = [_modulate_bf16(x, shift, scale) for x in xs]
    fs = [None] * len(xs)
    for start, width in FF_CHUNKS:
        us = []
        for h in hs:
            a = _dot(h, w1_ref[:, start:start + width])
            b = _dot(h, w1_ref[:, D_FF + start:D_FF + start + width])
            us.append((a * jax.nn.sigmoid(a) * b).astype(BF16))
        for i, u in enumerate(us):
            part = _dot(u, w2_ref[start:start + width, :])
            fs[i] = part if fs[i] is None else fs[i] + part
    return [_layer_norm(alpha * x + (0.5 * gate) * f, ln_g, ln_b) for x, f in zip(xs, fs)]


def _sub_tiles(n_rows):
    return [slice(r0, r0 + SUB_TILE) for r0 in range(0, n_rows, SUB_TILE)]


def _rms_head(xh, g):
    ms = jnp.mean(xh * xh, axis=-1, keepdims=True)
    return xh * lax.rsqrt(ms + RMS_EPS) * g


def _rope(xh, cos2, sin2):
    return xh * cos2 + pltpu.roll(xh, HEAD_DIM // 2, axis=1) * sin2


def _ones_column_block(rows):
    col = lax.broadcasted_iota(jnp.int32, (rows, V_AUG - HEAD_DIM), 1)
    return jnp.where(col == 0, 1.0, 0.0).astype(BF16)


def _stack_heads(heads):
    return jnp.concatenate(heads, axis=0)


def _head_cols(hh):
    return slice(hh * HEAD_DIM, (hh + 1) * HEAD_DIM)


def _front_ctx_kernel(x_ref, mod_ref, lng_ref, lnb_ref, qg_ref, kg_ref, *refs,
                      n_own, n_riders, n_pre, alpha):
    own_in = refs[:n_own]
    rider_in = refs[n_own:n_own + n_riders]
    outs = refs[n_own + n_riders:]
    x1_ref, a_ref, k_ref, v_ref, pin_ref = outs[:5]
    rider_out = outs[5:5 + n_riders]
    own_out = outs[5 + n_riders:5 + n_riders + n_own]
    w1_ref, w2_ref, win_ref = outs[5 + n_riders + n_own:]
    step = pl.program_id(0)

    own_whole = (w1_ref, w2_ref, win_ref)

    def chunk_rows(ref, index):
        chunk = ref.shape[0]
        return pl.ds(pl.multiple_of(index * chunk, BF16_SUBLANES), chunk)

    @pl.when(step < n_pre)
    def _load_own_weights():
        for src, whole in zip(own_in, own_whole):
            whole[chunk_rows(src, step), :] = src[...].astype(BF16)

    @pl.when(step >= n_pre)
    def _token_step():
        for src, dst in zip(rider_in, rider_out):
            dst[...] = src[...].astype(BF16)
        for dst, whole in zip(own_out, own_whole):
            dst[...] = whole[chunk_rows(dst, step - n_pre), :]
        _front_ctx_body(x_ref, mod_ref, w1_ref, w2_ref, lng_ref, lnb_ref, win_ref, qg_ref, kg_ref,
                        x1_ref, a_ref, k_ref, v_ref, pin_ref, alpha)


def _front_ctx_body(x_ref, mod_ref, w1_ref, w2_ref, lng_ref, lnb_ref, win_ref, qg_ref, kg_ref,
                    x1_ref, a_ref, k_ref, v_ref, pin_ref, alpha):
    qg = qg_ref[...] * Q_SCALE
    kg = kg_ref[...]
    tiles = _sub_tiles(x_ref.shape[0])
    x1s = _ffn_rows([x_ref[rows, :] for rows in tiles], mod_ref, 0, w1_ref, w2_ref,
                    lng_ref[...], lnb_ref[...], alpha)
    gw = GROUP * HEAD_DIM
    scores = []
    for rows, x1 in zip(tiles, x1s):
        x1_ref[rows, :] = x1
        h = _modulate_bf16(x1, mod_ref[3:4, :], mod_ref[4:5, :])
        q_cols = [_dot(h, win_ref[:, :gw]), _dot(h, win_ref[:, gw:ATTN_WIDTH])]
        q4s = [_stack_heads([_rms_head(q_cols[0][:, _head_cols(g)], qg).astype(BF16)
                             for g in range(GROUP)])]
        kv_cols = _dot(h, win_ref[:, ATTN_WIDTH:QKV_WIDTH])
        q4s.append(_stack_heads([_rms_head(q_cols[1][:, _head_cols(g)], qg).astype(BF16)
                                 for g in range(GROUP)]))
        pin_ref[rows, :] = _dot(h, win_ref[:, QKV_WIDTH:])
        for kv in range(N_KV_HEADS):
            kn = _rms_head(kv_cols[:, _head_cols(kv)], kg)
            v_kv = kv_cols[:, KV_WIDTH + kv * HEAD_DIM:KV_WIDTH + (kv + 1) * HEAD_DIM]
            head_rows = pl.ds(N_KV_HEADS * rows.start + kv, SUB_TILE, stride=N_KV_HEADS)
            k_ref[head_rows, :] = kn
            v_ref[head_rows, :] = v_kv
            scores.append((rows, kv, _dot_nt(kn.astype(BF16), q4s[kv]), v_kv.T.astype(BF16)))
    probs = []
    for rows, kv, st, vt_kv in scores:
        p = jnp.exp2(st - jnp.max(st, axis=0, keepdims=True))
        probs.append((rows, kv, p.astype(BF16), jnp.sum(p, axis=0, keepdims=True), vt_kv))
    for rows, kv, p, l, vt_kv in probs:
        ot = _dot(vt_kv, p) / l
        for g in range(GROUP):
            a_ref[rows, _head_cols(kv * GROUP + g)] = (
                ot[:, g * SUB_TILE:(g + 1) * SUB_TILE].T.astype(BF16))


def _front_lat_kernel(x_ref, mod_ref, w1_ref, w2_ref, lng_ref, lnb_ref, win_ref, qg_ref, kg_ref,
                      cos_ref, sin_ref, x1_ref, q_ref, kb_ref, vb_ref, pin_ref, *, alpha):
    qg = qg_ref[...] * Q_SCALE
    kg = kg_ref[...]
    ones_col = _ones_column_block(SUB_TILE)
    tiles = _sub_tiles(x_ref.shape[0])
    x1s = _ffn_rows([x_ref[rows, :] for rows in tiles], mod_ref, 0, w1_ref, w2_ref,
                    lng_ref[...], lnb_ref[...], alpha)
    for rows, x1 in zip(tiles, x1s):
        x1_ref[rows, :] = x1
        proj = _dot(_modulate_bf16(x1, mod_ref[3:4, :], mod_ref[4:5, :]), win_ref[...])
        pin_ref[rows, :] = proj[:, QKV_WIDTH:]
        cos2 = cos_ref[rows, :]
        sin2 = sin_ref[rows, :]
        for hh in range(N_HEADS):
            q_ref[rows, _head_cols(hh)] = _rope(
                _rms_head(proj[:, _head_cols(hh)], qg), cos2, sin2).astype(BF16)
        for kv in range(N_KV_HEADS):
            kn = _rms_head(proj[:, ATTN_WIDTH + kv * HEAD_DIM:ATTN_WIDTH + (kv + 1) * HEAD_DIM], kg)
            kb_ref[rows, _head_cols(kv)] = _rope(kn, cos2, sin2).astype(BF16)
            v0 = ATTN_WIDTH + KV_WIDTH + kv * HEAD_DIM
            vb_ref[rows, kv * V_AUG:kv * V_AUG + HEAD_DIM] = proj[:, v0:v0 + HEAD_DIM].astype(BF16)
            vb_ref[rows, kv * V_AUG + HEAD_DIM:(kv + 1) * V_AUG] = ones_col


def _front_in_specs(tm, mod_row_of_token, w1, w2, w_front):
    return [
        pl.BlockSpec((tm, D_MODEL), lambda i: (i, 0)),
        pl.BlockSpec((None, N_MOD, D_MODEL), lambda i: (mod_row_of_token(i * tm), 0, 0)),
        _resident(w1.shape),
        _resident(w2.shape),
        _resident((1, D_MODEL)),
        _resident((1, D_MODEL)),
        _resident(w_front.shape),
        _resident((1, HEAD_DIM)),
        _resident((1, HEAD_DIM)),
    ]


def _front_ctx(x2d, mod, mod_row_of_token, ln_g, ln_b, q_g, k_g, own, riders, *, alpha):
    n_tok = x2d.shape[0]
    tm = TOKEN_TILE
    n_steps = n_tok // tm
    n_pre = PRE_STEPS
    tile_of = lambda i: jnp.maximum(i - n_pre, 0)
    pre_of = lambda i: jnp.minimum(i, n_pre - 1)
    tok = lambda width: pl.BlockSpec((tm, width), lambda i: (tile_of(i), 0))
    cache_spec = pl.BlockSpec((tm * N_KV_HEADS, HEAD_DIM), lambda i: (tile_of(i), 0))
    cache_shape = jax.ShapeDtypeStruct((n_tok * N_KV_HEADS, HEAD_DIM), F32)

    def chunked(weights, n_chunks, chunk_of):
        in_specs, out_specs, shapes = [], [], []
        for w, col_block, width in weights:
            chunk = w.shape[0] // n_chunks
            assert chunk * n_chunks == w.shape[0] and chunk % BF16_SUBLANES == 0
            in_specs.append(
                pl.BlockSpec((chunk, width), lambda i, cb=col_block: (chunk_of(i), cb)))
            out_specs.append(pl.BlockSpec((chunk, width), lambda i: (chunk_of(i), 0)))
            shapes.append(jax.ShapeDtypeStruct((w.shape[0], width), BF16))
        return in_specs, out_specs, shapes

    own_in, _, own_shapes = chunked(own, n_pre, pre_of)
    _, own_out, _ = chunked(own, n_steps, tile_of)
    rider_in, rider_out, rider_shapes = chunked(riders, n_steps, tile_of)
    return pl.pallas_call(
        functools.partial(_front_ctx_kernel, n_own=len(own), n_riders=len(riders), n_pre=n_pre,
                          alpha=alpha),
        grid=(n_pre + n_steps,),
        in_specs=[
            tok(D_MODEL),
            pl.BlockSpec((None, N_MOD, D_MODEL),
                         lambda i: (mod_row_of_token(tile_of(i) * tm), 0, 0)),
            _resident((1, D_MODEL)),
            _resident((1, D_MODEL)),
            _resident((1, HEAD_DIM)),
            _resident((1, HEAD_DIM)),
        ] + own_in + rider_in,
        out_specs=[tok(D_MODEL), tok(ATTN_WIDTH), cache_spec, cache_spec, tok(POOL_WIDTH)]
        + rider_out + own_out,
        out_shape=[
            jax.ShapeDtypeStruct((n_tok, D_MODEL), F32),
            jax.ShapeDtypeStruct((n_tok, ATTN_WIDTH), BF16),
            cache_shape,
            cache_shape,
            jax.ShapeDtypeStruct((n_tok, POOL_WIDTH), F32),
        ] + rider_shapes + own_shapes,
        scratch_shapes=[pltpu.VMEM(s.shape, BF16) for s in own_shapes],
        compiler_params=_compiler_params(1),
        name="front_ctx",
    )(x2d, mod, ln_g.reshape(1, D_MODEL), ln_b.reshape(1, D_MODEL),
      q_g.reshape(1, HEAD_DIM), k_g.reshape(1, HEAD_DIM),
      *[w for w, _, _ in own], *[w for w, _, _ in riders])


def _front_lat(x2d, mod, mod_row_of_token, w1, w2, ln_g, ln_b, w_front, q_g, k_g, cos2, sin2,
               *, n_lat, alpha):
    n_tok = x2d.shape[0]
    tm = TOKEN_TILE
    tiles_per_seq = n_lat // tm
    tok = lambda width: pl.BlockSpec((tm, width), lambda i: (i, 0))
    rope_spec = pl.BlockSpec((tm, HEAD_DIM), lambda i: (i % tiles_per_seq, 0))
    return pl.pallas_call(
        functools.partial(_front_lat_kernel, alpha=alpha),
        grid=(n_tok // tm,),
        in_specs=_front_in_specs(tm, mod_row_of_token, w1, w2, w_front) + [rope_spec, rope_spec],
        out_specs=[tok(D_MODEL), tok(ATTN_WIDTH), tok(KV_WIDTH), tok(N_KV_HEADS * V_AUG),
                   tok(POOL_WIDTH)],
        out_shape=[
            jax.ShapeDtypeStruct((n_tok, D_MODEL), F32),
            jax.ShapeDtypeStruct((n_tok, ATTN_WIDTH), BF16),
            jax.ShapeDtypeStruct((n_tok, KV_WIDTH), BF16),
            jax.ShapeDtypeStruct((n_tok, N_KV_HEADS * V_AUG), BF16),
            jax.ShapeDtypeStruct((n_tok, POOL_WIDTH), F32),
        ],
        compiler_params=_compiler_params(1),
        name="front_lat",
    )(x2d, mod, w1, w2, ln_g.reshape(1, D_MODEL), ln_b.reshape(1, D_MODEL), w_front,
      q_g.reshape(1, HEAD_DIM), k_g.reshape(1, HEAD_DIM), cos2, sin2)


def _attn_lat_kernel(q_ref, kc_ref, vc_ref, kl_ref, vl_ref, o_ref):
    n_lat = kl_ref.shape[0]
    past = kc_ref.shape[0] // N_KV_HEADS
    head_rows = pl.ds(pl.program_id(1), past, stride=N_KV_HEADS)
    vc = jnp.concatenate([vc_ref[head_rows, :].astype(BF16), _ones_column_block(past)], axis=1)
    chunks = [(kc_ref[head_rows, :].astype(BF16), vc)]
    for c0 in range(0, n_lat, KV_CHUNK):
        chunks.append((kl_ref[c0:c0 + KV_CHUNK, :], vl_ref[c0:c0 + KV_CHUNK, :]))
    for r0 in range(0, q_ref.shape[0], Q_SUB):
        rows = slice(r0, r0 + Q_SUB)
        q4 = _stack_heads([q_ref[rows, _head_cols(g)] for g in range(GROUP)])
        m = None
        acc = None
        for k_c, v_c in chunks:
            s = _dot_nt(q4, k_c)
            m_c = jnp.max(s, axis=-1, keepdims=True)
            if m is None:
                m = m_c
                acc = _dot(jnp.exp2(s - m).astype(BF16), v_c)
            else:
                m_new = jnp.maximum(m, m_c)
                acc = jnp.exp2(m - m_new) * acc + _dot(jnp.exp2(s - m_new).astype(BF16), v_c)
                m = m_new
        o = acc[:, :HEAD_DIM] / acc[:, HEAD_DIM:HEAD_DIM + 1]
        for g in range(GROUP):
            o_ref[rows, _head_cols(g)] = o[g * Q_SUB:(g + 1) * Q_SUB].astype(BF16)


def _attn_lat(q, kb, vb, cache_k, cache_v, n_lat):
    n_tok = q.shape[0]
    n_batch = n_tok // n_lat
    tq = Q_TILE
    q_tiles = n_lat // tq
    gw = GROUP * HEAD_DIM
    cache_spec = pl.BlockSpec((None,) + cache_k.shape[1:], lambda b, kv, i: (b, 0, 0))
    return pl.pallas_call(
        _attn_lat_kernel,
        grid=(n_batch, N_KV_HEADS, q_tiles),
        in_specs=[
            pl.BlockSpec((tq, gw), lambda b, kv, i: (b * q_tiles + i, kv)),
            cache_spec,
            cache_spec,
            pl.BlockSpec((n_lat, HEAD_DIM), lambda b, kv, i: (b, kv)),
            pl.BlockSpec((n_lat, V_AUG), lambda b, kv, i: (b, kv)),
        ],
        out_specs=pl.BlockSpec((tq, gw), lambda b, kv, i: (b * q_tiles + i, kv)),
        out_shape=jax.ShapeDtypeStruct((n_tok, ATTN_WIDTH), BF16),
        compiler_params=_compiler_params(3),
        name="attn_lat",
    )(q, cache_k, cache_v, kb, vb)


def _window_sums(e):
    n = e.shape[0]
    back = lambda a, k: pltpu.roll(a, k, axis=0)
    fwd = lambda a, k: pltpu.roll(a, n - k, axis=0)
    c2 = e + back(e, 1)
    c4 = back(c2, 1) + fwd(c2, 1)
    c8 = back(c4, 2) + fwd(c4, 2)
    c16 = back(c8, 4) + fwd(c8, 4)
    return c2, c4, c8, c16


def _back_kernel(x_ref, a_ref, pin_ref, pp_ref, pn_ref, mod_ref, wg_ref, poolw_ref, pscale_ref,
                 wua_ref, wup_ref, wo_ref, lng1_ref, lnb1_ref, w1_ref, w2_ref, lng2_ref, lnb2_ref,
                 o_ref, *, seq_len, alpha):
    tm = x_ref.shape[0]
    tile_start = pl.program_id(0) * tm
    shift = mod_ref[3:4, :]
    scale = mod_ref[4:5, :]
    gate = mod_ref[5:6, :]
    p_ext = jnp.concatenate([pp_ref[...], pin_ref[...], pn_ref[...]], axis=0)

    tiles = _sub_tiles(tm)
    xs = [x_ref[rows, :] for rows in tiles]

    gated = []
    pooled_up = []
    half = D_MODEL // 2
    for rows, x in zip(tiles, xs):
        h = _modulate_bf16(x, shift, scale)
        a_blk = a_ref[rows, :]
        big_operands = [
            (h, wg_ref, slice(0, D_MODEL)),
            (h, wg_ref, slice(D_MODEL, 2 * D_MODEL)),
            (a_blk, wua_ref, slice(0, half)),
            (a_blk, wua_ref, slice(half, D_MODEL)),
        ]
        big = []

        pos = (tile_start + rows.start) % seq_len
        e_all = p_ext[rows.start:rows.start + SUB_TILE + 2 * HALO, :]
        row = lax.broadcasted_iota(jnp.int32, (SUB_TILE + 2 * HALO, 1), 0)
        first_row = jnp.where(pos == 0, HALO, 0)
        end_row = jnp.where(pos + SUB_TILE == seq_len, HALO + SUB_TILE, SUB_TILE + 2 * HALO)
        e_all = jnp.where(jnp.logical_or(row < first_row, row >= end_row), 0.0, e_all)
        t = pos + lax.broadcasted_iota(jnp.int32, (SUB_TILE, 1), 0)
        pgs = []
        for g, w in enumerate(POOL_WINDOWS):
            lhs, w_ref, cols = big_operands[g]
            big.append(_dot(lhs, w_ref[:, cols]))
            sl = slice(g * POOL_GROUP_DIM, (g + 1) * POOL_GROUP_DIM)
            e = e_all[:, sl]
            win = _window_sums(e)[g][HALO:HALO + SUB_TILE]
            lo = jnp.maximum(t - w // 2, 0)
            hi = jnp.minimum(t + w // 2, seq_len)
            inv_cnt = 1.0 / (hi - lo).astype(F32)
            pooled = (win * inv_cnt - e[HALO:HALO + SUB_TILE]).astype(BF16)
            pgs.append(_dot(pooled, poolw_ref[g]))

        ga, gp, au_lo, au_hi = big
        pg = jnp.concatenate(pgs, axis=1)
        gated.append((jax.nn.sigmoid(ga) * jnp.concatenate([au_lo, au_hi], axis=1), gp))
        pooled_up.append(_dot((pg * pscale_ref[...]).astype(BF16), wup_ref[...]))

    merged = [(ga_up + jax.nn.sigmoid(gp) * pu).astype(BF16)
              for (ga_up, gp), pu in zip(gated, pooled_up)]
    mixes = [_dot(m, wo_ref[...]) for m in merged]
    x2s = [_layer_norm(alpha * x + gate * mix, lng1_ref[...], lnb1_ref[...])
           for x, mix in zip(xs, mixes)]

    ys = _ffn_rows(x2s, mod_ref, 6, w1_ref, w2_ref, lng2_ref[...], lnb2_ref[...], alpha)
    for rows, y in zip(tiles, ys):
        o_ref[rows, :] = y


def _back(x2d, attn, pin, mod, mod_row_of_token, w_g, pool_w, pool_scale, w_up_attn, w_up_pool,
          w_out, ln_g1, ln_b1, w1, w2, ln_g2, ln_b2, *, seq_len, alpha):
    n_tok = x2d.shape[0]
    tm = TOKEN_TILE
    halo_per_tile = tm // HALO
    n_halo_blocks = n_tok // HALO
    row_vec = lambda v: v.reshape(1, D_MODEL)
    return pl.pallas_call(
        functools.partial(_back_kernel, seq_len=seq_len, alpha=alpha),
        grid=(n_tok // tm,),
        in_specs=[
            pl.BlockSpec((tm, D_MODEL), lambda i: (i, 0)),
            pl.BlockSpec((tm, ATTN_WIDTH), lambda i: (i, 0)),
            pl.BlockSpec((tm, POOL_WIDTH), lambda i: (i, 0)),
            pl.BlockSpec((HALO, POOL_WIDTH), lambda i: (jnp.maximum(i * halo_per_tile - 1, 0), 0)),
            pl.BlockSpec((HALO, POOL_WIDTH),
                         lambda i: (jnp.minimum((i + 1) * halo_per_tile, n_halo_blocks - 1), 0)),
            pl.BlockSpec((None, N_MOD, D_MODEL), lambda i: (mod_row_of_token(i * tm), 0, 0)),
            _resident(w_g.shape),
            _resident(pool_w.shape),
            _resident((1, POOL_WIDTH)),
            _resident(w_up_attn.shape),
            _resident(w_up_pool.shape),
            _resident(w_out.shape),
            _resident((1, D_MODEL)),
            _resident((1, D_MODEL)),
            _resident(w1.shape),
            _resident(w2.shape),
            _resident((1, D_MODEL)),
            _resident((1, D_MODEL)),
        ],
        out_specs=pl.BlockSpec((tm, D_MODEL), lambda i: (i, 0)),
        out_shape=jax.ShapeDtypeStruct((n_tok, D_MODEL), F32),
        compiler_params=_compiler_params(1),
        name="back",
    )(x2d, attn, pin, pin, pin, mod, w_g, pool_w, pool_scale.reshape(1, POOL_WIDTH),
      w_up_attn, w_up_pool, w_out, row_vec(ln_g1), row_vec(ln_b1), w1, w2,
      row_vec(ln_g2), row_vec(ln_b2))


def _rope_tables(n_tokens):
    t = np.arange(n_tokens)
    n_freq = HEAD_DIM // 4
    inv_freq = ROPE_THETA ** (-np.arange(n_freq, dtype=np.float64) / n_freq)
    ang = np.concatenate([(t // GRID_W)[:, None] * inv_freq, (t % GRID_W)[:, None] * inv_freq],
                         axis=-1)
    cos, sin = np.cos(ang), np.sin(ang)
    return (jnp.asarray(np.concatenate([cos, cos], axis=-1), F32),
            jnp.asarray(np.concatenate([-sin, sin], axis=-1), F32))


def kernel(x_prompt, x_sample, cache_k, cache_v, c, c_ctx, w_mod, b_mod, ln_g, ln_b, ffn1_w1, ffn1_w2, w_in, q_norm_g, k_norm_g, pool_w, pool_scale, w_up_attn, w_up_pool, w_out, ffn2_w1, ffn2_w2):
    batch, seq, _ = x_prompt.shape
    dec_batch, n_lat, _ = x_sample.shape
    depth = w_mod.shape[0]
    alpha = (2.0 * depth) ** 0.25
    assert seq == SUB_TILE and (batch * seq) % TOKEN_TILE == 0
    assert n_lat % TOKEN_TILE == 0 and n_lat % Q_TILE == 0 and n_lat % KV_CHUNK == 0
    assert 1 + dec_batch <= MOD_ROWS

    cos2, sin2 = _rope_tables((n_lat // GRID_W) * GRID_W)
    cond = jnp.zeros((MOD_ROWS, D_MODEL), F32).at[0].set(c_ctx).at[1:1 + dec_batch].set(c)

    ctx_row = lambda t: 0
    lat_row = lambda t: 1 + t // n_lat

    yp = x_prompt.reshape(batch * seq, D_MODEL)
    ys = x_sample.reshape(dec_batch * n_lat, D_MODEL)
    new_k, new_v = [], []
    for l in range(depth):
        pool_w_b = pool_w[l].astype(BF16)
        assert w_in.shape[2] == 2 * FRONT_WIDTH
        own = [(ffn1_w1[l], 0, 2 * D_FF), (ffn1_w2[l], 0, D_MODEL), (w_in[l], 0, FRONT_WIDTH)]
        riders = [(ffn2_w1[l], 0, 2 * D_FF), (ffn2_w2[l], 0, D_MODEL),
                  (w_in[l], 1, FRONT_WIDTH), (w_up_attn[l], 0, D_MODEL),
                  (w_up_pool[l], 0, D_MODEL), (w_out[l], 0, D_MODEL)]

        mod = _modulation_table(cond, w_mod[l], b_mod[l])

        (x1, attn, k_l, v_l, pin, w1b, w2b, w_g, w_ua, w_upl, w_o,
         w1a, w2a, w_front) = _front_ctx(
            yp, mod, ctx_row, ln_g[l, 0], ln_b[l, 0], q_norm_g[l], k_norm_g[l], own, riders,
            alpha=alpha)
        back = functools.partial(
            _back, w_g=w_g, pool_w=pool_w_b, pool_scale=pool_scale[l], w_up_attn=w_ua,
            w_up_pool=w_upl, w_out=w_o, ln_g1=ln_g[l, 1], ln_b1=ln_b[l, 1], w1=w1b, w2=w2b,
            ln_g2=ln_g[l, 2], ln_b2=ln_b[l, 2], alpha=alpha)
        yp = back(x1, attn, pin, mod, ctx_row, seq_len=seq)
        new_k.append(k_l.reshape(batch, seq, N_KV_HEADS, HEAD_DIM))
        new_v.append(v_l.reshape(batch, seq, N_KV_HEADS, HEAD_DIM))

        x1, q, kb, vb, pin = _front_lat(
            ys, mod, lat_row, w1a, w2a, ln_g[l, 0], ln_b[l, 0], w_front,
            q_norm_g[l], k_norm_g[l], cos2, sin2, n_lat=n_lat, alpha=alpha)
        past = cache_k.shape[2]
        attn = _attn_lat(q, kb, vb,
                         cache_k[:, l].reshape(dec_batch, past * N_KV_HEADS, HEAD_DIM),
                         cache_v[:, l].reshape(dec_batch, past * N_KV_HEADS, HEAD_DIM), n_lat)
        ys = back(x1, attn, pin, mod, lat_row, seq_len=n_lat)

    return (yp.reshape(batch, seq, D_MODEL),
            ys.reshape(dec_batch, n_lat, D_MODEL),
            jnp.stack(new_k, axis=1),
            jnp.stack(new_v, axis=1))
```

```python
import functools
import math

import numpy as np

import jax
import jax.numpy as jnp
from jax import lax
from jax.experimental import pallas as pl
from jax.experimental.pallas import tpu as pltpu

D_MODEL = 1024
GRID_W = 64
N_HEADS = 8
N_KV_HEADS = 2
HEAD_DIM = 128
GROUP = N_HEADS // N_KV_HEADS
ATTN_WIDTH = N_HEADS * HEAD_DIM
KV_WIDTH = N_KV_HEADS * HEAD_DIM
POOL_WINDOWS = (2, 4, 8, 16)
N_POOL_GROUPS = 4
POOL_GROUP_DIM = 128
POOL_WIDTH = N_POOL_GROUPS * POOL_GROUP_DIM
QKV_WIDTH = ATTN_WIDTH + 2 * KV_WIDTH
FRONT_WIDTH = QKV_WIDTH + POOL_WIDTH
D_FF = 2816
N_MOD = 9
ROPE_THETA = 10000.0
LN_EPS = 1e-6
RMS_EPS = 1e-6

SUBLANES = 8
MXU_DIM = 256
VMEM_LIMIT_BYTES = 58 * 1024 * 1024

HALO = SUBLANES
MOD_ROWS = SUBLANES
TOKEN_TILE = 512
SUB_TILE = 256
BF16_SUBLANES = 2 * SUBLANES
PRE_STEPS = 16
Q_TILE = 512
Q_SUB = 256
KV_CHUNK = 256
FF_CHUNKS = ((0, 768), (768, 1024), (1792, 1024))
Q_SCALE = HEAD_DIM ** -0.5 * math.log2(math.e)
V_AUG = MXU_DIM

BF16 = jnp.bfloat16
F32 = jnp.float32


def _dot(a, b):
    return jnp.dot(a, b, preferred_element_type=F32)


def _dot_nt(a, b):
    return lax.dot_general(a, b, (((1,), (1,)), ((), ())), preferred_element_type=F32)


def _layer_norm(x, g, b):
    mu = jnp.mean(x, axis=-1, keepdims=True)
    xc = x - mu
    var = jnp.mean(xc * xc, axis=-1, keepdims=True)
    return xc * lax.rsqrt(var + LN_EPS) * g + b


def _modulate_bf16(x, shift, scale):
    return (x * (1.0 + scale) + shift).astype(BF16)


def _resident(shape):
    zeros = (0,) * len(shape)
    return pl.BlockSpec(shape, lambda *_: zeros, pipeline_mode=pl.Buffered(1))


def _compiler_params(n_axes):
    return pltpu.CompilerParams(
        dimension_semantics=("arbitrary",) * n_axes,
        vmem_limit_bytes=VMEM_LIMIT_BYTES,
    )


def _mod_kernel(cond_ref, w_ref, b_ref, o_ref):
    c = cond_ref[...]
    s = (c * jax.nn.sigmoid(c)).astype(BF16)
    vec = _dot(s, w_ref[...].astype(BF16)) + b_ref[...]
    o_ref[:, pl.ds(pl.program_id(0), 1), :] = vec[:, None, :]


def _modulation_table(cond, w_mod, b_mod):
    n_out = w_mod.shape[1]
    assert n_out == N_MOD * D_MODEL
    return pl.pallas_call(
        _mod_kernel,
        grid=(N_MOD,),
        in_specs=[
            pl.BlockSpec((MOD_ROWS, D_MODEL), lambda j: (0, 0)),
            pl.BlockSpec((D_MODEL, D_MODEL), lambda j: (0, j)),
            pl.BlockSpec((1, D_MODEL), lambda j: (0, j)),
        ],
        out_specs=pl.BlockSpec((MOD_ROWS, N_MOD, D_MODEL), lambda j: (0, 0, 0)),
        out_shape=jax.ShapeDtypeStruct((MOD_ROWS, N_MOD, D_MODEL), F32),
        compiler_params=_compiler_params(1),
        name="modulation",
    )(cond, w_mod, b_mod.reshape(1, n_out))


def _ffn_rows(xs, mod_ref, mod_base, w1_ref, w2_ref, ln_g, ln_b, alpha, mid_work=None):
    shift = mod_ref[mod_base:mod_base + 1, :]
    scale = mod_ref[mod_base + 1:mod_base + 2, :]
    gate = mod_ref[mod_base + 2:mod_base + 3, :]
    hs = [_modulate_bf16(x, shift, scale) for x in xs]
    fs = [None] * len(xs)
    for start, width in FF_CHUNKS:
        us = []
        for h in hs:
            a = _dot(h, w1_ref[:, start:start + width])
            b = _dot(h, w1_ref[:, D_FF + start:D_FF + start + width])
            us.append((a * jax.nn.sigmoid(a) * b).astype(BF16))
        for i, u in enumerate(us):
            part = _dot(u, w2_ref[start:start + width, :])
            fs[i] = part if fs[i] is None else fs[i] + part
        if mid_work is not None and start == FF_CHUNKS[0][0]:
            mid_work(fs[0][:SUBLANES, :HEAD_DIM])
    return [_layer_norm(alpha * x + (0.5 * gate) * f, ln_g, ln_b) for x, f in zip(xs, fs)]


def _sub_tiles(n_rows):
    return [slice(r0, r0 + SUB_TILE) for r0 in range(0, n_rows, SUB_TILE)]


def _rms_head(xh, g):
    ms = jnp.mean(xh * xh, axis=-1, keepdims=True)
    return xh * lax.rsqrt(ms + RMS_EPS) * g


def _rope(xh, cos2, sin2):
    return xh * cos2 + pltpu.roll(xh, HEAD_DIM // 2, axis=1) * sin2


def _ones_column_block(rows):
    col = lax.broadcasted_iota(jnp.int32, (rows, V_AUG - HEAD_DIM), 1)
    return jnp.where(col == 0, 1.0, 0.0).astype(BF16)


def _stack_heads(heads):
    return jnp.concatenate(heads, axis=0)


def _head_cols(hh):
    return slice(hh * HEAD_DIM, (hh + 1) * HEAD_DIM)


def _front_ctx_kernel(x_ref, mod_ref, lng_ref, lnb_ref, qg_ref, kg_ref, *refs,
                      n_own, n_riders, n_pre, alpha):
    own_in = refs[:n_own]
    rider_in = refs[n_own:n_own + n_riders]
    outs = refs[n_own + n_riders:]
    x1_ref, a_ref, k_ref, v_ref, pin_ref = outs[:5]
    rider_out = outs[5:5 + n_riders]
    own_out = outs[5 + n_riders:5 + n_riders + n_own]
    w1_ref, w2_ref, win_ref = outs[5 + n_riders + n_own:]
    step = pl.program_id(0)

    own_whole = (w1_ref, w2_ref, win_ref)

    def chunk_rows(ref, index):
        chunk = ref.shape[0]
        return pl.ds(pl.multiple_of(index * chunk, BF16_SUBLANES), chunk)

    @pl.when(step < n_pre)
    def _load_own_weights():
        for src, whole in zip(own_in, own_whole):
            whole[chunk_rows(src, step), :] = src[...].astype(BF16)

    @pl.when(step >= n_pre)
    def _token_step():
        def cast_riders(anchor):
            bits = pltpu.bitcast(anchor, jnp.uint32)
            zero = ((bits >> 16) >> 16)[0:1, 0:1].astype(F32)
            for src, dst in zip(rider_in, rider_out):
                dst[...] = (src[...] + zero).astype(BF16)

        _front_ctx_body(x_ref, mod_ref, w1_ref, w2_ref, lng_ref, lnb_ref, win_ref, qg_ref, kg_ref,
                        x1_ref, a_ref, k_ref, v_ref, pin_ref, alpha, cast_riders)
        for dst, whole in zip(own_out, own_whole):
            dst[...] = whole[chunk_rows(dst, step - n_pre), :]


def _front_ctx_body(x_ref, mod_ref, w1_ref, w2_ref, lng_ref, lnb_ref, win_ref, qg_ref, kg_ref,
                    x1_ref, a_ref, k_ref, v_ref, pin_ref, alpha, mid_ffn_work):
    qg = qg_ref[...] * Q_SCALE
    kg = kg_ref[...]
    tiles = _sub_tiles(x_ref.shape[0])
    x1s = _ffn_rows([x_ref[rows, :] for rows in tiles], mod_ref, 0, w1_ref, w2_ref,
                    lng_ref[...], lnb_ref[...], alpha, mid_ffn_work)
    gw = GROUP * HEAD_DIM
    scores = []
    for rows, x1 in zip(tiles, x1s):
        x1_ref[rows, :] = x1
        h = _modulate_bf16(x1, mod_ref[3:4, :], mod_ref[4:5, :])
        q_cols = [_dot(h, win_ref[:, :gw]), _dot(h, win_ref[:, gw:ATTN_WIDTH])]
        q4s = [_stack_heads([_rms_head(q_cols[0][:, _head_cols(g)], qg).astype(BF16)
                             for g in range(GROUP)])]
        kv_cols = _dot(h, win_ref[:, ATTN_WIDTH:QKV_WIDTH])
        q4s.append(_stack_heads([_rms_head(q_cols[1][:, _head_cols(g)], qg).astype(BF16)
                                 for g in range(GROUP)]))
        pin_ref[rows, :] = _dot(h, win_ref[:, QKV_WIDTH:])
        for kv in range(N_KV_HEADS):
            kn = _rms_head(kv_cols[:, _head_cols(kv)], kg)
            v_kv = kv_cols[:, KV_WIDTH + kv * HEAD_DIM:KV_WIDTH + (kv + 1) * HEAD_DIM]
            head_rows = pl.ds(N_KV_HEADS * rows.start + kv, SUB_TILE, stride=N_KV_HEADS)
            k_ref[head_rows, :] = kn
            v_ref[head_rows, :] = v_kv
            scores.append((rows, kv, _dot_nt(kn.astype(BF16), q4s[kv]), v_kv.T.astype(BF16)))
    probs = []
    for rows, kv, st, vt_kv in scores:
        p = jnp.exp2(st - jnp.max(st, axis=0, keepdims=True))
        probs.append((rows, kv, p.astype(BF16), jnp.sum(p, axis=0, keepdims=True), vt_kv))
    for rows, kv, p, l, vt_kv in probs:
        ot = _dot(vt_kv, p) / l
        for g in range(GROUP):
            a_ref[rows, _head_cols(kv * GROUP + g)] = (
                ot[:, g * SUB_TILE:(g + 1) * SUB_TILE].T.astype(BF16))


def _front_lat_kernel(x_ref, mod_ref, w1_ref, w2_ref, lng_ref, lnb_ref, win_ref, qg_ref, kg_ref,
                      cos_ref, sin_ref, x1_ref, q_ref, kb_ref, vb_ref, pin_ref, *, alpha):
    qg = qg_ref[...] * Q_SCALE
    kg = kg_ref[...]
    ones_col = _ones_column_block(SUB_TILE)
    tiles = _sub_tiles(x_ref.shape[0])
    x1s = _ffn_rows([x_ref[rows, :] for rows in tiles], mod_ref, 0, w1_ref, w2_ref,
                    lng_ref[...], lnb_ref[...], alpha)
    for rows, x1 in zip(tiles, x1s):
        x1_ref[rows, :] = x1
        proj = _dot(_modulate_bf16(x1, mod_ref[3:4, :], mod_ref[4:5, :]), win_ref[...])
        pin_ref[rows, :] = proj[:, QKV_WIDTH:]
        cos2 = cos_ref[rows, :]
        sin2 = sin_ref[rows, :]
        for hh in range(N_HEADS):
            q_ref[rows, _head_cols(hh)] = _rope(
                _rms_head(proj[:, _head_cols(hh)], qg), cos2, sin2).astype(BF16)
        for kv in range(N_KV_HEADS):
            kn = _rms_head(proj[:, ATTN_WIDTH + kv * HEAD_DIM:ATTN_WIDTH + (kv + 1) * HEAD_DIM], kg)
            kb_ref[rows, _head_cols(kv)] = _rope(kn, cos2, sin2).astype(BF16)
            v0 = ATTN_WIDTH + KV_WIDTH + kv * HEAD_DIM
            vb_ref[rows, kv * V_AUG:kv * V_AUG + HEAD_DIM] = proj[:, v0:v0 + HEAD_DIM].astype(BF16)
            vb_ref[rows, kv * V_AUG + HEAD_DIM:(kv + 1) * V_AUG] = ones_col


def _front_in_specs(tm, mod_row_of_token, w1, w2, w_front):
    return [
        pl.BlockSpec((tm, D_MODEL), lambda i: (i, 0)),
        pl.BlockSpec((None, N_MOD, D_MODEL), lambda i: (mod_row_of_token(i * tm), 0, 0)),
        _resident(w1.shape),
        _resident(w2.shape),
        _resident((1, D_MODEL)),
        _resident((1, D_MODEL)),
        _resident(w_front.shape),
        _resident((1, HEAD_DIM)),
        _resident((1, HEAD_DIM)),
    ]


def _front_ctx(x2d, mod, mod_row_of_token, ln_g, ln_b, q_g, k_g, own, riders, *, alpha):
    n_tok = x2d.shape[0]
    tm = TOKEN_TILE
    n_steps = n_tok // tm
    n_pre = PRE_STEPS
    tile_of = lambda i: jnp.maximum(i - n_pre, 0)
    pre_of = lambda i: jnp.minimum(i, n_pre - 1)
    tok = lambda width: pl.BlockSpec((tm, width), lambda i: (tile_of(i), 0))
    cache_spec = pl.BlockSpec((tm * N_KV_HEADS, HEAD_DIM), lambda i: (tile_of(i), 0))
    cache_shape = jax.ShapeDtypeStruct((n_tok * N_KV_HEADS, HEAD_DIM), F32)

    def chunked(weights, n_chunks, chunk_of):
        in_specs, out_specs, shapes = [], [], []
        for w, col_block, width in weights:
            chunk = w.shape[0] // n_chunks
            assert chunk * n_chunks == w.shape[0] and chunk % BF16_SUBLANES == 0
            in_specs.append(
                pl.BlockSpec((chunk, width), lambda i, cb=col_block: (chunk_of(i), cb)))
            out_specs.append(pl.BlockSpec((chunk, width), lambda i: (chunk_of(i), 0)))
            shapes.append(jax.ShapeDtypeStruct((w.shape[0], width), BF16))
        return in_specs, out_specs, shapes

    own_in, _, own_shapes = chunked(own, n_pre, pre_of)
    _, own_out, _ = chunked(own, n_steps, tile_of)
    rider_in, rider_out, rider_shapes = chunked(riders, n_steps, tile_of)
    return pl.pallas_call(
        functools.partial(_front_ctx_kernel, n_own=len(own), n_riders=len(riders), n_pre=n_pre,
                          alpha=alpha),
        grid=(n_pre + n_steps,),
        in_specs=[
            tok(D_MODEL),
            pl.BlockSpec((None, N_MOD, D_MODEL),
                         lambda i: (mod_row_of_token(tile_of(i) * tm), 0, 0)),
            _resident((1, D_MODEL)),
            _resident((1, D_MODEL)),
            _resident((1, HEAD_DIM)),
            _resident((1, HEAD_DIM)),
        ] + own_in + rider_in,
        out_specs=[tok(D_MODEL), tok(ATTN_WIDTH), cache_spec, cache_spec, tok(POOL_WIDTH)]
        + rider_out + own_out,
        out_shape=[
            jax.ShapeDtypeStruct((n_tok, D_MODEL), F32),
            jax.ShapeDtypeStruct((n_tok, ATTN_WIDTH), BF16),
            cache_shape,
            cache_shape,
            jax.ShapeDtypeStruct((n_tok, POOL_WIDTH), F32),
        ] + rider_shapes + own_shapes,
        scratch_shapes=[pltpu.VMEM(s.shape, BF16) for s in own_shapes],
        compiler_params=_compiler_params(1),
        name="front_ctx",
    )(x2d, mod, ln_g.reshape(1, D_MODEL), ln_b.reshape(1, D_MODEL),
      q_g.reshape(1, HEAD_DIM), k_g.reshape(1, HEAD_DIM),
      *[w for w, _, _ in own], *[w for w, _, _ in riders])


def _front_lat(x2d, mod, mod_row_of_token, w1, w2, ln_g, ln_b, w_front, q_g, k_g, cos2, sin2,
               *, n_lat, alpha):
    n_tok = x2d.shape[0]
    tm = TOKEN_TILE
    tiles_per_seq = n_lat // tm
    tok = lambda width: pl.BlockSpec((tm, width), lambda i: (i, 0))
    rope_spec = pl.BlockSpec((tm, HEAD_DIM), lambda i: (i % tiles_per_seq, 0))
    return pl.pallas_call(
        functools.partial(_front_lat_kernel, alpha=alpha),
        grid=(n_tok // tm,),
        in_specs=_front_in_specs(tm, mod_row_of_token, w1, w2, w_front) + [rope_spec, rope_spec],
        out_specs=[tok(D_MODEL), tok(ATTN_WIDTH), tok(KV_WIDTH), tok(N_KV_HEADS * V_AUG),
                   tok(POOL_WIDTH)],
        out_shape=[
            jax.ShapeDtypeStruct((n_tok, D_MODEL), F32),
            jax.ShapeDtypeStruct((n_tok, ATTN_WIDTH), BF16),
            jax.ShapeDtypeStruct((n_tok, KV_WIDTH), BF16),
            jax.ShapeDtypeStruct((n_tok, N_KV_HEADS * V_AUG), BF16),
            jax.ShapeDtypeStruct((n_tok, POOL_WIDTH), F32),
        ],
        compiler_params=_compiler_params(1),
        name="front_lat",
    )(x2d, mod, w1, w2, ln_g.reshape(1, D_MODEL), ln_b.reshape(1, D_MODEL), w_front,
      q_g.reshape(1, HEAD_DIM), k_g.reshape(1, HEAD_DIM), cos2, sin2)


def _attn_lat_kernel(q_ref, kc_ref, vc_ref, kl_ref, vl_ref, o_ref):
    n_lat = kl_ref.shape[0]
    past = kc_ref.shape[0] // N_KV_HEADS
    head_rows = pl.ds(pl.program_id(1), past, stride=N_KV_HEADS)
    vc = jnp.concatenate([vc_ref[head_rows, :].astype(BF16), _ones_column_block(past)], axis=1)
    chunks = [(kc_ref[head_rows, :].astype(BF16), vc)]
    for c0 in range(0, n_lat, KV_CHUNK):
        chunks.append((kl_ref[c0:c0 + KV_CHUNK, :], vl_ref[c0:c0 + KV_CHUNK, :]))
    for r0 in range(0, q_ref.shape[0], Q_SUB):
        rows = slice(r0, r0 + Q_SUB)
        q4 = _stack_heads([q_ref[rows, _head_cols(g)] for g in range(GROUP)])
        m = None
        acc = None
        for k_c, v_c in chunks:
            s = _dot_nt(q4, k_c)
            m_c = jnp.max(s, axis=-1, keepdims=True)
            if m is None:
                m = m_c
                acc = _dot(jnp.exp2(s - m).astype(BF16), v_c)
            else:
                m_new = jnp.maximum(m, m_c)
                acc = jnp.exp2(m - m_new) * acc + _dot(jnp.exp2(s - m_new).astype(BF16), v_c)
                m = m_new
        o = acc[:, :HEAD_DIM] / acc[:, HEAD_DIM:HEAD_DIM + 1]
        for g in range(GROUP):
            o_ref[rows, _head_cols(g)] = o[g * Q_SUB:(g + 1) * Q_SUB].astype(BF16)


def _attn_lat(q, kb, vb, cache_k, cache_v, n_lat):
    n_tok = q.shape[0]
    n_batch = n_tok // n_lat
    tq = Q_TILE
    q_tiles = n_lat // tq
    gw = GROUP * HEAD_DIM
    cache_spec = pl.BlockSpec((None,) + cache_k.shape[1:], lambda b, kv, i: (b, 0, 0))
    return pl.pallas_call(
        _attn_lat_kernel,
        grid=(n_batch, N_KV_HEADS, q_tiles),
        in_specs=[
            pl.BlockSpec((tq, gw), lambda b, kv, i: (b * q_tiles + i, kv)),
            cache_spec,
            cache_spec,
            pl.BlockSpec((n_lat, HEAD_DIM), lambda b, kv, i: (b, kv)),
            pl.BlockSpec((n_lat, V_AUG), lambda b, kv, i: (b, kv)),
        ],
        out_specs=pl.BlockSpec((tq, gw), lambda b, kv, i: (b * q_tiles + i, kv)),
        out_shape=jax.ShapeDtypeStruct((n_tok, ATTN_WIDTH), BF16),
        compiler_params=_compiler_params(3),
        name="attn_lat",
    )(q, cache_k, cache_v, kb, vb)


def _window_sums(e):
    n = e.shape[0]
    back = lambda a, k: pltpu.roll(a, k, axis=0)
    fwd = lambda a, k: pltpu.roll(a, n - k, axis=0)
    c2 = e + back(e, 1)
    c4 = back(c2, 1) + fwd(c2, 1)
    c8 = back(c4, 2) + fwd(c4, 2)
    c16 = back(c8, 4) + fwd(c8, 4)
    return c2, c4, c8, c16


def _back_kernel(x_ref, a_ref, pin_ref, pp_ref, pn_ref, mod_ref, wg_ref, poolw_ref, pscale_ref,
                 wua_ref, wup_ref, wo_ref, lng1_ref, lnb1_ref, w1_ref, w2_ref, lng2_ref, lnb2_ref,
                 o_ref, *, seq_len, alpha):
    tm = x_ref.shape[0]
    tile_start = pl.program_id(0) * tm
    shift = mod_ref[3:4, :]
    scale = mod_ref[4:5, :]
    gate = mod_ref[5:6, :]
    p_ext = jnp.concatenate([pp_ref[...], pin_ref[...], pn_ref[...]], axis=0)

    tiles = _sub_tiles(tm)
    xs = [x_ref[rows, :] for rows in tiles]

    gated = []
    pooled_up = []
    half = D_MODEL // 2
    for rows, x in zip(tiles, xs):
        h = _modulate_bf16(x, shift, scale)
        a_blk = a_ref[rows, :]
        big_operands = [
            (h, wg_ref, slice(0, D_MODEL)),
            (h, wg_ref, slice(D_MODEL, 2 * D_MODEL)),
            (a_blk, wua_ref, slice(0, half)),
            (a_blk, wua_ref, slice(half, D_MODEL)),
        ]
        big = []

        pos = (tile_start + rows.start) % seq_len
        e_all = p_ext[rows.start:rows.start + SUB_TILE + 2 * HALO, :]
        row = lax.broadcasted_iota(jnp.int32, (SUB_TILE + 2 * HALO, 1), 0)
        first_row = jnp.where(pos == 0, HALO, 0)
        end_row = jnp.where(pos + SUB_TILE == seq_len, HALO + SUB_TILE, SUB_TILE + 2 * HALO)
        e_all = jnp.where(jnp.logical_or(row < first_row, row >= end_row), 0.0, e_all)
        t = pos + lax.broadcasted_iota(jnp.int32, (SUB_TILE, 1), 0)
        pgs = []
        for g, w in enumerate(POOL_WINDOWS):
            lhs, w_ref, cols = big_operands[g]
            big.append(_dot(lhs, w_ref[:, cols]))
            sl = slice(g * POOL_GROUP_DIM, (g + 1) * POOL_GROUP_DIM)
            e = e_all[:, sl]
            win = _window_sums(e)[g][HALO:HALO + SUB_TILE]
            lo = jnp.maximum(t - w // 2, 0)
            hi = jnp.minimum(t + w // 2, seq_len)
            inv_cnt = 1.0 / (hi - lo).astype(F32)
            pooled = (win * inv_cnt - e[HALO:HALO + SUB_TILE]).astype(BF16)
            pgs.append(_dot(pooled, poolw_ref[g].astype(BF16)))

        ga, gp, au_lo, au_hi = big
        pg = jnp.concatenate(pgs, axis=1)
        gated.append((jax.nn.sigmoid(ga) * jnp.concatenate([au_lo, au_hi], axis=1), gp))
        pooled_up.append(_dot((pg * pscale_ref[...]).astype(BF16), wup_ref[...]))

    merged = [(ga_up + jax.nn.sigmoid(gp) * pu).astype(BF16)
              for (ga_up, gp), pu in zip(gated, pooled_up)]
    mixes = [_dot(m, wo_ref[...]) for m in merged]
    x2s = [_layer_norm(alpha * x + gate * mix, lng1_ref[...], lnb1_ref[...])
           for x, mix in zip(xs, mixes)]

    ys = _ffn_rows(x2s, mod_ref, 6, w1_ref, w2_ref, lng2_ref[...], lnb2_ref[...], alpha)
    for rows, y in zip(tiles, ys):
        o_ref[rows, :] = y


def _back(x2d, attn, pin, mod, mod_row_of_token, w_g, pool_w, pool_scale, w_up_attn, w_up_pool,
          w_out, ln_g1, ln_b1, w1, w2, ln_g2, ln_b2, *, seq_len, alpha):
    n_tok = x2d.shape[0]
    tm = TOKEN_TILE
    halo_per_tile = tm // HALO
    n_halo_blocks = n_tok // HALO
    row_vec = lambda v: v.reshape(1, D_MODEL)
    return pl.pallas_call(
        functools.partial(_back_kernel, seq_len=seq_len, alpha=alpha),
        grid=(n_tok // tm,),
        in_specs=[
            pl.BlockSpec((tm, D_MODEL), lambda i: (i, 0)),
            pl.BlockSpec((tm, ATTN_WIDTH), lambda i: (i, 0)),
            pl.BlockSpec((tm, POOL_WIDTH), lambda i: (i, 0)),
            pl.BlockSpec((HALO, POOL_WIDTH), lambda i: (jnp.maximum(i * halo_per_tile - 1, 0), 0)),
            pl.BlockSpec((HALO, POOL_WIDTH),
                         lambda i: (jnp.minimum((i + 1) * halo_per_tile, n_halo_blocks - 1), 0)),
            pl.BlockSpec((None, N_MOD, D_MODEL), lambda i: (mod_row_of_token(i * tm), 0, 0)),
            _resident(w_g.shape),
            _resident(pool_w.shape),
            _resident((1, POOL_WIDTH)),
            _resident(w_up_attn.shape),
            _resident(w_up_pool.shape),
            _resident(w_out.shape),
            _resident((1, D_MODEL)),
            _resident((1, D_MODEL)),
            _resident(w1.shape),
            _resident(w2.shape),
            _resident((1, D_MODEL)),
            _resident((1, D_MODEL)),
        ],
        out_specs=pl.BlockSpec((tm, D_MODEL), lambda i: (i, 0)),
        out_shape=jax.ShapeDtypeStruct((n_tok, D_MODEL), F32),
        compiler_params=_compiler_params(1),
        name="back",
    )(x2d, attn, pin, pin, pin, mod, w_g, pool_w, pool_scale.reshape(1, POOL_WIDTH),
      w_up_attn, w_up_pool, w_out, row_vec(ln_g1), row_vec(ln_b1), w1, w2,
      row_vec(ln_g2), row_vec(ln_b2))


def _rope_tables(n_tokens):
    t = np.arange(n_tokens)
    n_freq = HEAD_DIM // 4
    inv_freq = ROPE_THETA ** (-np.arange(n_freq, dtype=np.float64) / n_freq)
    ang = np.concatenate([(t // GRID_W)[:, None] * inv_freq, (t % GRID_W)[:, None] * inv_freq],
                         axis=-1)
    cos, sin = np.cos(ang), np.sin(ang)
    return (jnp.asarray(np.concatenate([cos, cos], axis=-1), F32),
            jnp.asarray(np.concatenate([-sin, sin], axis=-1), F32))


def kernel(x_prompt, x_sample, cache_k, cache_v, c, c_ctx, w_mod, b_mod, ln_g, ln_b, ffn1_w1, ffn1_w2, w_in, q_norm_g, k_norm_g, pool_w, pool_scale, w_up_attn, w_up_pool, w_out, ffn2_w1, ffn2_w2):
    batch, seq, _ = x_prompt.shape
    dec_batch, n_lat, _ = x_sample.shape
    depth = w_mod.shape[0]
    alpha = (2.0 * depth) ** 0.25
    assert seq == SUB_TILE and (batch * seq) % TOKEN_TILE == 0
    assert n_lat % TOKEN_TILE == 0 and n_lat % Q_TILE == 0 and n_lat % KV_CHUNK == 0
    assert 1 + dec_batch <= MOD_ROWS

    cos2, sin2 = _rope_tables((n_lat // GRID_W) * GRID_W)
    cond = jnp.concatenate(
        [c_ctx[None, :], c, jnp.zeros((MOD_ROWS - 1 - dec_batch, D_MODEL), F32)], axis=0)

    ctx_row = lambda t: 0
    lat_row = lambda t: 1 + t // n_lat

    yp = x_prompt.reshape(batch * seq, D_MODEL)
    ys = x_sample.reshape(dec_batch * n_lat, D_MODEL)
    new_k, new_v = [], []
    for l in range(depth):
        assert w_in.shape[2] == 2 * FRONT_WIDTH
        own = [(ffn1_w1[l], 0, 2 * D_FF), (ffn1_w2[l], 0, D_MODEL), (w_in[l], 0, FRONT_WIDTH)]
        riders = [(ffn2_w1[l], 0, 2 * D_FF), (ffn2_w2[l], 0, D_MODEL),
                  (w_in[l], 1, FRONT_WIDTH), (w_up_attn[l], 0, D_MODEL),
                  (w_up_pool[l], 0, D_MODEL), (w_out[l], 0, D_MODEL)]

        mod = _modulation_table(cond, w_mod[l], b_mod[l])

        (x1, attn, k_l, v_l, pin, w1b, w2b, w_g, w_ua, w_upl, w_o,
         w1a, w2a, w_front) = _front_ctx(
            yp, mod, ctx_row, ln_g[l, 0], ln_b[l, 0], q_norm_g[l], k_norm_g[l], own, riders,
            alpha=alpha)
        back = functools.partial(
            _back, w_g=w_g, pool_w=pool_w[l], pool_scale=pool_scale[l], w_up_attn=w_ua,
            w_up_pool=w_upl, w_out=w_o, ln_g1=ln_g[l, 1], ln_b1=ln_b[l, 1], w1=w1b, w2=w2b,
            ln_g2=ln_g[l, 2], ln_b2=ln_b[l, 2], alpha=alpha)
        yp = back(x1, attn, pin, mod, ctx_row, seq_len=seq)
        new_k.append(k_l.reshape(batch, seq, N_KV_HEADS, HEAD_DIM))
        new_v.append(v_l.reshape(batch, seq, N_KV_HEADS, HEAD_DIM))

        x1, q, kb, vb, pin = _front_lat(
            ys, mod, lat_row, w1a, w2a, ln_g[l, 0], ln_b[l, 0], w_front,
            q_norm_g[l], k_norm_g[l], cos2, sin2, n_lat=n_lat, alpha=alpha)
        past = cache_k.shape[2]
        attn = _attn_lat(q, kb, vb,
                         cache_k[:, l].reshape(dec_batch, past * N_KV_HEADS, HEAD_DIM),
                         cache_v[:, l].reshape(dec_batch, past * N_KV_HEADS, HEAD_DIM), n_lat)
        ys = back(x1, attn, pin, mod, lat_row, seq_len=n_lat)

    return (yp.reshape(batch, seq, D_MODEL),
            ys.reshape(dec_batch, n_lat, D_MODEL),
            jnp.stack(new_k, axis=1),
            jnp.stack(new_v, axis=1))
```

```python
import functools
import math

import numpy as np

import jax
import jax.numpy as jnp
from jax import lax
from jax.experimental import pallas as pl
from jax.experimental.pallas import tpu as pltpu

D_MODEL = 1024
GRID_W = 64
N_HEADS = 8
N_KV_HEADS = 2
HEAD_DIM = 128
GROUP = N_HEADS // N_KV_HEADS
ATTN_WIDTH = N_HEADS * HEAD_DIM
KV_WIDTH = N_KV_HEADS * HEAD_DIM
POOL_WINDOWS = (2, 4, 8, 16)
N_POOL_GROUPS = 4
POOL_GROUP_DIM = 128
POOL_WIDTH = N_POOL_GROUPS * POOL_GROUP_DIM
QKV_WIDTH = ATTN_WIDTH + 2 * KV_WIDTH
FRONT_WIDTH = QKV_WIDTH + POOL_WIDTH
D_FF = 2816
N_MOD = 9
ROPE_THETA = 10000.0
LN_EPS = 1e-6
RMS_EPS = 1e-6

SUBLANES = 8
MXU_DIM = 256
VMEM_LIMIT_BYTES = 58 * 1024 * 1024

HALO = SUBLANES
MOD_ROWS = SUBLANES
TOKEN_TILE = 512
SUB_TILE = 256
BF16_SUBLANES = 2 * SUBLANES
PRE_STEPS = 16
Q_TILE = 512
Q_SUB = 256
KV_CHUNK = 256
FF_CHUNKS = ((0, 768), (768, 1024), (1792, 1024))
Q_SCALE = HEAD_DIM ** -0.5 * math.log2(math.e)
V_AUG = MXU_DIM

BF16 = jnp.bfloat16
F32 = jnp.float32


def _dot(a, b):
    return jnp.dot(a, b, preferred_element_type=F32)


def _dot_nt(a, b):
    return lax.dot_general(a, b, (((1,), (1,)), ((), ())), preferred_element_type=F32)


def _layer_norm(x, g, b):
    mu = jnp.mean(x, axis=-1, keepdims=True)
    xc = x - mu
    var = jnp.mean(xc * xc, axis=-1, keepdims=True)
    return xc * lax.rsqrt(var + LN_EPS) * g + b


def _modulate_bf16(x, shift, scale):
    return (x * (1.0 + scale) + shift).astype(BF16)


def _resident(shape):
    zeros = (0,) * len(shape)
    return pl.BlockSpec(shape, lambda *_: zeros, pipeline_mode=pl.Buffered(1))


def _compiler_params(n_axes, independent_steps=False):
    return pltpu.CompilerParams(
        dimension_semantics=("parallel" if independent_steps else "arbitrary",) * n_axes,
        vmem_limit_bytes=VMEM_LIMIT_BYTES,
    )


def _mod_kernel(cond_ref, w_ref, b_ref, o_ref):
    c = cond_ref[...]
    s = (c * jax.nn.sigmoid(c)).astype(BF16)
    vec = _dot(s, w_ref[...].astype(BF16)) + b_ref[...]
    o_ref[:, pl.ds(pl.program_id(0), 1), :] = vec[:, None, :]


def _modulation_table(cond, w_mod, b_mod):
    n_out = w_mod.shape[1]
    assert n_out == N_MOD * D_MODEL
    return pl.pallas_call(
        _mod_kernel,
        grid=(N_MOD,),
        in_specs=[
            pl.BlockSpec((MOD_ROWS, D_MODEL), lambda j: (0, 0)),
            pl.BlockSpec((D_MODEL, D_MODEL), lambda j: (0, j)),
            pl.BlockSpec((1, D_MODEL), lambda j: (0, j)),
        ],
        out_specs=pl.BlockSpec((MOD_ROWS, N_MOD, D_MODEL), lambda j: (0, 0, 0)),
        out_shape=jax.ShapeDtypeStruct((MOD_ROWS, N_MOD, D_MODEL), F32),
        compiler_params=_compiler_params(1),
        name="modulation",
    )(cond, w_mod, b_mod.reshape(1, n_out))


def _ffn_rows(xs, mod_ref, mod_base, w1_ref, w2_ref, ln_g, ln_b, alpha):
    shift = mod_ref[mod_base:mod_base + 1, :]
    scale = mod_ref[mod_base + 1:mod_base + 2, :]
    gate = mod_ref[mod_base + 2:mod_base + 3, :]
    hs = [_modulate_bf16(x, shift, scale) for x in xs]
    fs = [None] * len(xs)
    for start, width in FF_CHUNKS:
        us = []
        for h in hs:
            a = _dot(h, w1_ref[:, start:start + width])
            b = _dot(h, w1_ref[:, D_FF + start:D_FF + start + width])
            us.append((a * jax.nn.sigmoid(a) * b).astype(BF16))
        for i, u in enumerate(us):
            part = _dot(u, w2_ref[start:start + width, :])
            fs[i] = part if fs[i] is None else fs[i] + part
    return [_layer_norm(alpha * x + (0.5 * gate) * f, ln_g, ln_b) for x, f in zip(xs, fs)]


def _sub_tiles(n_rows):
    return [slice(r0, r0 + SUB_TILE) for r0 in range(0, n_rows, SUB_TILE)]


def _rms_head(xh, g):
    ms = jnp.mean(xh * xh, axis=-1, keepdims=True)
    return xh * lax.rsqrt(ms + RMS_EPS) * g


def _rope(xh, cos2, sin2):
    return xh * cos2 + pltpu.roll(xh, HEAD_DIM // 2, axis=1) * sin2


def _ones_column_block(rows):
    col = lax.broadcasted_iota(jnp.int32, (rows, V_AUG - HEAD_DIM), 1)
    return jnp.where(col == 0, 1.0, 0.0).astype(BF16)


def _stack_heads(heads):
    return jnp.concatenate(heads, axis=0)


def _head_cols(hh):
    return slice(hh * HEAD_DIM, (hh + 1) * HEAD_DIM)


def _front_ctx_kernel(x_ref, mod_ref, lng_ref, lnb_ref, qg_ref, kg_ref, *refs,
                      n_own, n_riders, n_pre, alpha):
    own_in = refs[:n_own]
    rider_in = refs[n_own:n_own + n_riders]
    outs = refs[n_own + n_riders:]
    x1_ref, a_ref, k_ref, v_ref, pin_ref = outs[:5]
    rider_out = outs[5:5 + n_riders]
    own_out = outs[5 + n_riders:5 + n_riders + n_own]
    w1_ref, w2_ref, win_ref = outs[5 + n_riders + n_own:]
    step = pl.program_id(0)

    own_whole = (w1_ref, w2_ref, win_ref)

    def chunk_rows(ref, index):
        chunk = ref.shape[0]
        return pl.ds(pl.multiple_of(index * chunk, BF16_SUBLANES), chunk)

    @pl.when(step < n_pre)
    def _load_own_weights():
        for src, whole in zip(own_in, own_whole):
            whole[chunk_rows(src, step), :] = src[...].astype(BF16)

    @pl.when(step >= n_pre)
    def _token_step():
        _front_ctx_body(x_ref, mod_ref, w1_ref, w2_ref, lng_ref, lnb_ref, win_ref, qg_ref, kg_ref,
                        x1_ref, a_ref, k_ref, v_ref, pin_ref, alpha)
        for src, dst in zip(rider_in, rider_out):
            dst[...] = src[...].astype(BF16)
        for dst, whole in zip(own_out, own_whole):
            dst[...] = whole[chunk_rows(dst, step - n_pre), :]


def _front_ctx_body(x_ref, mod_ref, w1_ref, w2_ref, lng_ref, lnb_ref, win_ref, qg_ref, kg_ref,
                    x1_ref, a_ref, k_ref, v_ref, pin_ref, alpha):
    qg = qg_ref[...] * Q_SCALE
    kg = kg_ref[...]
    tiles = _sub_tiles(x_ref.shape[0])
    x1s = _ffn_rows([x_ref[rows, :] for rows in tiles], mod_ref, 0, w1_ref, w2_ref,
                    lng_ref[...], lnb_ref[...], alpha)
    gw = GROUP * HEAD_DIM
    scores = []
    for rows, x1 in zip(tiles, x1s):
        x1_ref[rows, :] = x1
        h = _modulate_bf16(x1, mod_ref[3:4, :], mod_ref[4:5, :])
        q_cols = [_dot(h, win_ref[:, :gw]), _dot(h, win_ref[:, gw:ATTN_WIDTH])]
        q4s = [_stack_heads([_rms_head(q_cols[0][:, _head_cols(g)], qg).astype(BF16)
                             for g in range(GROUP)])]
        kv_cols = _dot(h, win_ref[:, ATTN_WIDTH:QKV_WIDTH])
        q4s.append(_stack_heads([_rms_head(q_cols[1][:, _head_cols(g)], qg).astype(BF16)
                                 for g in range(GROUP)]))
        pin_ref[rows, :] = _dot(h, win_ref[:, QKV_WIDTH:])
        for kv in range(N_KV_HEADS):
            kn = _rms_head(kv_cols[:, _head_cols(kv)], kg)
            v_kv = kv_cols[:, KV_WIDTH + kv * HEAD_DIM:KV_WIDTH + (kv + 1) * HEAD_DIM]
            head_rows = pl.ds(N_KV_HEADS * rows.start + kv, SUB_TILE, stride=N_KV_HEADS)
            k_ref[head_rows, :] = kn
            v_ref[head_rows, :] = v_kv
            scores.append((rows, kv, _dot_nt(kn.astype(BF16), q4s[kv]), v_kv.T.astype(BF16)))
    probs = []
    for rows, kv, st, vt_kv in scores:
        p = jnp.exp2(st - jnp.max(st, axis=0, keepdims=True))
        probs.append((rows, kv, p.astype(BF16), jnp.sum(p, axis=0, keepdims=True), vt_kv))
    for rows, kv, p, l, vt_kv in probs:
        ot = _dot(vt_kv, p) / l
        for g in range(GROUP):
            a_ref[rows, _head_cols(kv * GROUP + g)] = (
                ot[:, g * SUB_TILE:(g + 1) * SUB_TILE].T.astype(BF16))


def _front_lat_kernel(x_ref, mod_ref, w1_ref, w2_ref, lng_ref, lnb_ref, win_ref, qg_ref, kg_ref,
                      cos_ref, sin_ref, x1_ref, q_ref, kb_ref, vb_ref, pin_ref, *, alpha):
    qg = qg_ref[...] * Q_SCALE
    kg = kg_ref[...]
    ones_col = _ones_column_block(SUB_TILE)
    tiles = _sub_tiles(x_ref.shape[0])
    x1s = _ffn_rows([x_ref[rows, :] for rows in tiles], mod_ref, 0, w1_ref, w2_ref,
                    lng_ref[...], lnb_ref[...], alpha)
    for rows, x1 in zip(tiles, x1s):
        x1_ref[rows, :] = x1
        proj = _dot(_modulate_bf16(x1, mod_ref[3:4, :], mod_ref[4:5, :]), win_ref[...])
        pin_ref[rows, :] = proj[:, QKV_WIDTH:]
        cos2 = cos_ref[rows, :]
        sin2 = sin_ref[rows, :]
        for hh in range(N_HEADS):
            q_ref[rows, _head_cols(hh)] = _rope(
                _rms_head(proj[:, _head_cols(hh)], qg), cos2, sin2).astype(BF16)
        for kv in range(N_KV_HEADS):
            kn = _rms_head(proj[:, ATTN_WIDTH + kv * HEAD_DIM:ATTN_WIDTH + (kv + 1) * HEAD_DIM], kg)
            kb_ref[rows, _head_cols(kv)] = _rope(kn, cos2, sin2).astype(BF16)
            v0 = ATTN_WIDTH + KV_WIDTH + kv * HEAD_DIM
            vb_ref[rows, kv * V_AUG:kv * V_AUG + HEAD_DIM] = proj[:, v0:v0 + HEAD_DIM].astype(BF16)
            vb_ref[rows, kv * V_AUG + HEAD_DIM:(kv + 1) * V_AUG] = ones_col


def _front_in_specs(tm, mod_row_of_token, w1, w2, w_front):
    return [
        pl.BlockSpec((tm, D_MODEL), lambda i: (i, 0)),
        pl.BlockSpec((None, N_MOD, D_MODEL), lambda i: (mod_row_of_token(i * tm), 0, 0)),
        _resident(w1.shape),
        _resident(w2.shape),
        _resident((1, D_MODEL)),
        _resident((1, D_MODEL)),
        _resident(w_front.shape),
        _resident((1, HEAD_DIM)),
        _resident((1, HEAD_DIM)),
    ]


def _front_ctx(x2d, mod, mod_row_of_token, ln_g, ln_b, q_g, k_g, own, riders, *, alpha):
    n_tok = x2d.shape[0]
    tm = TOKEN_TILE
    n_steps = n_tok // tm
    n_pre = PRE_STEPS
    tile_of = lambda i: jnp.maximum(i - n_pre, 0)
    pre_of = lambda i: jnp.minimum(i, n_pre - 1)
    tok = lambda width: pl.BlockSpec((tm, width), lambda i: (tile_of(i), 0))
    cache_spec = pl.BlockSpec((tm * N_KV_HEADS, HEAD_DIM), lambda i: (tile_of(i), 0))
    cache_shape = jax.ShapeDtypeStruct((n_tok * N_KV_HEADS, HEAD_DIM), F32)

    def chunked(weights, n_chunks, chunk_of):
        in_specs, out_specs, shapes = [], [], []
        for w, col_block, width in weights:
            chunk = w.shape[0] // n_chunks
            assert chunk * n_chunks == w.shape[0] and chunk % BF16_SUBLANES == 0
            in_specs.append(
                pl.BlockSpec((chunk, width), lambda i, cb=col_block: (chunk_of(i), cb)))
            out_specs.append(pl.BlockSpec((chunk, width), lambda i: (chunk_of(i), 0)))
            shapes.append(jax.ShapeDtypeStruct((w.shape[0], width), BF16))
        return in_specs, out_specs, shapes

    own_in, _, own_shapes = chunked(own, n_pre, pre_of)
    _, own_out, _ = chunked(own, n_steps, tile_of)
    rider_in, rider_out, rider_shapes = chunked(riders, n_steps, tile_of)
    return pl.pallas_call(
        functools.partial(_front_ctx_kernel, n_own=len(own), n_riders=len(riders), n_pre=n_pre,
                          alpha=alpha),
        grid=(n_pre + n_steps,),
        in_specs=[
            tok(D_MODEL),
            pl.BlockSpec((None, N_MOD, D_MODEL),
                         lambda i: (mod_row_of_token(tile_of(i) * tm), 0, 0)),
            _resident((1, D_MODEL)),
            _resident((1, D_MODEL)),
            _resident((1, HEAD_DIM)),
            _resident((1, HEAD_DIM)),
        ] + own_in + rider_in,
        out_specs=[tok(D_MODEL), tok(ATTN_WIDTH), cache_spec, cache_spec, tok(POOL_WIDTH)]
        + rider_out + own_out,
        out_shape=[
            jax.ShapeDtypeStruct((n_tok, D_MODEL), F32),
            jax.ShapeDtypeStruct((n_tok, ATTN_WIDTH), BF16),
            cache_shape,
            cache_shape,
            jax.ShapeDtypeStruct((n_tok, POOL_WIDTH), F32),
        ] + rider_shapes + own_shapes,
        scratch_shapes=[pltpu.VMEM(s.shape, BF16) for s in own_shapes],
        compiler_params=_compiler_params(1),
        name="front_ctx",
    )(x2d, mod, ln_g.reshape(1, D_MODEL), ln_b.reshape(1, D_MODEL),
      q_g.reshape(1, HEAD_DIM), k_g.reshape(1, HEAD_DIM),
      *[w for w, _, _ in own], *[w for w, _, _ in riders])


def _front_lat(x2d, mod, mod_row_of_token, w1, w2, ln_g, ln_b, w_front, q_g, k_g, cos2, sin2,
               *, n_lat, alpha):
    n_tok = x2d.shape[0]
    tm = TOKEN_TILE
    tiles_per_seq = n_lat // tm
    tok = lambda width: pl.BlockSpec((tm, width), lambda i: (i, 0))
    rope_spec = pl.BlockSpec((tm, HEAD_DIM), lambda i: (i % tiles_per_seq, 0))
    return pl.pallas_call(
        functools.partial(_front_lat_kernel, alpha=alpha),
        grid=(n_tok // tm,),
        in_specs=_front_in_specs(tm, mod_row_of_token, w1, w2, w_front) + [rope_spec, rope_spec],
        out_specs=[tok(D_MODEL), tok(ATTN_WIDTH), tok(KV_WIDTH), tok(N_KV_HEADS * V_AUG),
                   tok(POOL_WIDTH)],
        out_shape=[
            jax.ShapeDtypeStruct((n_tok, D_MODEL), F32),
            jax.ShapeDtypeStruct((n_tok, ATTN_WIDTH), BF16),
            jax.ShapeDtypeStruct((n_tok, KV_WIDTH), BF16),
            jax.ShapeDtypeStruct((n_tok, N_KV_HEADS * V_AUG), BF16),
            jax.ShapeDtypeStruct((n_tok, POOL_WIDTH), F32),
        ],
        compiler_params=_compiler_params(1, independent_steps=True),
        name="front_lat",
    )(x2d, mod, w1, w2, ln_g.reshape(1, D_MODEL), ln_b.reshape(1, D_MODEL), w_front,
      q_g.reshape(1, HEAD_DIM), k_g.reshape(1, HEAD_DIM), cos2, sin2)


def _attn_lat_kernel(q_ref, kc_ref, vc_ref, kl_ref, vl_ref, o_ref):
    n_lat = kl_ref.shape[0]
    past = kc_ref.shape[0] // N_KV_HEADS
    head_rows = pl.ds(pl.program_id(1), past, stride=N_KV_HEADS)
    vc = jnp.concatenate([vc_ref[head_rows, :].astype(BF16), _ones_column_block(past)], axis=1)
    chunks = [(kc_ref[head_rows, :].astype(BF16), vc)]
    for c0 in range(0, n_lat, KV_CHUNK):
        chunks.append((kl_ref[c0:c0 + KV_CHUNK, :], vl_ref[c0:c0 + KV_CHUNK, :]))
    for r0 in range(0, q_ref.shape[0], Q_SUB):
        rows = slice(r0, r0 + Q_SUB)
        q4 = _stack_heads([q_ref[rows, _head_cols(g)] for g in range(GROUP)])
        m = None
        acc = None
        for k_c, v_c in chunks:
            s = _dot_nt(q4, k_c)
            m_c = jnp.max(s, axis=-1, keepdims=True)
            if m is None:
                m = m_c
                acc = _dot(jnp.exp2(s - m).astype(BF16), v_c)
            else:
                m_new = jnp.maximum(m, m_c)
                acc = jnp.exp2(m - m_new) * acc + _dot(jnp.exp2(s - m_new).astype(BF16), v_c)
                m = m_new
        o = acc[:, :HEAD_DIM] / acc[:, HEAD_DIM:HEAD_DIM + 1]
        for g in range(GROUP):
            o_ref[rows, _head_cols(g)] = o[g * Q_SUB:(g + 1) * Q_SUB].astype(BF16)


def _attn_lat(q, kb, vb, cache_k, cache_v, n_lat):
    n_tok = q.shape[0]
    n_batch = n_tok // n_lat
    tq = Q_TILE
    q_tiles = n_lat // tq
    gw = GROUP * HEAD_DIM
    cache_spec = pl.BlockSpec((None,) + cache_k.shape[1:], lambda b, kv, i: (b, 0, 0))
    return pl.pallas_call(
        _attn_lat_kernel,
        grid=(n_batch, N_KV_HEADS, q_tiles),
        in_specs=[
            pl.BlockSpec((tq, gw), lambda b, kv, i: (b * q_tiles + i, kv)),
            cache_spec,
            cache_spec,
            pl.BlockSpec((n_lat, HEAD_DIM), lambda b, kv, i: (b, kv)),
            pl.BlockSpec((n_lat, V_AUG), lambda b, kv, i: (b, kv)),
        ],
        out_specs=pl.BlockSpec((tq, gw), lambda b, kv, i: (b * q_tiles + i, kv)),
        out_shape=jax.ShapeDtypeStruct((n_tok, ATTN_WIDTH), BF16),
        compiler_params=_compiler_params(3, independent_steps=True),
        name="attn_lat",
    )(q, cache_k, cache_v, kb, vb)


def _window_sums(e):
    n = e.shape[0]
    back = lambda a, k: pltpu.roll(a, k, axis=0)
    fwd = lambda a, k: pltpu.roll(a, n - k, axis=0)
    c2 = e + back(e, 1)
    c4 = back(c2, 1) + fwd(c2, 1)
    c8 = back(c4, 2) + fwd(c4, 2)
    c16 = back(c8, 4) + fwd(c8, 4)
    return c2, c4, c8, c16


def _back_kernel(x_ref, a_ref, pin_ref, pp_ref, pn_ref, mod_ref, wg_ref, poolw_ref, pscale_ref,
                 wua_ref, wup_ref, wo_ref, lng1_ref, lnb1_ref, w1_ref, w2_ref, lng2_ref, lnb2_ref,
                 o_ref, *, seq_len, alpha):
    tm = x_ref.shape[0]
    tile_start = pl.program_id(0) * tm
    shift = mod_ref[3:4, :]
    scale = mod_ref[4:5, :]
    gate = mod_ref[5:6, :]
    p_ext = jnp.concatenate([pp_ref[...], pin_ref[...], pn_ref[...]], axis=0)

    tiles = _sub_tiles(tm)
    xs = [x_ref[rows, :] for rows in tiles]

    gated = []
    pooled_up = []
    half = D_MODEL // 2
    for rows, x in zip(tiles, xs):
        h = _modulate_bf16(x, shift, scale)
        a_blk = a_ref[rows, :]
        big_operands = [
            (h, wg_ref, slice(0, D_MODEL)),
            (h, wg_ref, slice(D_MODEL, 2 * D_MODEL)),
            (a_blk, wua_ref, slice(0, half)),
            (a_blk, wua_ref, slice(half, D_MODEL)),
        ]
        big = []

        pos = (tile_start + rows.start) % seq_len
        e_all = p_ext[rows.start:rows.start + SUB_TILE + 2 * HALO, :]
        row = lax.broadcasted_iota(jnp.int32, (SUB_TILE + 2 * HALO, 1), 0)
        first_row = jnp.where(pos == 0, HALO, 0)
        end_row = jnp.where(pos + SUB_TILE == seq_len, HALO + SUB_TILE, SUB_TILE + 2 * HALO)
        e_all = jnp.where(jnp.logical_or(row < first_row, row >= end_row), 0.0, e_all)
        t = pos + lax.broadcasted_iota(jnp.int32, (SUB_TILE, 1), 0)
        pgs = []
        for g, w in enumerate(POOL_WINDOWS):
            lhs, w_ref, cols = big_operands[g]
            big.append(_dot(lhs, w_ref[:, cols]))
            sl = slice(g * POOL_GROUP_DIM, (g + 1) * POOL_GROUP_DIM)
            e = e_all[:, sl]
            win = _window_sums(e)[g][HALO:HALO + SUB_TILE]
            lo = jnp.maximum(t - w // 2, 0)
            hi = jnp.minimum(t + w // 2, seq_len)
            inv_cnt = 1.0 / (hi - lo).astype(F32)
            pooled = (win * inv_cnt - e[HALO:HALO + SUB_TILE]).astype(BF16)
            pgs.append(_dot(pooled, poolw_ref[g].astype(BF16)))

        ga, gp, au_lo, au_hi = big
        pg = jnp.concatenate(pgs, axis=1)
        gated.append((jax.nn.sigmoid(ga) * jnp.concatenate([au_lo, au_hi], axis=1), gp))
        pooled_up.append(_dot((pg * pscale_ref[...]).astype(BF16), wup_ref[...]))

    merged = [(ga_up + jax.nn.sigmoid(gp) * pu).astype(BF16)
              for (ga_up, gp), pu in zip(gated, pooled_up)]
    mixes = [_dot(m, wo_ref[...]) for m in merged]
    x2s = [_layer_norm(alpha * x + gate * mix, lng1_ref[...], lnb1_ref[...])
           for x, mix in zip(xs, mixes)]

    ys = _ffn_rows(x2s, mod_ref, 6, w1_ref, w2_ref, lng2_ref[...], lnb2_ref[...], alpha)
    for rows, y in zip(tiles, ys):
        o_ref[rows, :] = y


def _back(x2d, attn, pin, mod, mod_row_of_token, w_g, pool_w, pool_scale, w_up_attn, w_up_pool,
          w_out, ln_g1, ln_b1, w1, w2, ln_g2, ln_b2, *, seq_len, alpha):
    n_tok = x2d.shape[0]
    tm = TOKEN_TILE
    halo_per_tile = tm // HALO
    n_halo_blocks = n_tok // HALO
    row_vec = lambda v: v.reshape(1, D_MODEL)
    return pl.pallas_call(
        functools.partial(_back_kernel, seq_len=seq_len, alpha=alpha),
        grid=(n_tok // tm,),
        in_specs=[
            pl.BlockSpec((tm, D_MODEL), lambda i: (i, 0)),
            pl.BlockSpec((tm, ATTN_WIDTH), lambda i: (i, 0)),
            pl.BlockSpec((tm, POOL_WIDTH), lambda i: (i, 0)),
            pl.BlockSpec((HALO, POOL_WIDTH), lambda i: (jnp.maximum(i * halo_per_tile - 1, 0), 0)),
            pl.BlockSpec((HALO, POOL_WIDTH),
                         lambda i: (jnp.minimum((i + 1) * halo_per_tile, n_halo_blocks - 1), 0)),
            pl.BlockSpec((None, N_MOD, D_MODEL), lambda i: (mod_row_of_token(i * tm), 0, 0)),
            _resident(w_g.shape),
            _resident(pool_w.shape),
            _resident((1, POOL_WIDTH)),
            _resident(w_up_attn.shape),
            _resident(w_up_pool.shape),
            _resident(w_out.shape),
            _resident((1, D_MODEL)),
            _resident((1, D_MODEL)),
            _resident(w1.shape),
            _resident(w2.shape),
            _resident((1, D_MODEL)),
            _resident((1, D_MODEL)),
        ],
        out_specs=pl.BlockSpec((tm, D_MODEL), lambda i: (i, 0)),
        out_shape=jax.ShapeDtypeStruct((n_tok, D_MODEL), F32),
        compiler_params=_compiler_params(1, independent_steps=True),
        name="back",
    )(x2d, attn, pin, pin, pin, mod, w_g, pool_w, pool_scale.reshape(1, POOL_WIDTH),
      w_up_attn, w_up_pool, w_out, row_vec(ln_g1), row_vec(ln_b1), w1, w2,
      row_vec(ln_g2), row_vec(ln_b2))


def _rope_tables(n_tokens):
    t = np.arange(n_tokens)
    n_freq = HEAD_DIM // 4
    inv_freq = ROPE_THETA ** (-np.arange(n_freq, dtype=np.float64) / n_freq)
    ang = np.concatenate([(t // GRID_W)[:, None] * inv_freq, (t % GRID_W)[:, None] * inv_freq],
                         axis=-1)
    cos, sin = np.cos(ang), np.sin(ang)
    return (jnp.asarray(np.concatenate([cos, cos], axis=-1), F32),
            jnp.asarray(np.concatenate([-sin, sin], axis=-1), F32))


def kernel(x_prompt, x_sample, cache_k, cache_v, c, c_ctx, w_mod, b_mod, ln_g, ln_b, ffn1_w1, ffn1_w2, w_in, q_norm_g, k_norm_g, pool_w, pool_scale, w_up_attn, w_up_pool, w_out, ffn2_w1, ffn2_w2):
    batch, seq, _ = x_prompt.shape
    dec_batch, n_lat, _ = x_sample.shape
    depth = w_mod.shape[0]
    alpha = (2.0 * depth) ** 0.25
    assert seq == SUB_TILE and (batch * seq) % TOKEN_TILE == 0
    assert n_lat % TOKEN_TILE == 0 and n_lat % Q_TILE == 0 and n_lat % KV_CHUNK == 0
    assert 1 + dec_batch <= MOD_ROWS

    cos2, sin2 = _rope_tables((n_lat // GRID_W) * GRID_W)
    cond = jnp.concatenate(
        [c_ctx[None, :], c, jnp.zeros((MOD_ROWS - 1 - dec_batch, D_MODEL), F32)], axis=0)

    ctx_row = lambda t: 0
    lat_row = lambda t: 1 + t // n_lat

    yp = x_prompt.reshape(batch * seq, D_MODEL)
    ys = x_sample.reshape(dec_batch * n_lat, D_MODEL)
    new_k, new_v = [], []
    for l in range(depth):
        assert w_in.shape[2] == 2 * FRONT_WIDTH
        own = [(ffn1_w1[l], 0, 2 * D_FF), (ffn1_w2[l], 0, D_MODEL), (w_in[l], 0, FRONT_WIDTH)]
        riders = [(ffn2_w1[l], 0, 2 * D_FF), (ffn2_w2[l], 0, D_MODEL),
                  (w_in[l], 1, FRONT_WIDTH), (w_up_attn[l], 0, D_MODEL),
                  (w_up_pool[l], 0, D_MODEL), (w_out[l], 0, D_MODEL)]

        mod = _modulation_table(cond, w_mod[l], b_mod[l])

        (x1, attn, k_l, v_l, pin, w1b, w2b, w_g, w_ua, w_upl, w_o,
         w1a, w2a, w_front) = _front_ctx(
            yp, mod, ctx_row, ln_g[l, 0], ln_b[l, 0], q_norm_g[l], k_norm_g[l], own, riders,
            alpha=alpha)
        back = functools.partial(
            _back, w_g=w_g, pool_w=pool_w[l], pool_scale=pool_scale[l], w_up_attn=w_ua,
            w_up_pool=w_upl, w_out=w_o, ln_g1=ln_g[l, 1], ln_b1=ln_b[l, 1], w1=w1b, w2=w2b,
            ln_g2=ln_g[l, 2], ln_b2=ln_b[l, 2], alpha=alpha)
        yp = back(x1, attn, pin, mod, ctx_row, seq_len=seq)
        new_k.append(k_l.reshape(batch, seq, N_KV_HEADS, HEAD_DIM))
        new_v.append(v_l.reshape(batch, seq, N_KV_HEADS, HEAD_DIM))

        x1, q, kb, vb, pin = _front_lat(
            ys, mod, lat_row, w1a, w2a, ln_g[l, 0], ln_b[l, 0], w_front,
            q_norm_g[l], k_norm_g[l], cos2, sin2, n_lat=n_lat, alpha=alpha)
        past = cache_k.shape[2]
        attn = _attn_lat(q, kb, vb,
                         cache_k[:, l].reshape(dec_batch, past * N_KV_HEADS, HEAD_DIM),
                         cache_v[:, l].reshape(dec_batch, past * N_KV_HEADS, HEAD_DIM), n_lat)
        ys = back(x1, attn, pin, mod, lat_row, seq_len=n_lat)

    return (yp.reshape(batch, seq, D_MODEL),
            ys.reshape(dec_batch, n_lat, D_MODEL),
            jnp.stack(new_k, axis=1),
            jnp.stack(new_v, axis=1))
```
